```python
import jax, jax.numpy as jnp
from jax import lax
import numpy as np


D_MODEL = 1024
BATCH = 8
SEQ = 2048
DEPTH = 2

N_MEM = 256
RWKV_HEADS = 8
RWKV_HEAD_DIM = 64
RWKV_W = RWKV_HEADS * RWKV_HEAD_DIM
W_LORA = 64
A_LORA = 64
G_LORA = 128
RWKV_IN = 3 * RWKV_W + W_LORA + A_LORA + G_LORA
RWKV_LN_EPS = RWKV_HEAD_DIM * 1e-5
LRU_BLOCKS = 8
LRU_W = 512
LRU_BLOCK_DIM = LRU_W // LRU_BLOCKS
CONV_WIDTH = 4
LRU_C = 8.0
MLA_HEADS = 8
MLA_NOPE = 64
MLA_ROPE = 32
MLA_QK = MLA_NOPE + MLA_ROPE
MLA_V = 64
Q_RANK = 256
KV_RANK = 128
ROPE_THETA = 10000.0
Q_BLOCK = 128
N_BRANCH = 3
BRANCH_W = 512
XA_HEADS = 4
XA_HEAD_DIM = 128
XA_W = XA_HEADS * XA_HEAD_DIM
D_FF = -(-8 * D_MODEL // (3 * 256)) * 256
LRU_OFF = RWKV_IN
MLA_OFF = LRU_OFF + 2 * LRU_W
GATE_OFF = MLA_OFF + Q_RANK + KV_RANK + MLA_ROPE
D_IN = GATE_OFF + N_BRANCH * D_MODEL

kernel_name = 'hybrid_rwkv7_rglru_mla_gated_block'


def rms_norm(x, g, eps=1e-6):
    xf = x.astype(jnp.float32)
    y = xf * lax.rsqrt(jnp.mean(xf * xf, axis=-1, keepdims=True) + eps)
    return (y * g.astype(jnp.float32)).astype(x.dtype)


def rope(x, cos, sin):
    x1, x2 = jnp.split(x, 2, axis=-1)
    return jnp.concatenate([x1 * cos - x2 * sin, x2 * cos + x1 * sin], axis=-1)


def token_shift(p):
    return jnp.pad(p, ((0, 0), (1, 0), (0, 0)))[:, :-1]


def rwkv7_scan(r, w, k, v, kk, kka):
    B, S, H, N = r.shape

    def step(state, inp):
        r_t, w_t, k_t, v_t, kk_t, kka_t = inp
        sa = jnp.einsum('bhvk,bhk->bhv', state, kk_t)
        state = (state * w_t[:, :, None, :] - sa[..., None] * kka_t[:, :, None, :]
                 + v_t[..., None] * k_t[:, :, None, :])
        return state, jnp.einsum('bhvk,bhk->bhv', state, r_t)

    xs = tuple(jnp.swapaxes(t, 0, 1) for t in (r, w, k, v, kk, kka))
    s0 = jnp.zeros((B, H, N, N), jnp.float32)
    _, ys = lax.scan(step, s0, xs)
    return jnp.swapaxes(ys, 0, 1)


def rwkv7_mix(p, mu, w0, w_up, a0, a_up, g_up, k_k, k_a, r_k, ln_g, ln_b):
    B, S, _ = p.shape
    p = p + (token_shift(p) - p) * mu
    o1, o2, o3 = RWKV_W, 2 * RWKV_W, 3 * RWKV_W
    r, k, v, wd, ad, gd = jnp.split(p, [o1, o2, o3, o3 + W_LORA, o3 + W_LORA + A_LORA], axis=-1)
    w_raw = -jax.nn.softplus(-(w0 + jnp.tanh(wd) @ w_up).astype(jnp.float32)) - 0.5
    decay = jnp.exp(-jnp.exp(w_raw))
    a = jax.nn.sigmoid(a0 + ad @ a_up)
    g = jax.nn.sigmoid(gd) @ g_up
    heads = lambda t: t.reshape(B, S, RWKV_HEADS, RWKV_HEAD_DIM)
    kk = heads(k * k_k).astype(jnp.float32)
    kk = kk / jnp.maximum(jnp.sqrt(jnp.sum(kk * kk, axis=-1, keepdims=True)), 1e-12)
    k = k * (1.0 + (a - 1.0) * k_a)
    r, k, v, decay, a = heads(r), heads(k), heads(v), heads(decay), heads(a)
    y = rwkv7_scan(r, decay, k, v, kk, kk * a)
    mean = jnp.mean(y, axis=-1, keepdims=True)
    var = jnp.mean(jnp.square(y - mean), axis=-1, keepdims=True)
    y = ((y - mean) * lax.rsqrt(var + RWKV_LN_EPS)).reshape(B, S, RWKV_W) * ln_g + ln_b
    bonus = jnp.sum(r * k * r_k, axis=-1, keepdims=True) * v
    y = y + bonus.reshape(B, S, RWKV_W)
    return (y * g).astype(p.dtype)


def _lin_combine(e1, e2):
    a1, b1 = e1
    a2, b2 = e2
    return a1 * a2, a2 * b1 + b2


def rglru_mix(xb, gb, conv_w, conv_b, wa, ba, wx, bx, lam):
    B, S, _ = xb.shape
    xc = lax.conv_general_dilated(xb, conv_w[:, None, :].astype(xb.dtype), window_strides=(1,),
                                  padding=[(CONV_WIDTH - 1, 0)],
                                  dimension_numbers=('NWC', 'WIO', 'NWC'),
                                  feature_group_count=LRU_W) + conv_b
    xh = xc.reshape(B, S, LRU_BLOCKS, LRU_BLOCK_DIM)
    r = jax.nn.sigmoid(jnp.einsum('bsni,nij->bsnj', xh, wa).reshape(B, S, LRU_W) + ba)
    i = jax.nn.sigmoid(jnp.einsum('bsni,nij->bsnj', xh, wx).reshape(B, S, LRU_W) + bx)
    log_a = -LRU_C * r.astype(jnp.float32) * jax.nn.softplus(-lam.astype(jnp.float32))
    a = jnp.exp(log_a)
    u = jnp.sqrt(-jnp.expm1(2.0 * log_a)) * (i * xc)
    _, h = lax.associative_scan(_lin_combine, (a, u), axis=1)
    return (h * jax.nn.gelu(gb)).astype(xb.dtype)


def causal_attention(q, k, v):
    B, S, H, Dk = q.shape
    nb = S // Q_BLOCK
    scale = Dk ** -0.5
    qb = q.reshape(B, nb, Q_BLOCK, H, Dk).transpose(1, 0, 2, 3, 4)
    kpos = jnp.arange(S)

    def block(args):
        qi, bi = args
        qpos = bi * Q_BLOCK + jnp.arange(Q_BLOCK)
        s = jnp.einsum('bqhd,bkhd->bhqk', qi, k, preferred_element_type=jnp.float32) * scale
        s = jnp.where(kpos[None, :] <= qpos[:, None], s, -jnp.inf)
        pr = jax.nn.softmax(s, axis=-1)
        return jnp.einsum('bhqk,bkhd->bqhd', pr.astype(v.dtype), v)

    o = lax.map(block, (qb, jnp.arange(nb)))
    return o.transpose(1, 0, 2, 3, 4).reshape(B, S, H, v.shape[-1])


def mla_mix(cq, ckv, kr, cos, sin, q_norm, w_uq, kv_norm, w_ukv, q_gain, k_gain):
    B, S, _ = cq.shape
    q = (rms_norm(cq, q_norm) @ w_uq).reshape(B, S, MLA_HEADS, MLA_QK)
    kv = (rms_norm(ckv, kv_norm) @ w_ukv).reshape(B, S, MLA_HEADS, MLA_NOPE + MLA_V)
    k_nope, v = jnp.split(kv, [MLA_NOPE], axis=-1)
    k = jnp.concatenate([k_nope, jnp.broadcast_to(kr[:, :, None, :], (B, S, MLA_HEADS, MLA_ROPE))], axis=-1)
    q = rms_norm(q, q_gain)
    k = rms_norm(k, k_gain)
    q = jnp.concatenate([q[..., :MLA_NOPE], rope(q[..., MLA_NOPE:], cos, sin).astype(q.dtype)], axis=-1)
    k = jnp.concatenate([k[..., :MLA_NOPE], rope(k[..., MLA_NOPE:], cos, sin).astype(k.dtype)], axis=-1)
    return causal_attention(q, k, v).reshape(B, S, MLA_HEADS * MLA_V)


def memory_xattn(h, m, w_q, w_kv, q_gain, k_gain, w_o):
    B, S, _ = h.shape
    M = m.shape[1]
    q = rms_norm((h @ w_q).reshape(B, S, XA_HEADS, XA_HEAD_DIM), q_gain)
    kv = (m @ w_kv).reshape(B, M, XA_HEADS, 2 * XA_HEAD_DIM)
    k, v = jnp.split(kv, 2, axis=-1)
    k = rms_norm(k, k_gain)
    s = jnp.einsum('bqhd,bkhd->bhqk', q, k, preferred_element_type=jnp.float32) * XA_HEAD_DIM ** -0.5
    pr = jax.nn.softmax(s, axis=-1)
    o = jnp.einsum('bhqk,bkhd->bqhd', pr.astype(v.dtype), v).reshape(B, S, XA_W)
    return o @ w_o


def swiglu(h, w1, w3, w2):
    return (jax.nn.silu(h @ w1) * (h @ w3)) @ w2


def setup_inputs(seed: int = 0) -> dict:
    key = jax.random.key(seed)
    ks = iter(jax.random.split(key, 64))

    def nrm(shape, scale):
        return jax.random.normal(next(ks), shape, jnp.float32) * scale

    def gain(shape):
        return 1.0 + nrm(shape, 0.1)

    L, D = DEPTH, D_MODEL
    x = nrm((BATCH, SEQ, D), 1.0)
    mem = nrm((BATCH, N_MEM, D), 1.0)
    positions = (jnp.arange(SEQ, dtype=jnp.int32)[None, :]
                 + jax.random.randint(next(ks), (BATCH, 1), 0, 4096, dtype=jnp.int32))
    a_c = jax.random.uniform(next(ks), (L, LRU_W), jnp.float32, 0.9, 0.999)
    a_base = a_c ** (1.0 / LRU_C)
    lru_lambda = jnp.log(a_base) - jnp.log1p(-a_base)
    return {
        'x': x,
        'mem': mem,
        'positions': positions,
        'norm_mix': gain((L, D)),
        'norm_xattn': gain((L, D)),
        'norm_mem': gain((L, D)),
        'norm_ffn': gain((L, D)),
        'w_in': nrm((L, D, D_IN), D ** -0.5),
        'b_gate': nrm((L, N_BRANCH * D), 0.1),
        'rwkv_mu': jax.random.uniform(next(ks), (L, RWKV_IN), jnp.float32),
        'rwkv_w0': nrm((L, RWKV_W), 0.5) - 0.5,
        'rwkv_w_up': nrm((L, W_LORA, RWKV_W), 0.1),
        'rwkv_a0': nrm((L, RWKV_W), 0.5),
        'rwkv_a_up': nrm((L, A_LORA, RWKV_W), A_LORA ** -0.5),
        'rwkv_g_up': nrm((L, G_LORA, RWKV_W), G_LORA ** -0.5),
        'rwkv_k_k': gain((L, RWKV_W)),
        'rwkv_k_a': gain((L, RWKV_W)),
        'rwkv_r_k': nrm((L, RWKV_HEADS, RWKV_HEAD_DIM), 0.1),
        'rwkv_ln_g': gain((L, RWKV_W)),
        'rwkv_ln_b': nrm((L, RWKV_W), 0.01),
        'lru_conv_w': nrm((L, CONV_WIDTH, LRU_W), CONV_WIDTH ** -0.5),
        'lru_conv_b': nrm((L, LRU_W), 0.01),
        'lru_wa': nrm((L, LRU_BLOCKS, LRU_BLOCK_DIM, LRU_BLOCK_DIM), LRU_BLOCK_DIM ** -0.5),
        'lru_ba': nrm((L, LRU_W), 0.01),
        'lru_wx': nrm((L, LRU_BLOCKS, LRU_BLOCK_DIM, LRU_BLOCK_DIM), LRU_BLOCK_DIM ** -0.5),
        'lru_bx': nrm((L, LRU_W), 0.01),
        'lru_lambda': lru_lambda,
        'mla_q_norm': gain((L, Q_RANK)),
        'mla_w_uq': nrm((L, Q_RANK, MLA_HEADS * MLA_QK), Q_RANK ** -0.5),
        'mla_kv_norm': gain((L, KV_RANK)),
        'mla_w_ukv': nrm((L, KV_RANK, MLA_HEADS * (MLA_NOPE + MLA_V)), KV_RANK ** -0.5),
        'mla_q_gain': gain((L, MLA_QK)),
        'mla_k_gain': gain((L, MLA_QK)),
        'w_branch': nrm((L, N_BRANCH, BRANCH_W, D), BRANCH_W ** -0.5),
        'w_out': nrm((L, D, D), D ** -0.5),
        'xa_w_q': nrm((L, D, XA_W), D ** -0.5),
        'xa_w_kv': nrm((L, D, 2 * XA_W), D ** -0.5),
        'xa_q_gain': gain((L, XA_HEAD_DIM)),
        'xa_k_gain': gain((L, XA_HEAD_DIM)),
        'xa_w_o': nrm((L, XA_W, D), XA_W ** -0.5),
        'ffn_w1': nrm((L, D, D_FF), D ** -0.5),
        'ffn_w3': nrm((L, D, D_FF), D ** -0.5),
        'ffn_w2': nrm((L, D_FF, D), D_FF ** -0.5),
    }


def reference(x, mem, positions, norm_mix, norm_xattn, norm_mem, norm_ffn, w_in, b_gate,
              rwkv_mu, rwkv_w0, rwkv_w_up, rwkv_a0, rwkv_a_up, rwkv_g_up, rwkv_k_k, rwkv_k_a,
              rwkv_r_k, rwkv_ln_g, rwkv_ln_b, lru_conv_w, lru_conv_b, lru_wa, lru_ba, lru_wx,
              lru_bx, lru_lambda, mla_q_norm, mla_w_uq, mla_kv_norm, mla_w_ukv, mla_q_gain,
              mla_k_gain, w_branch, w_out, xa_w_q, xa_w_kv, xa_q_gain, xa_k_gain, xa_w_o,
              ffn_w1, ffn_w3, ffn_w2):
    B, S, _ = x.shape
    inv_freq = ROPE_THETA ** (-jnp.arange(0, MLA_ROPE, 2, dtype=jnp.float32) / MLA_ROPE)
    ang = positions.astype(jnp.float32)[..., None] * inv_freq
    cos = jnp.cos(ang)[:, :, None, :]
    sin = jnp.sin(ang)[:, :, None, :]
    for l in range(DEPTH):
        h = rms_norm(x, norm_mix[l])
        p = h @ w_in[l]
        y_a = rwkv7_mix(p[..., :RWKV_IN], rwkv_mu[l], rwkv_w0[l], rwkv_w_up[l], rwkv_a0[l],
                        rwkv_a_up[l], rwkv_g_up[l], rwkv_k_k[l], rwkv_k_a[l], rwkv_r_k[l],
                        rwkv_ln_g[l], rwkv_ln_b[l])
        y_b = rglru_mix(p[..., LRU_OFF:LRU_OFF + LRU_W], p[..., LRU_OFF + LRU_W:MLA_OFF],
                        lru_conv_w[l], lru_conv_b[l], lru_wa[l], lru_ba[l], lru_wx[l], lru_bx[l],
                        lru_lambda[l])
        y_c = mla_mix(p[..., MLA_OFF:MLA_OFF + Q_RANK],
                      p[..., MLA_OFF + Q_RANK:MLA_OFF + Q_RANK + KV_RANK],
                      p[..., MLA_OFF + Q_RANK + KV_RANK:GATE_OFF], cos, sin,
                      mla_q_norm[l], mla_w_uq[l], mla_kv_norm[l], mla_w_ukv[l],
                      mla_q_gain[l], mla_k_gain[l])
        gates = jax.nn.sigmoid(p[..., GATE_OFF:] + b_gate[l]).reshape(B, S, N_BRANCH, D_MODEL)
        branches = jnp.stack([y_a, y_b, y_c], axis=2)
        proj = jnp.einsum('bsnc,ncd->bsnd', branches, w_branch[l])
        merged = jnp.sum(gates * proj, axis=2)
        x = x + merged @ w_out[l]
        x = x + memory_xattn(rms_norm(x, norm_xattn[l]), rms_norm(mem, norm_mem[l]),
                             xa_w_q[l], xa_w_kv[l], xa_q_gain[l], xa_k_gain[l], xa_w_o[l])
        x = x + swiglu(rms_norm(x, norm_ffn[l]), ffn_w1[l], ffn_w3[l], ffn_w2[l])
    return x
```

```python
import functools
import math

import jax
import jax.numpy as jnp
from jax import lax
from jax.experimental import pallas as pl
from jax.experimental.pallas import tpu as pltpu

F32 = jnp.float32
BF16 = jnp.bfloat16
HIGHEST = lax.Precision.HIGHEST

D_MODEL = 1024
N_MEM = 256
RWKV_HEADS = 8
RWKV_HEAD_DIM = 64
RWKV_W = RWKV_HEADS * RWKV_HEAD_DIM
W_LORA = 64
A_LORA = 64
G_LORA = 128
RWKV_IN = 3 * RWKV_W + W_LORA + A_LORA + G_LORA
RWKV_LN_EPS = RWKV_HEAD_DIM * 1e-5
LRU_BLOCKS = 8
LRU_W = 512
CONV_WIDTH = 4
LRU_C = 8.0
MLA_HEADS = 8
MLA_NOPE = 64
MLA_ROPE = 32
MLA_QK = MLA_NOPE + MLA_ROPE
MLA_V = 64
Q_RANK = 256
KV_RANK = 128
ROPE_THETA = 10000.0
N_BRANCH = 3
BRANCH_W = 512
XA_HEADS = 4
XA_HEAD_DIM = 128
XA_W = XA_HEADS * XA_HEAD_DIM
D_FF = -(-8 * D_MODEL // (3 * 256)) * 256
LRU_OFF = RWKV_IN
MLA_OFF = LRU_OFF + 2 * LRU_W
GATE_OFF = MLA_OFF + Q_RANK + KV_RANK + MLA_ROPE

LANES = 128
SUBLANES = 8
VMEM_LIMIT = 56 * 1024 * 1024
MLA_PAD = 4 * LANES
MLA_HEAD_PAD = LANES
RWKV_CHUNK = 64
INV_BLOCK = 16
TOK_TILE = 512
ATT_TILE = 256


def _params(*sem):
    return pltpu.CompilerParams(dimension_semantics=sem, vmem_limit_bytes=VMEM_LIMIT)


def _dot(a, b):
    return jnp.dot(a.astype(BF16), b.astype(BF16), preferred_element_type=F32)


def _dot_nt(a, b):
    return lax.dot_general(a.astype(BF16), b.astype(BF16), (((1,), (1,)), ((), ())),
                           preferred_element_type=F32)


def _sigmoid(x):
    return 1.0 / (1.0 + jnp.exp(-x))


def _rms(x, g, eps=1e-6):
    return x * lax.rsqrt(jnp.mean(x * x, axis=-1, keepdims=True) + eps) * g


def _segsum(x, bd):
    hi = x.astype(BF16)
    lo = (x - hi.astype(F32)).astype(BF16)
    return (jnp.dot(hi, bd, preferred_element_type=F32)
            + jnp.dot(lo, bd, preferred_element_type=F32))


def _full(shape):
    n = len(shape)
    return pl.BlockSpec(shape, lambda *_: (0,) * n)


def _in_proj_kernel(x_ref, g_ref, wr_ref, wl_ref, wm_ref, or_ref, ol_ref, om_ref):
    h = _rms(x_ref[...], g_ref[...]).astype(BF16)
    or_ref[...] = jnp.dot(h, wr_ref[...], preferred_element_type=F32)
    ol_ref[...] = jnp.dot(h, wl_ref[...], preferred_element_type=F32)
    om_ref[...] = jnp.dot(h, wm_ref[...], preferred_element_type=F32)


def _in_proj(x2, g, w_rwkv, w_lru, w_mla):
    T = x2.shape[0]
    tm = min(TOK_TILE, T)
    row = lambda n: pl.BlockSpec((tm, n), lambda i: (i, 0))
    return pl.pallas_call(
        _in_proj_kernel,
        grid=(T // tm,),
        in_specs=[row(D_MODEL), _full((1, D_MODEL)), _full(w_rwkv.shape), _full(w_lru.shape),
                  _full(w_mla.shape)],
        out_specs=[row(RWKV_IN), row(2 * LRU_W), row(MLA_PAD)],
        out_shape=[jax.ShapeDtypeStruct((T, RWKV_IN), F32),
                   jax.ShapeDtypeStruct((T, 2 * LRU_W), F32),
                   jax.ShapeDtypeStruct((T, MLA_PAD), F32)],
        compiler_params=_params("parallel"),
        name="in_proj",
    )(x2, g, w_rwkv, w_lru, w_mla)


def _rwkv_prep_kernel(p_ref, mu_ref, w0_ref, wup_ref, a0_ref, aup_ref, gup_ref, kk_ref, ka_ref,
                      rk_ref, bd_ref, r_o, lw_o, k_o, v_o, kk_o, kka_o, g_o, bonus_o, carry):
    @pl.when(pl.program_id(1) == 0)
    def _():
        carry[...] = jnp.zeros_like(carry)

    p = p_ref[0]
    ts = p.shape[0]
    rows = lax.broadcasted_iota(jnp.int32, p.shape, 0)
    prev = jnp.where(rows == 0, carry[SUBLANES - 1:SUBLANES, :], pltpu.roll(p, 1, 0))
    carry[...] = p[ts - SUBLANES:, :]
    pm = p + (prev - p) * mu_ref[...]
    o1, o2, o3 = RWKV_W, 2 * RWKV_W, 3 * RWKV_W
    r, k, v = pm[:, :o1], pm[:, o1:o2], pm[:, o2:o3]
    wa = pm[:, o3:o3 + W_LORA + A_LORA]
    gd = pm[:, o3 + W_LORA + A_LORA:]
    z = w0_ref[...] + _dot(jnp.tanh(wa), wup_ref[...])
    lw = -math.exp(-0.5) * _sigmoid(z)
    a = _sigmoid(a0_ref[...] + _dot(wa, aup_ref[...]))
    g = _dot(_sigmoid(gd), gup_ref[...])
    bd = bd_ref[...]
    kk = k * kk_ref[...]
    kk = kk / jnp.maximum(jnp.sqrt(_segsum(kk * kk, bd)), 1e-12)
    k2 = k * (1.0 + (a - 1.0) * ka_ref[...])
    bonus = _segsum(r * k2 * rk_ref[...], bd) * v
    r_o[0] = r
    lw_o[0] = lw
    k_o[0] = k2
    v_o[0] = v
    kk_o[0] = kk
    kka_o[0] = kk * a
    g_o[0] = g
    bonus_o[0] = bonus


def _rwkv_prep(p3, mu, w0, wup_pad, a0, aup_pad, gup, k_k, k_a, r_k, bd):
    B, S, _ = p3.shape
    ts = min(TOK_TILE, S)
    vec = _full((1, RWKV_W))
    out_spec = pl.BlockSpec((1, ts, RWKV_W), lambda b, s: (b, s, 0))
    out_shape = jax.ShapeDtypeStruct((B, S, RWKV_W), F32)
    return pl.pallas_call(
        _rwkv_prep_kernel,
        grid=(B, S // ts),
        in_specs=[pl.BlockSpec((1, ts, RWKV_IN), lambda b, s: (b, s, 0)), _full((1, RWKV_IN)),
                  vec, _full(wup_pad.shape), vec, _full(aup_pad.shape), _full(gup.shape),
                  vec, vec, vec, _full(bd.shape)],
        out_specs=[out_spec] * 8,
        out_shape=[out_shape] * 8,
        scratch_shapes=[pltpu.VMEM((SUBLANES, RWKV_IN), F32)],
        compiler_params=_params("parallel", "arbitrary"),
        name="rwkv_prep",
    )(p3, mu, w0, wup_pad, a0, aup_pad, gup, k_k, k_a, r_k, bd)


def _bmm(a, b):
    return jnp.einsum("hij,hjk->hik", a, b, precision=HIGHEST, preferred_element_type=F32)


def _unit_lower_inverse(L, ti, ii):
    C = L.shape[-1]
    eye = (ti == ii).astype(F32)
    blk = INV_BLOCK
    Ld = jnp.where(ti // blk == ii // blk, L, 0.0)
    T = eye + Ld
    Lp = Ld
    span = 2
    while span < blk:
        Lp = _bmm(Lp, Lp)
        T = T + _bmm(T, Lp)
        span *= 2
    while blk < C:
        off = jnp.where((ti // (2 * blk) == ii // (2 * blk)) & (ti // blk != ii // blk), L, 0.0)
        T = T + _bmm(_bmm(T, off), T)
        blk *= 2
    return T


def _rwkv_chunk_kernel(r_ref, lw_ref, k_ref, v_ref, kk_ref, kka_ref, g_ref, bonus_ref, lng_ref,
                       lnb_ref, ltri_ref, hmask_ref, bd_ref, bdmask_ref, y_ref, s_ref):
    @pl.when(pl.program_id(1) == 0)
    def _():
        s_ref[...] = jnp.zeros_like(s_ref)

    H, C = RWKV_HEADS, RWKV_CHUNK
    lw = lw_ref[0]
    v = v_ref[0]
    cum = jnp.dot(ltri_ref[...], lw, precision=HIGHEST, preferred_element_type=F32)
    ge = jnp.exp(cum)
    gi = jnp.exp(-cum)
    At = -kk_ref[0] * jnp.exp(cum - lw)
    Bt = kka_ref[0] * gi
    Kt = k_ref[0] * gi
    Rt = r_ref[0] * ge
    hmask = hmask_ref[...]

    def stack(x):
        return jnp.concatenate([x] * H, axis=0) * hmask

    def head_apply(m, x):
        pr = _dot(m, x) * hmask
        return jnp.sum(pr.reshape(H, C, RWKV_W), axis=0)

    lhs = jnp.concatenate([stack(At), stack(Rt)], axis=0)
    sb = _dot_nt(lhs, Bt)
    sk = _dot_nt(lhs, Kt)
    ti = lax.broadcasted_iota(jnp.int32, (H, C, C), 1)
    ii = lax.broadcasted_iota(jnp.int32, (H, C, C), 2)
    strict = ii < ti
    incl = ii <= ti
    n = H * C
    AB = jnp.where(strict, sb[:n].reshape(H, C, C), 0.0)
    AK = jnp.where(strict, sk[:n].reshape(H, C, C), 0.0)
    RB = jnp.where(incl, sb[n:].reshape(H, C, C), 0.0)
    RK = jnp.where(incl, sk[n:].reshape(H, C, C), 0.0)
    Tinv = _unit_lower_inverse(AB, ti, ii)

    S0 = s_ref[...]
    X = _dot_nt(jnp.concatenate([At, Rt], axis=0), S0)
    U = head_apply(Tinv.reshape(n, C), X[:C] + head_apply(AK.reshape(n, C), v))
    Y = X[C:] + head_apply(RB.reshape(n, C), U) + head_apply(RK.reshape(n, C), v)
    UV = jnp.concatenate([U, v], axis=0)
    BK = jnp.concatenate([Bt, Kt], axis=0)
    upd = lax.dot_general(UV.astype(BF16), BK.astype(BF16), (((0,), (0,)), ((), ())),
                          preferred_element_type=F32)
    s_ref[...] = (S0 + upd * bdmask_ref[...]) * ge[C - 1:C, :]

    bd = bd_ref[...]
    inv_n = 1.0 / RWKV_HEAD_DIM
    mean = _segsum(Y, bd) * inv_n
    yc = Y - mean
    var = _segsum(yc * yc, bd) * inv_n
    yn = yc * lax.rsqrt(var + RWKV_LN_EPS) * lng_ref[...] + lnb_ref[...]
    y_ref[0] = (yn + bonus_ref[0]) * g_ref[0]


def _rwkv_scan(r, lw, k, v, kk, kka, g, bonus, ln_g, ln_b, bd):
    B, S, W = r.shape
    C, H = RWKV_CHUNK, RWKV_HEADS
    tok = pl.BlockSpec((1, C, W), lambda b, c: (b, c, 0))
    vec = _full((1, W))
    ltri = (jnp.arange(C)[None, :] <= jnp.arange(C)[:, None]).astype(F32)
    head_of_lane = jnp.arange(W) // RWKV_HEAD_DIM
    hmask = (jnp.repeat(jnp.arange(H), C)[:, None] == head_of_lane[None, :]).astype(F32)
    bdmask = (head_of_lane[:, None] == head_of_lane[None, :]).astype(F32)
    return pl.pallas_call(
        _rwkv_chunk_kernel,
        grid=(B, S // C),
        in_specs=[tok] * 8 + [vec, vec, _full((C, C)), _full((H * C, W)), _full((W, W)),
                              _full((W, W))],
        out_specs=tok,
        out_shape=jax.ShapeDtypeStruct((B, S, W), F32),
        scratch_shapes=[pltpu.VMEM((W, W), F32)],
        compiler_params=_params("parallel", "arbitrary"),
        name="rwkv_chunk",
    )(r, lw, k, v, kk, kka, g, bonus, ln_g, ln_b, ltri, hmask, bd, bdmask)


def _shift_rows(x, d, fill, rows):
    return jnp.where(rows < d, fill, pltpu.roll(x, d, 0))


def _lru_kernel(p_ref, cw_ref, cb_ref, wa_ref, ba_ref, wx_ref, bx_ref, lam_ref, y_ref,
                xpad, hcarry):
    @pl.when(pl.program_id(1) == 0)
    def _():
        xpad[0:SUBLANES, :] = jnp.zeros((SUBLANES, LRU_W), F32)
        hcarry[...] = jnp.zeros_like(hcarry)

    p = p_ref[0]
    ts = p.shape[0]
    xb, gb = p[:, :LRU_W], p[:, LRU_W:]
    xpad[SUBLANES:, :] = xb
    cw = cw_ref[...]
    xc = cb_ref[...] + xb * cw[CONV_WIDTH - 1:CONV_WIDTH, :]
    for j in range(CONV_WIDTH - 1):
        lo = SUBLANES - (CONV_WIDTH - 1) + j
        xc = xc + xpad[lo:lo + ts, :] * cw[j:j + 1, :]
    xpad[0:SUBLANES, :] = xb[ts - SUBLANES:, :]
    rg = _sigmoid(_dot(xc, wa_ref[...]) + ba_ref[...])
    ig = _sigmoid(_dot(xc, wx_ref[...]) + bx_ref[...])
    lam = lam_ref[...]
    softplus_neg_lam = jnp.maximum(-lam, 0.0) + jnp.log(1.0 + jnp.exp(-jnp.abs(lam)))
    log_a = -LRU_C * rg * softplus_neg_lam
    a = jnp.exp(log_a)
    u = jnp.sqrt(1.0 - jnp.exp(2.0 * log_a)) * (ig * xc)
    rows = lax.broadcasted_iota(jnp.int32, a.shape, 0)
    d = 1
    while d < ts:
        u = u + a * _shift_rows(u, d, 0.0, rows)
        a = a * _shift_rows(a, d, 1.0, rows)
        d *= 2
    h = u + a * hcarry[SUBLANES - 1:SUBLANES, :]
    hcarry[...] = h[ts - SUBLANES:, :]
    gelu = 0.5 * gb * (1.0 + jnp.tanh(math.sqrt(2.0 / math.pi) * (gb + 0.044715 * gb * gb * gb)))
    y_ref[0] = h * gelu


def _lru(p3, conv_w, conv_b, wa_bd, ba, wx_bd, bx, lam):
    B, S, _ = p3.shape
    ts = min(TOK_TILE, S)
    vec = _full((1, LRU_W))
    return pl.pallas_call(
        _lru_kernel,
        grid=(B, S // ts),
        in_specs=[pl.BlockSpec((1, ts, 2 * LRU_W), lambda b, s: (b, s, 0)),
                  _full((CONV_WIDTH, LRU_W)), vec, _full((LRU_W, LRU_W)), vec,
                  _full((LRU_W, LRU_W)), vec, vec],
        out_specs=pl.BlockSpec((1, ts, LRU_W), lambda b, s: (b, s, 0)),
        out_shape=jax.ShapeDtypeStruct((B, S, LRU_W), F32),
        scratch_shapes=[pltpu.VMEM((ts + SUBLANES, LRU_W), F32), pltpu.VMEM((SUBLANES, LRU_W), F32)],
        compiler_params=_params("parallel", "arbitrary"),
        name="rglru",
    )(p3, conv_w, conv_b, wa_bd, ba, wx_bd, bx, lam)


def _mla_prep_kernel(p_ref, pos_ref, freq_ref, sign_ref, qn_ref, wuq_ref, kvn_ref, wuk_ref,
                     wuv_ref, qg_ref, kg_ref, q_o, k_o, v_o):
    p = p_ref[...]
    cq = p[:, :Q_RANK]
    ckv = p[:, Q_RANK:Q_RANK + KV_RANK]
    kr = p[:, Q_RANK + KV_RANK:]
    ang = pos_ref[...].astype(F32) * freq_ref[...]
    cosf = jnp.cos(ang)
    sinf = jnp.sin(ang) * sign_ref[...]
    lane = lax.broadcasted_iota(jnp.int32, ang.shape, 1)
    first_half = lane < MLA_NOPE + MLA_ROPE // 2

    def rope(x):
        half = MLA_ROPE // 2
        swapped = jnp.where(first_half, pltpu.roll(x, LANES - half, 1), pltpu.roll(x, half, 1))
        return x * cosf + swapped * sinf

    q = _dot(_rms(cq, qn_ref[...]), wuq_ref[...])
    kn = _dot(_rms(ckv, kvn_ref[...]), wuk_ref[...])
    v_o[...] = _dot(_rms(ckv, kvn_ref[...]), wuv_ref[...]).astype(BF16)
    scale = MLA_QK ** -0.5
    for h in range(MLA_HEADS):
        sl = slice(h * MLA_HEAD_PAD, (h + 1) * MLA_HEAD_PAD)
        qh = q[:, sl]
        qh = qh * lax.rsqrt(jnp.sum(qh * qh, axis=-1, keepdims=True) / MLA_QK + 1e-6) * qg_ref[...]
        q_o[:, sl] = (rope(qh) * scale).astype(BF16)
        kh = kn[:, sl] + kr
        kh = kh * lax.rsqrt(jnp.sum(kh * kh, axis=-1, keepdims=True) / MLA_QK + 1e-6) * kg_ref[...]
        k_o[:, sl] = rope(kh).astype(BF16)


def _mla_prep(p_mla, pos, freq, sign, q_norm, wuq, kv_norm, wuk, wuv, q_gain, k_gain):
    T = p_mla.shape[0]
    tm = min(TOK_TILE, T)
    W = MLA_HEADS * MLA_HEAD_PAD
    row = lambda n: pl.BlockSpec((tm, n), lambda i: (i, 0))
    return pl.pallas_call(
        _mla_prep_kernel,
        grid=(T // tm,),
        in_specs=[row(MLA_PAD), row(1), _full((1, LANES)), _full((1, LANES)), _full((1, Q_RANK)),
                  _full(wuq.shape), _full((1, KV_RANK)), _full(wuk.shape), _full(wuv.shape),
                  _full((1, LANES)), _full((1, LANES))],
        out_specs=[row(W), row(W), row(MLA_HEADS * MLA_V)],
        out_shape=[jax.ShapeDtypeStruct((T, W), BF16), jax.ShapeDtypeStruct((T, W), BF16),
                   jax.ShapeDtypeStruct((T, MLA_HEADS * MLA_V), BF16)],
        compiler_params=_params("parallel"),
        name="mla_prep",
    )(p_mla, pos, freq, sign, q_norm, wuq, kv_norm, wuk, wuv, q_gain, k_gain)


def _mla_attn_kernel(q_ref, k_ref, v_ref, o_ref):
    qi = pl.program_id(1)
    tq = q_ref.shape[1]
    rowi = lax.broadcasted_iota(jnp.int32, (tq, tq), 0)
    coli = lax.broadcasted_iota(jnp.int32, (tq, tq), 1)
    outs = []
    for h in range(MLA_HEADS):
        q = q_ref[0, :, h * MLA_HEAD_PAD:(h + 1) * MLA_HEAD_PAD]

        def scores(j):
            start = pl.multiple_of(j * tq, tq)
            kb = k_ref[0, pl.ds(start, tq), h * MLA_HEAD_PAD:(h + 1) * MLA_HEAD_PAD]
            vb = v_ref[0, pl.ds(start, tq), h * MLA_V:(h + 1) * MLA_V]
            return _dot_nt(q, kb), vb

        def update(carry, s, vb):
            m, l, acc = carry
            m2 = jnp.maximum(m, jnp.max(s, axis=-1, keepdims=True))
            pr = jnp.exp(s - m2)
            alpha = jnp.exp(m - m2)
            l2 = alpha * l + jnp.sum(pr, axis=-1, keepdims=True)
            acc2 = alpha * acc + _dot(pr, vb)
            return m2, l2, acc2

        def body(j, carry):
            s, vb = scores(j)
            return update(carry, s, vb)

        init = (jnp.full((tq, 1), -1e30, F32), jnp.zeros((tq, 1), F32), jnp.zeros((tq, MLA_V), F32))
        carry = lax.fori_loop(0, qi, body, init)
        s, vb = scores(qi)
        s = jnp.where(coli <= rowi, s, -1e30)
        m, l, acc = update(carry, s, vb)
        outs.append(acc / l)
    o_ref[0] = jnp.concatenate(outs, axis=-1)


def _mla_attn(q, k, v):
    B, S, W = q.shape
    tq = min(ATT_TILE, S)
    return pl.pallas_call(
        _mla_attn_kernel,
        grid=(B, S // tq),
        in_specs=[pl.BlockSpec((1, tq, W), lambda b, i: (b, i, 0)),
                  pl.BlockSpec((1, S, W), lambda b, i: (b, 0, 0)),
                  pl.BlockSpec((1, S, MLA_HEADS * MLA_V), lambda b, i: (b, 0, 0))],
        out_specs=pl.BlockSpec((1, tq, MLA_HEADS * MLA_V), lambda b, i: (b, i, 0)),
        out_shape=jax.ShapeDtypeStruct((B, S, MLA_HEADS * MLA_V), F32),
        compiler_params=_params("parallel", "arbitrary"),
        name="mla_attn",
    )(q, k, v)


def _merge_kernel(x_ref, ya_ref, yb_ref, yc_ref, g_ref, wg_ref, bg_ref, wb_ref, wo_ref, o_ref):
    x = x_ref[...]
    h = _rms(x, g_ref[...]).astype(BF16)
    merged = None
    for n, y_ref in enumerate((ya_ref, yb_ref, yc_ref)):
        sl = slice(n * D_MODEL, (n + 1) * D_MODEL)
        gate = _sigmoid(jnp.dot(h, wg_ref[:, sl], preferred_element_type=F32) + bg_ref[:, sl])
        term = gate * _dot(y_ref[...], wb_ref[n])
        merged = term if merged is None else merged + term
    o_ref[...] = x + _dot(merged, wo_ref[...])


def _merge(x2, ya, yb, yc, g, w_gate, b_gate, w_branch, w_out):
    T = x2.shape[0]
    tm = min(TOK_TILE, T)
    row = lambda n: pl.BlockSpec((tm, n), lambda i: (i, 0))
    return pl.pallas_call(
        _merge_kernel,
        grid=(T // tm,),
        in_specs=[row(D_MODEL), row(BRANCH_W), row(BRANCH_W), row(BRANCH_W), _full((1, D_MODEL)),
                  _full(w_gate.shape), _full(b_gate.shape), _full(w_branch.shape),
                  _full(w_out.shape)],
        out_specs=row(D_MODEL),
        out_shape=jax.ShapeDtypeStruct((T, D_MODEL), F32),
        compiler_params=_params("parallel"),
        name="merge",
    )(x2, ya, yb, yc, g, w_gate, b_gate, w_branch, w_out)


def _mem_kv_kernel(m_ref, g_ref, wk_ref, wv_ref, kg_ref, k_o, v_o):
    h = _rms(m_ref[...], g_ref[...]).astype(BF16)
    k = jnp.dot(h, wk_ref[...], preferred_element_type=F32)
    v_o[...] = jnp.dot(h, wv_ref[...], preferred_element_type=F32).astype(BF16)
    for hd in range(XA_HEADS):
        sl = slice(hd * XA_HEAD_DIM, (hd + 1) * XA_HEAD_DIM)
        k_o[:, sl] = _rms(k[:, sl], kg_ref[...]).astype(BF16)


def _mem_kv(mem2, g, wk, wv, k_gain):
    M = mem2.shape[0]
    tm = min(TOK_TILE, M)
    row = lambda n: pl.BlockSpec((tm, n), lambda i: (i, 0))
    return pl.pallas_call(
        _mem_kv_kernel,
        grid=(M // tm,),
        in_specs=[row(D_MODEL), _full((1, D_MODEL)), _full(wk.shape), _full(wv.shape),
                  _full((1, XA_HEAD_DIM))],
        out_specs=[row(XA_W), row(XA_W)],
        out_shape=[jax.ShapeDtypeStruct((M, XA_W), BF16)] * 2,
        compiler_params=_params("parallel"),
        name="mem_kv",
    )(mem2, g, wk, wv, k_gain)


def _xattn_kernel(x_ref, k_ref, v_ref, g_ref, wq_ref, qg_ref, wo_ref, o_ref):
    x = x_ref[0]
    h = _rms(x, g_ref[...]).astype(BF16)
    q = jnp.dot(h, wq_ref[...], preferred_element_type=F32)
    scale = XA_HEAD_DIM ** -0.5
    outs = []
    for hd in range(XA_HEADS):
        sl = slice(hd * XA_HEAD_DIM, (hd + 1) * XA_HEAD_DIM)
        qh = _rms(q[:, sl], qg_ref[...]) * scale
        s = _dot_nt(qh, k_ref[0, :, sl])
        pr = jnp.exp(s - jnp.max(s, axis=-1, keepdims=True))
        pr = pr / jnp.sum(pr, axis=-1, keepdims=True)
        outs.append(_dot(pr, v_ref[0, :, sl]))
    o = jnp.concatenate(outs, axis=-1)
    o_ref[0] = x + _dot(o, wo_ref[...])


def _xattn(x3, k3, v3, g, wq, q_gain, wo):
    B, S, _ = x3.shape
    ts = min(TOK_TILE, S)
    M = k3.shape[1]
    return pl.pallas_call(
        _xattn_kernel,
        grid=(B, S // ts),
        in_specs=[pl.BlockSpec((1, ts, D_MODEL), lambda b, s: (b, s, 0)),
                  pl.BlockSpec((1, M, XA_W), lambda b, s: (b, 0, 0)),
                  pl.BlockSpec((1, M, XA_W), lambda b, s: (b, 0, 0)),
                  _full((1, D_MODEL)), _full(wq.shape), _full((1, XA_HEAD_DIM)), _full(wo.shape)],
        out_specs=pl.BlockSpec((1, ts, D_MODEL), lambda b, s: (b, s, 0)),
        out_shape=jax.ShapeDtypeStruct((B, S, D_MODEL), F32),
        compiler_params=_params("parallel", "parallel"),
        name="xattn",
    )(x3, k3, v3, g, wq, q_gain, wo)


FF_SPLIT = 2


def _ffn_kernel(x_ref, g_ref, w1_ref, w3_ref, w2_ref, o_ref):
    x = x_ref[...]
    h = _rms(x, g_ref[...]).astype(BF16)
    step = D_FF // FF_SPLIT
    acc = x
    for c in range(FF_SPLIT):
        sl = slice(c * step, (c + 1) * step)
        a = jnp.dot(h, w1_ref[:, sl], preferred_element_type=F32)
        b = jnp.dot(h, w3_ref[:, sl], preferred_element_type=F32)
        z = a * _sigmoid(a) * b
        acc = acc + _dot(z, w2_ref[sl, :])
    o_ref[...] = acc


def _ffn(x2, g, w1, w3, w2):
    T = x2.shape[0]
    tm = min(TOK_TILE, T)
    row = lambda n: pl.BlockSpec((tm, n), lambda i: (i, 0))
    return pl.pallas_call(
        _ffn_kernel,
        grid=(T // tm,),
        in_specs=[row(D_MODEL), _full((1, D_MODEL)), _full(w1.shape), _full(w3.shape),
                  _full(w2.shape)],
        out_specs=row(D_MODEL),
        out_shape=jax.ShapeDtypeStruct((T, D_MODEL), F32),
        compiler_params=_params("parallel"),
        name="ffn",
    )(x2, g, w1, w3, w2)


def _block_diag(w):
    n, i, j = w.shape
    eye = jnp.eye(n, dtype=w.dtype)
    return jnp.einsum("nij,nm->nimj", w, eye).reshape(n * i, n * j)


def _pad_heads(w, heads, width, pad_to):
    lead = w.shape[:-1]
    w = w.reshape(lead + (heads, width))
    w = jnp.pad(w, [(0, 0)] * len(lead) + [(0, 0), (0, pad_to - width)])
    return w.reshape(lead + (heads * pad_to,))


def _row(v):
    return v.reshape(1, -1).astype(F32)


def kernel(x, mem, positions, norm_mix, norm_xattn, norm_mem, norm_ffn, w_in, b_gate, rwkv_mu, rwkv_w0, rwkv_w_up, rwkv_a0, rwkv_a_up, rwkv_g_up, rwkv_k_k, rwkv_k_a, rwkv_r_k, rwkv_ln_g, rwkv_ln_b, lru_conv_w, lru_conv_b, lru_wa, lru_ba, lru_wx, lru_bx, lru_lambda, mla_q_norm, mla_w_uq, mla_kv_norm, mla_w_ukv, mla_q_gain, mla_k_gain, w_branch, w_out, xa_w_q, xa_w_kv, xa_q_gain, xa_k_gain, xa_w_o, ffn_w1, ffn_w3, ffn_w2):
    B, S, D = x.shape
    T = B * S
    depth = w_in.shape[0]
    x2 = x.reshape(T, D)
    mem2 = mem.reshape(B * N_MEM, D)
    pos = positions.reshape(T, 1)

    inv_freq = ROPE_THETA ** (-jnp.arange(0, MLA_ROPE, 2, dtype=F32) / MLA_ROPE)
    half = MLA_ROPE // 2
    freq = jnp.zeros((LANES,), F32).at[MLA_NOPE:MLA_NOPE + half].set(inv_freq)
    freq = freq.at[MLA_NOPE + half:MLA_QK].set(inv_freq).reshape(1, LANES)
    sign = jnp.zeros((LANES,), F32).at[MLA_NOPE:MLA_NOPE + half].set(-1.0)
    sign = sign.at[MLA_NOPE + half:MLA_QK].set(1.0).reshape(1, LANES)
    head_of_lane = jnp.arange(RWKV_W) // RWKV_HEAD_DIM
    bd_ones = (head_of_lane[:, None] == head_of_lane[None, :]).astype(BF16)

    for l in range(depth):
        w = w_in[l]
        w_rwkv = w[:, :RWKV_IN].astype(BF16)
        w_lru = w[:, LRU_OFF:MLA_OFF].astype(BF16)
        w_mla = jnp.concatenate(
            [w[:, MLA_OFF:MLA_OFF + Q_RANK + KV_RANK], jnp.zeros((D, MLA_NOPE), F32),
             w[:, MLA_OFF + Q_RANK + KV_RANK:GATE_OFF],
             jnp.zeros((D, LANES - MLA_QK), F32)], axis=1).astype(BF16)
        w_gate = w[:, GATE_OFF:].astype(BF16)

        p_rwkv, p_lru, p_mla = _in_proj(x2, _row(norm_mix[l]), w_rwkv, w_lru, w_mla)

        zeros_lora = jnp.zeros((W_LORA, RWKV_W), F32)
        wup_pad = jnp.concatenate([rwkv_w_up[l], zeros_lora], axis=0).astype(BF16)
        aup_pad = jnp.concatenate([zeros_lora, rwkv_a_up[l]], axis=0).astype(BF16)
        prep = _rwkv_prep(p_rwkv.reshape(B, S, RWKV_IN), _row(rwkv_mu[l]), _row(rwkv_w0[l]), wup_pad,
                          _row(rwkv_a0[l]), aup_pad, rwkv_g_up[l].astype(BF16), _row(rwkv_k_k[l]),
                          _row(rwkv_k_a[l]), _row(rwkv_r_k[l]), bd_ones)
        y_a = _rwkv_scan(*prep, _row(rwkv_ln_g[l]), _row(rwkv_ln_b[l]), bd_ones).reshape(T, RWKV_W)

        y_b = _lru(p_lru.reshape(B, S, 2 * LRU_W), lru_conv_w[l], _row(lru_conv_b[l]),
                   _block_diag(lru_wa[l]).astype(BF16), _row(lru_ba[l]),
                   _block_diag(lru_wx[l]).astype(BF16), _row(lru_bx[l]),
                   _row(lru_lambda[l])).reshape(T, LRU_W)

        wuq = _pad_heads(mla_w_uq[l], MLA_HEADS, MLA_QK, MLA_HEAD_PAD).astype(BF16)
        wukv = mla_w_ukv[l].reshape(KV_RANK, MLA_HEADS, MLA_NOPE + MLA_V)
        wuk = _pad_heads(wukv[:, :, :MLA_NOPE].reshape(KV_RANK, -1), MLA_HEADS, MLA_NOPE,
                         MLA_HEAD_PAD).astype(BF16)
        wuv = wukv[:, :, MLA_NOPE:].reshape(KV_RANK, -1).astype(BF16)
        pad_gain = lambda gvec: jnp.pad(gvec, (0, LANES - MLA_QK)).reshape(1, LANES)
        q, k, v = _mla_prep(p_mla, pos, freq, sign, _row(mla_q_norm[l]), wuq, _row(mla_kv_norm[l]),
                            wuk, wuv, pad_gain(mla_q_gain[l]), pad_gain(mla_k_gain[l]))
        W = MLA_HEADS * MLA_HEAD_PAD
        y_c = _mla_attn(q.reshape(B, S, W), k.reshape(B, S, W),
                        v.reshape(B, S, MLA_HEADS * MLA_V)).reshape(T, MLA_HEADS * MLA_V)

        x2 = _merge(x2, y_a, y_b, y_c, _row(norm_mix[l]), w_gate, _row(b_gate[l]),
                    w_branch[l].astype(BF16), w_out[l].astype(BF16))

        wkv = xa_w_kv[l].reshape(D, XA_HEADS, 2, XA_HEAD_DIM)
        mk, mv = _mem_kv(mem2, _row(norm_mem[l]), wkv[:, :, 0].reshape(D, XA_W).astype(BF16),
                         wkv[:, :, 1].reshape(D, XA_W).astype(BF16), _row(xa_k_gain[l]))
        x2 = _xattn(x2.reshape(B, S, D), mk.reshape(B, N_MEM, XA_W), mv.reshape(B, N_MEM, XA_W),
                    _row(norm_xattn[l]), xa_w_q[l].astype(BF16), _row(xa_q_gain[l]),
                    xa_w_o[l].astype(BF16)).reshape(T, D)

        x2 = _ffn(x2, _row(norm_ffn[l]), ffn_w1[l].astype(BF16), ffn_w3[l].astype(BF16),
                  ffn_w2[l].astype(BF16))
    return x2.reshape(B, S, D)
```

```python
import functools
import math

import jax
import jax.numpy as jnp
from jax import lax
from jax.experimental import pallas as pl
from jax.experimental.pallas import tpu as pltpu

F32 = jnp.float32
BF16 = jnp.bfloat16
HIGHEST = lax.Precision.HIGHEST

D_MODEL = 1024
N_MEM = 256
RWKV_HEADS = 8
RWKV_HEAD_DIM = 64
RWKV_W = RWKV_HEADS * RWKV_HEAD_DIM
W_LORA = 64
A_LORA = 64
G_LORA = 128
RWKV_IN = 3 * RWKV_W + W_LORA + A_LORA + G_LORA
RWKV_LN_EPS = RWKV_HEAD_DIM * 1e-5
LRU_BLOCKS = 8
LRU_W = 512
CONV_WIDTH = 4
LRU_C = 8.0
MLA_HEADS = 8
MLA_NOPE = 64
MLA_ROPE = 32
MLA_QK = MLA_NOPE + MLA_ROPE
MLA_V = 64
Q_RANK = 256
KV_RANK = 128
ROPE_THETA = 10000.0
N_BRANCH = 3
BRANCH_W = 512
XA_HEADS = 4
XA_HEAD_DIM = 128
XA_W = XA_HEADS * XA_HEAD_DIM
D_FF = -(-8 * D_MODEL // (3 * 256)) * 256
LRU_OFF = RWKV_IN
MLA_OFF = LRU_OFF + 2 * LRU_W
GATE_OFF = MLA_OFF + Q_RANK + KV_RANK + MLA_ROPE

LANES = 128
SUBLANES = 8
VMEM_LIMIT = 56 * 1024 * 1024
MLA_PAD = 4 * LANES
MLA_HEAD_PAD = LANES
RWKV_CHUNK = 64
RWKV_GROUP = 4
TOK_TILE = 512
ATT_TILE = 256


def _params(*sem):
    return pltpu.CompilerParams(dimension_semantics=sem, vmem_limit_bytes=VMEM_LIMIT)


def _dot(a, b):
    return jnp.dot(a.astype(BF16), b.astype(BF16), preferred_element_type=F32)


def _dot_nt(a, b):
    return lax.dot_general(a.astype(BF16), b.astype(BF16), (((1,), (1,)), ((), ())),
                           preferred_element_type=F32)


def _sigmoid(x):
    return 1.0 / (1.0 + jnp.exp(-x))


def _rms(x, g, eps=1e-6):
    return x * lax.rsqrt(jnp.mean(x * x, axis=-1, keepdims=True) + eps) * g


def _segsum(x, bd):
    hi = x.astype(BF16)
    lo = (x - hi.astype(F32)).astype(BF16)
    return (jnp.dot(hi, bd, preferred_element_type=F32)
            + jnp.dot(lo, bd, preferred_element_type=F32))


def _full(shape):
    n = len(shape)
    return pl.BlockSpec(shape, lambda *_: (0,) * n)


def _in_proj_kernel(x_ref, g_ref, wr_ref, wl_ref, wm_ref, or_ref, ol_ref, om_ref):
    h = _rms(x_ref[...], g_ref[...]).astype(BF16)
    or_ref[...] = jnp.dot(h, wr_ref[...], preferred_element_type=F32)
    ol_ref[...] = jnp.dot(h, wl_ref[...], preferred_element_type=F32)
    om_ref[...] = jnp.dot(h, wm_ref[...], preferred_element_type=F32)


def _in_proj(x2, g, w_rwkv, w_lru, w_mla):
    T = x2.shape[0]
    tm = min(TOK_TILE, T)
    row = lambda n: pl.BlockSpec((tm, n), lambda i: (i, 0))
    return pl.pallas_call(
        _in_proj_kernel,
        grid=(T // tm,),
        in_specs=[row(D_MODEL), _full((1, D_MODEL)), _full(w_rwkv.shape), _full(w_lru.shape),
                  _full(w_mla.shape)],
        out_specs=[row(RWKV_IN), row(2 * LRU_W), row(MLA_PAD)],
        out_shape=[jax.ShapeDtypeStruct((T, RWKV_IN), F32),
                   jax.ShapeDtypeStruct((T, 2 * LRU_W), F32),
                   jax.ShapeDtypeStruct((T, MLA_PAD), F32)],
        compiler_params=_params("parallel"),
        name="in_proj",
    )(x2, g, w_rwkv, w_lru, w_mla)


def _rwkv_prep_kernel(p_ref, mu_ref, w0_ref, wup_ref, a0_ref, aup_ref, gup_ref, kk_ref, ka_ref,
                      rk_ref, bd_ref, r_o, lw_o, k_o, v_o, kk_o, kka_o, g_o, bonus_o, carry):
    @pl.when(pl.program_id(1) == 0)
    def _():
        carry[...] = jnp.zeros_like(carry)

    p = p_ref[0]
    ts = p.shape[0]
    rows = lax.broadcasted_iota(jnp.int32, p.shape, 0)
    prev = jnp.where(rows == 0, carry[SUBLANES - 1:SUBLANES, :], pltpu.roll(p, 1, 0))
    carry[...] = p[ts - SUBLANES:, :]
    pm = p + (prev - p) * mu_ref[...]
    o1, o2, o3 = RWKV_W, 2 * RWKV_W, 3 * RWKV_W
    r, k, v = pm[:, :o1], pm[:, o1:o2], pm[:, o2:o3]
    wa = pm[:, o3:o3 + W_LORA + A_LORA]
    gd = pm[:, o3 + W_LORA + A_LORA:]
    z = w0_ref[...] + _dot(jnp.tanh(wa), wup_ref[...])
    lw = -math.exp(-0.5) * _sigmoid(z)
    a = _sigmoid(a0_ref[...] + _dot(wa, aup_ref[...]))
    g = _dot(_sigmoid(gd), gup_ref[...])
    bd = bd_ref[...]
    kk = k * kk_ref[...]
    kk = kk / jnp.maximum(jnp.sqrt(_segsum(kk * kk, bd)), 1e-12)
    k2 = k * (1.0 + (a - 1.0) * ka_ref[...])
    bonus = _segsum(r * k2 * rk_ref[...], bd) * v
    r_o[0] = r
    lw_o[0] = lw
    k_o[0] = k2
    v_o[0] = v
    kk_o[0] = kk
    kka_o[0] = kk * a
    g_o[0] = g
    bonus_o[0] = bonus


def _rwkv_prep(p3, mu, w0, wup_pad, a0, aup_pad, gup, k_k, k_a, r_k, bd):
    B, S, _ = p3.shape
    ts = min(TOK_TILE, S)
    vec = _full((1, RWKV_W))
    out_spec = pl.BlockSpec((1, ts, RWKV_W), lambda b, s: (b, s, 0))
    out_shape = jax.ShapeDtypeStruct((B, S, RWKV_W), F32)
    return pl.pallas_call(
        _rwkv_prep_kernel,
        grid=(B, S // ts),
        in_specs=[pl.BlockSpec((1, ts, RWKV_IN), lambda b, s: (b, s, 0)), _full((1, RWKV_IN)),
                  vec, _full(wup_pad.shape), vec, _full(aup_pad.shape), _full(gup.shape),
                  vec, vec, vec, _full(bd.shape)],
        out_specs=[out_spec] * 8,
        out_shape=[out_shape] * 8,
        scratch_shapes=[pltpu.VMEM((SUBLANES, RWKV_IN), F32)],
        compiler_params=_params("parallel", "arbitrary"),
        name="rwkv_prep",
    )(p3, mu, w0, wup_pad, a0, aup_pad, gup, k_k, k_a, r_k, bd)


def _bmm(a, b):
    return jnp.einsum("hij,hjk->hik", a.astype(BF16), b.astype(BF16), preferred_element_type=F32)


def _unit_lower_inverse(L, ti, ii):
    C = L.shape[-1]
    eye = (ti == ii).astype(F32)
    blk = 2
    T = eye + jnp.where(ti // blk == ii // blk, L, 0.0)
    while blk < C:
        off = jnp.where((ti // (2 * blk) == ii // (2 * blk)) & (ti // blk != ii // blk), L, 0.0)
        T = T + _bmm(_bmm(T, off), T)
        blk *= 2
    return T


def _cumsum_rows(ltri, x):
    hi = x.astype(BF16)
    r1 = x - hi.astype(F32)
    mid = r1.astype(BF16)
    lo = (r1 - mid.astype(F32)).astype(BF16)
    dot = lambda t: jnp.dot(ltri, t, preferred_element_type=F32)
    return dot(hi) + dot(mid) + dot(lo)


def _rwkv_chunk_kernel(r_ref, lw_ref, k_ref, v_ref, kk_ref, kka_ref, g_ref, bonus_ref, lng_ref,
                       lnb_ref, ltri_ref, bd_ref, y_ref, s_ref):
    @pl.when(pl.program_id(1) == 0)
    def _():
        s_ref[...] = jnp.zeros_like(s_ref)

    H, C, N = RWKV_HEADS, RWKV_CHUNK, RWKV_HEAD_DIM
    PW = 2 * N
    G = r_ref.shape[1] // C

    lo_lane = lax.broadcasted_iota(jnp.int32, (C, PW), 1) < N

    def bd2(x):
        zero = jnp.zeros_like(x)
        return jnp.concatenate([jnp.where(lo_lane, x, zero), jnp.where(lo_lane, zero, x)], axis=0)

    trow = lax.broadcasted_iota(jnp.int32, (C, PW), 0)
    tcol = lax.broadcasted_iota(jnp.int32, (C, PW), 1) % C
    strict2 = tcol < trow
    incl2 = tcol <= trow
    vrow = lax.broadcasted_iota(jnp.int32, (PW, PW), 0) // N
    vcol = lax.broadcasted_iota(jnp.int32, (PW, PW), 1) // N
    same_head = vrow == vcol

    contract0 = (((0,), (0,)), ((), ()))
    contract1 = (((1,), (1,)), ((), ()))

    chunks = []
    Ls = []
    for c in range(G):
        rows = slice(c * C, (c + 1) * C)
        lw = lw_ref[0, rows, :]
        cum = _cumsum_rows(ltri_ref[...], lw)
        ge = jnp.exp(cum)
        gi = jnp.exp(-cum)
        At = (-kk_ref[0, rows, :] * jnp.exp(cum - lw)).astype(BF16)
        Bt = (kka_ref[0, rows, :] * gi).astype(BF16)
        Kt = (k_ref[0, rows, :] * gi).astype(BF16)
        Rt = (r_ref[0, rows, :] * ge).astype(BF16)
        V = v_ref[0, rows, :].astype(BF16)
        pairs = []
        for j in range(H // 2):
            sl = slice(j * PW, (j + 1) * PW)
            AR = jnp.concatenate([At[:, sl], Rt[:, sl]], axis=0)
            BK = jnp.concatenate([bd2(Bt[:, sl]), bd2(Kt[:, sl])], axis=0)
            sc = lax.dot_general(AR, BK, contract1, preferred_element_type=F32)
            ab = sc[:C, :PW]
            Ls += [ab[:, :C], ab[:, C:]]
            pairs.append((sl, At[:, sl], Bt[:, sl], Kt[:, sl], Rt[:, sl], V[:, sl],
                          sc[:C, PW:], sc[C:, :PW], sc[C:, PW:]))
        chunks.append((pairs, ge[C - 1:C, :]))
    ti = lax.broadcasted_iota(jnp.int32, (G * H, C, C), 1)
    ii = lax.broadcasted_iota(jnp.int32, (G * H, C, C), 2)
    Tinv = _unit_lower_inverse(jnp.where(ii < ti, jnp.stack(Ls), 0.0), ti, ii)

    local = []
    for c, (pairs, g_last) in enumerate(chunks):
        per_pair = []
        for j, (sl, At, Bt, Kt, Rt, V, ak, rb, rk) in enumerate(pairs):
            hd = (c * H // 2 + j) * 2
            Tl = jnp.concatenate([Tinv[hd], Tinv[hd + 1]], axis=1)
            akv = _dot(jnp.where(strict2, ak, 0.0), bd2(V))
            au = _dot(Tl, jnp.concatenate([bd2(At), bd2(akv.astype(BF16))], axis=1))
            A2 = au[:, :PW].astype(BF16)
            U0 = au[:, PW:].astype(BF16)
            rbm = jnp.where(incl2, rb, 0.0).astype(BF16)
            rkm = jnp.where(incl2, rk, 0.0).astype(BF16)
            R2 = Rt.astype(F32) + jnp.dot(rbm, bd2(A2), preferred_element_type=F32)
            Y0 = (jnp.dot(rbm, bd2(U0), preferred_element_type=F32)
                  + jnp.dot(rkm, bd2(V), preferred_element_type=F32))
            g_row = g_last[:, sl]
            Mq = jnp.where(same_head, lax.dot_general(A2, Bt, contract0,
                                                      preferred_element_type=F32), 0.0) * g_row
            Nq = jnp.where(same_head,
                           lax.dot_general(U0, Bt, contract0, preferred_element_type=F32)
                           + lax.dot_general(V, Kt, contract0, preferred_element_type=F32),
                           0.0) * g_row
            per_pair.append((sl, R2.astype(BF16), Y0, Mq.astype(BF16), Nq, g_row))
        local.append(per_pair)

    y_chunks = []
    for per_pair in local:
        ys = []
        for j, (sl, R2, Y0, Mq, Nq, g_row) in enumerate(per_pair):
            S0 = s_ref[j]
            Sb = S0.astype(BF16)
            ys.append(Y0 + lax.dot_general(R2, Sb, contract1, preferred_element_type=F32))
            s_ref[j] = S0 * g_row + jnp.dot(Sb, Mq, preferred_element_type=F32) + Nq
        y_chunks.append(jnp.concatenate(ys, axis=1))
    Y = jnp.concatenate(y_chunks, axis=0)

    bd = bd_ref[...]
    inv_n = 1.0 / N
    mean = _segsum(Y, bd) * inv_n
    yc = Y - mean
    var = _segsum(yc * yc, bd) * inv_n
    yn = yc * lax.rsqrt(var + RWKV_LN_EPS) * lng_ref[...] + lnb_ref[...]
    y_ref[0] = (yn + bonus_ref[0]) * g_ref[0]


def _rwkv_scan(r, lw, k, v, kk, kka, g, bonus, ln_g, ln_b, bd):
    B, S, W = r.shape
    C, H = RWKV_CHUNK, RWKV_HEADS
    ts = min(RWKV_GROUP * C, S)
    tok = pl.BlockSpec((1, ts, W), lambda b, c: (b, c, 0))
    vec = _full((1, W))
    ltri = (jnp.arange(C)[None, :] <= jnp.arange(C)[:, None]).astype(BF16)
    pair_w = 2 * RWKV_HEAD_DIM
    return pl.pallas_call(
        _rwkv_chunk_kernel,
        grid=(B, S // ts),
        in_specs=[tok] * 8 + [vec, vec, _full((C, C)), _full((W, W))],
        out_specs=tok,
        out_shape=jax.ShapeDtypeStruct((B, S, W), F32),
        scratch_shapes=[pltpu.VMEM((H // 2, pair_w, pair_w), F32)],
        compiler_params=_params("parallel", "arbitrary"),
        name="rwkv_chunk",
    )(r, lw, k, v, kk, kka, g, bonus, ln_g, ln_b, ltri, bd)


def _shift_rows(x, d, fill, rows):
    return jnp.where(rows < d, fill, pltpu.roll(x, d, 0))


def _lru_kernel(p_ref, cw_ref, cb_ref, wa_ref, ba_ref, wx_ref, bx_ref, lam_ref, y_ref,
                xpad, hcarry):
    @pl.when(pl.program_id(1) == 0)
    def _():
        xpad[0:SUBLANES, :] = jnp.zeros((SUBLANES, LRU_W), F32)
        hcarry[...] = jnp.zeros_like(hcarry)

    p = p_ref[0]
    ts = p.shape[0]
    xb, gb = p[:, :LRU_W], p[:, LRU_W:]
    xpad[SUBLANES:, :] = xb
    cw = cw_ref[...]
    xc = cb_ref[...] + xb * cw[CONV_WIDTH - 1:CONV_WIDTH, :]
    for j in range(CONV_WIDTH - 1):
        lo = SUBLANES - (CONV_WIDTH - 1) + j
        xc = xc + xpad[lo:lo + ts, :] * cw[j:j + 1, :]
    xpad[0:SUBLANES, :] = xb[ts - SUBLANES:, :]
    rg = _sigmoid(_dot(xc, wa_ref[...]) + ba_ref[...])
    ig = _sigmoid(_dot(xc, wx_ref[...]) + bx_ref[...])
    lam = lam_ref[...]
    softplus_neg_lam = jnp.maximum(-lam, 0.0) + jnp.log(1.0 + jnp.exp(-jnp.abs(lam)))
    log_a = -LRU_C * rg * softplus_neg_lam
    a = jnp.exp(log_a)
    u = jnp.sqrt(1.0 - jnp.exp(2.0 * log_a)) * (ig * xc)
    rows = lax.broadcasted_iota(jnp.int32, a.shape, 0)
    d = 1
    while d < ts:
        u = u + a * _shift_rows(u, d, 0.0, rows)
        a = a * _shift_rows(a, d, 1.0, rows)
        d *= 2
    h = u + a * hcarry[SUBLANES - 1:SUBLANES, :]
    hcarry[...] = h[ts - SUBLANES:, :]
    gelu = 0.5 * gb * (1.0 + jnp.tanh(math.sqrt(2.0 / math.pi) * (gb + 0.044715 * gb * gb * gb)))
    y_ref[0] = h * gelu


def _lru(p3, conv_w, conv_b, wa_bd, ba, wx_bd, bx, lam):
    B, S, _ = p3.shape
    ts = min(TOK_TILE, S)
    vec = _full((1, LRU_W))
    return pl.pallas_call(
        _lru_kernel,
        grid=(B, S // ts),
        in_specs=[pl.BlockSpec((1, ts, 2 * LRU_W), lambda b, s: (b, s, 0)),
                  _full((CONV_WIDTH, LRU_W)), vec, _full((LRU_W, LRU_W)), vec,
                  _full((LRU_W, LRU_W)), vec, vec],
        out_specs=pl.BlockSpec((1, ts, LRU_W), lambda b, s: (b, s, 0)),
        out_shape=jax.ShapeDtypeStruct((B, S, LRU_W), F32),
        scratch_shapes=[pltpu.VMEM((ts + SUBLANES, LRU_W), F32), pltpu.VMEM((SUBLANES, LRU_W), F32)],
        compiler_params=_params("parallel", "arbitrary"),
        name="rglru",
    )(p3, conv_w, conv_b, wa_bd, ba, wx_bd, bx, lam)


def _mla_prep_kernel(p_ref, pos_ref, freq_ref, sign_ref, qn_ref, wuq_ref, kvn_ref, wuk_ref,
                     wuv_ref, qg_ref, kg_ref, q_o, k_o, vt_o):
    p = p_ref[...]
    cq = p[:, :Q_RANK]
    ckv = p[:, Q_RANK:Q_RANK + KV_RANK]
    kr = p[:, Q_RANK + KV_RANK:]
    ang = pos_ref[...].astype(F32) * freq_ref[...]
    cosf = jnp.cos(ang)
    sinf = jnp.sin(ang) * sign_ref[...]
    lane = lax.broadcasted_iota(jnp.int32, ang.shape, 1)
    first_half = lane < MLA_NOPE + MLA_ROPE // 2

    def rope(x):
        half = MLA_ROPE // 2
        swapped = jnp.where(first_half, pltpu.roll(x, LANES - half, 1), pltpu.roll(x, half, 1))
        return x * cosf + swapped * sinf

    q = _dot(_rms(cq, qn_ref[...]), wuq_ref[...])
    ckv_n = _rms(ckv, kvn_ref[...]).astype(BF16)
    kn = jnp.dot(ckv_n, wuk_ref[...], preferred_element_type=F32)
    vt_o[0] = _dot_nt(wuv_ref[...], ckv_n).astype(BF16)
    scale = MLA_QK ** -0.5 * math.log2(math.e)
    for h in range(MLA_HEADS):
        sl = slice(h * MLA_HEAD_PAD, (h + 1) * MLA_HEAD_PAD)
        qh = q[:, sl]
        qh = qh * lax.rsqrt(jnp.sum(qh * qh, axis=-1, keepdims=True) / MLA_QK + 1e-6) * qg_ref[...]
        q_o[:, sl] = (rope(qh) * scale).astype(BF16)
        kh = kn[:, sl] + kr
        kh = kh * lax.rsqrt(jnp.sum(kh * kh, axis=-1, keepdims=True) / MLA_QK + 1e-6) * kg_ref[...]
        k_o[:, sl] = rope(kh).astype(BF16)


def _mla_prep(p_mla, pos, freq, sign, q_norm, wuq, kv_norm, wuk, wuv, q_gain, k_gain, B, S):
    T = p_mla.shape[0]
    tm = min(TOK_TILE, S)
    per_row = S // tm
    W = MLA_HEADS * MLA_HEAD_PAD
    row = lambda n: pl.BlockSpec((tm, n), lambda i: (i, 0))
    return pl.pallas_call(
        _mla_prep_kernel,
        grid=(T // tm,),
        in_specs=[row(MLA_PAD), row(1), _full((1, LANES)), _full((1, LANES)), _full((1, Q_RANK)),
                  _full(wuq.shape), _full((1, KV_RANK)), _full(wuk.shape), _full(wuv.shape),
                  _full((1, LANES)), _full((1, LANES))],
        out_specs=[row(W), row(W),
                   pl.BlockSpec((1, MLA_HEADS * MLA_V, tm), lambda i: (i // per_row, 0, i % per_row))],
        out_shape=[jax.ShapeDtypeStruct((T, W), BF16), jax.ShapeDtypeStruct((T, W), BF16),
                   jax.ShapeDtypeStruct((B, MLA_HEADS * MLA_V, S), BF16)],
        compiler_params=_params("parallel"),
        name="mla_prep",
    )(p_mla, pos, freq, sign, q_norm, wuq, kv_norm, wuk, wuv, q_gain, k_gain)


def _mla_attn_kernel(q_ref, k_ref, vt_ref, o_ref, acc_ref, m_ref, l_ref, s_ref, p_ref, a_ref):
    qi = pl.program_id(1)
    tq = q_ref.shape[1]
    m_ref[...] = jnp.full(m_ref.shape, -1e30, F32)
    l_ref[...] = jnp.zeros(l_ref.shape, F32)
    acc_ref[...] = jnp.zeros(acc_ref.shape, F32)

    def tile(j, masked):
        start = pl.multiple_of(j * tq, tq)
        for h in range(MLA_HEADS):
            hs = slice(h * MLA_HEAD_PAD, (h + 1) * MLA_HEAD_PAD)
            kb = k_ref[0, pl.ds(start, tq), hs]
            s_ref[h] = lax.dot_general(kb, q_ref[0, :, hs], (((1,), (1,)), ((), ())),
                                       preferred_element_type=F32)
        for h in range(MLA_HEADS):
            st = s_ref[h]
            if masked:
                kpos = lax.broadcasted_iota(jnp.int32, st.shape, 0)
                qpos = lax.broadcasted_iota(jnp.int32, st.shape, 1)
                st = jnp.where(kpos <= qpos, st, -1e30)
            m_old = m_ref[h:h + 1, :]
            m_new = jnp.maximum(m_old, jnp.max(st, axis=0, keepdims=True))
            pr = jnp.exp2(st - m_new)
            alpha = jnp.exp2(m_old - m_new)
            l_ref[h:h + 1, :] = alpha * l_ref[h:h + 1, :] + jnp.sum(pr, axis=0, keepdims=True)
            m_ref[h:h + 1, :] = m_new
            a_ref[h:h + 1, :] = alpha
            p_ref[h] = pr.astype(BF16)
        for h in range(MLA_HEADS):
            vs = slice(h * MLA_V, (h + 1) * MLA_V)
            pv = jnp.dot(vt_ref[0, vs, pl.ds(start, tq)], p_ref[h],
                         preferred_element_type=F32)
            acc_ref[vs, :] = a_ref[h:h + 1, :] * acc_ref[vs, :] + pv

    def body(j, carry):
        tile(j, False)
        return carry

    lax.fori_loop(0, qi, body, 0)
    tile(qi, True)
    for h in range(MLA_HEADS):
        vs = slice(h * MLA_V, (h + 1) * MLA_V)
        acc_ref[vs, :] = acc_ref[vs, :] / l_ref[h:h + 1, :]
    o_ref[0] = acc_ref[...].T


def _mla_attn(q, k, vt):
    B, S, W = q.shape
    tq = min(ATT_TILE, S)
    WV = MLA_HEADS * MLA_V
    return pl.pallas_call(
        _mla_attn_kernel,
        grid=(B, S // tq),
        in_specs=[pl.BlockSpec((1, tq, W), lambda b, i: (b, i, 0)),
                  pl.BlockSpec((1, S, W), lambda b, i: (b, 0, 0)),
                  pl.BlockSpec((1, WV, S), lambda b, i: (b, 0, 0))],
        out_specs=pl.BlockSpec((1, tq, WV), lambda b, i: (b, i, 0)),
        out_shape=jax.ShapeDtypeStruct((B, S, WV), F32),
        scratch_shapes=[pltpu.VMEM((WV, tq), F32), pltpu.VMEM((MLA_HEADS, tq), F32),
                        pltpu.VMEM((MLA_HEADS, tq), F32), pltpu.VMEM((MLA_HEADS, tq, tq), F32),
                        pltpu.VMEM((MLA_HEADS, tq, tq), BF16), pltpu.VMEM((MLA_HEADS, tq), F32)],
        compiler_params=_params("parallel", "arbitrary"),
        name="mla_attn",
    )(q, k, vt)


def _merge_kernel(x_ref, ya_ref, yb_ref, yc_ref, g_ref, wg_ref, bg_ref, wb_ref, wo_ref, o_ref):
    x = x_ref[...]
    h = _rms(x, g_ref[...]).astype(BF16)
    merged = None
    for n, y_ref in enumerate((ya_ref, yb_ref, yc_ref)):
        sl = slice(n * D_MODEL, (n + 1) * D_MODEL)
        gate = _sigmoid(jnp.dot(h, wg_ref[:, sl], preferred_element_type=F32) + bg_ref[:, sl])
        term = gate * _dot(y_ref[...], wb_ref[n])
        merged = term if merged is None else merged + term
    o_ref[...] = x + _dot(merged, wo_ref[...])


def _merge(x2, ya, yb, yc, g, w_gate, b_gate, w_branch, w_out):
    T = x2.shape[0]
    tm = min(TOK_TILE, T)
    row = lambda n: pl.BlockSpec((tm, n), lambda i: (i, 0))
    return pl.pallas_call(
        _merge_kernel,
        grid=(T // tm,),
        in_specs=[row(D_MODEL), row(BRANCH_W), row(BRANCH_W), row(BRANCH_W), _full((1, D_MODEL)),
                  _full(w_gate.shape), _full(b_gate.shape), _full(w_branch.shape),
                  _full(w_out.shape)],
        out_specs=row(D_MODEL),
        out_shape=jax.ShapeDtypeStruct((T, D_MODEL), F32),
        compiler_params=_params("parallel"),
        name="merge",
    )(x2, ya, yb, yc, g, w_gate, b_gate, w_branch, w_out)


def _mem_kv_kernel(m_ref, g_ref, wk_ref, wv_ref, kg_ref, k_o, v_o):
    h = _rms(m_ref[...], g_ref[...]).astype(BF16)
    k = jnp.dot(h, wk_ref[...], preferred_element_type=F32)
    v_o[...] = jnp.dot(h, wv_ref[...], preferred_element_type=F32).astype(BF16)
    for hd in range(XA_HEADS):
        sl = slice(hd * XA_HEAD_DIM, (hd + 1) * XA_HEAD_DIM)
        k_o[:, sl] = _rms(k[:, sl], kg_ref[...]).astype(BF16)


def _mem_kv(mem2, g, wk, wv, k_gain):
    M = mem2.shape[0]
    tm = min(TOK_TILE, M)
    row = lambda n: pl.BlockSpec((tm, n), lambda i: (i, 0))
    return pl.pallas_call(
        _mem_kv_kernel,
        grid=(M // tm,),
        in_specs=[row(D_MODEL), _full((1, D_MODEL)), _full(wk.shape), _full(wv.shape),
                  _full((1, XA_HEAD_DIM))],
        out_specs=[row(XA_W), row(XA_W)],
        out_shape=[jax.ShapeDtypeStruct((M, XA_W), BF16)] * 2,
        compiler_params=_params("parallel"),
        name="mem_kv",
    )(mem2, g, wk, wv, k_gain)


def _xattn_kernel(x_ref, k_ref, v_ref, g_ref, wq_ref, qg_ref, wo_ref, o_ref):
    x = x_ref[0]
    h = _rms(x, g_ref[...]).astype(BF16)
    q = jnp.dot(h, wq_ref[...], preferred_element_type=F32)
    scale = XA_HEAD_DIM ** -0.5
    outs = []
    for hd in range(XA_HEADS):
        sl = slice(hd * XA_HEAD_DIM, (hd + 1) * XA_HEAD_DIM)
        qh = _rms(q[:, sl], qg_ref[...]) * scale
        s = _dot_nt(qh, k_ref[0, :, sl])
        pr = jnp.exp(s - jnp.max(s, axis=-1, keepdims=True))
        pr = pr / jnp.sum(pr, axis=-1, keepdims=True)
        outs.append(_dot(pr, v_ref[0, :, sl]))
    o = jnp.concatenate(outs, axis=-1)
    o_ref[0] = x + _dot(o, wo_ref[...])


def _xattn(x3, k3, v3, g, wq, q_gain, wo):
    B, S, _ = x3.shape
    ts = min(TOK_TILE, S)
    M = k3.shape[1]
    return pl.pallas_call(
        _xattn_kernel,
        grid=(B, S // ts),
        in_specs=[pl.BlockSpec((1, ts, D_MODEL), lambda b, s: (b, s, 0)),
                  pl.BlockSpec((1, M, XA_W), lambda b, s: (b, 0, 0)),
                  pl.BlockSpec((1, M, XA_W), lambda b, s: (b, 0, 0)),
                  _full((1, D_MODEL)), _full(wq.shape), _full((1, XA_HEAD_DIM)), _full(wo.shape)],
        out_specs=pl.BlockSpec((1, ts, D_MODEL), lambda b, s: (b, s, 0)),
        out_shape=jax.ShapeDtypeStruct((B, S, D_MODEL), F32),
        compiler_params=_params("parallel", "parallel"),
        name="xattn",
    )(x3, k3, v3, g, wq, q_gain, wo)


FF_SPLIT = 2


def _ffn_kernel(x_ref, g_ref, w1_ref, w3_ref, w2_ref, o_ref):
    x = x_ref[...]
    h = _rms(x, g_ref[...]).astype(BF16)
    step = D_FF // FF_SPLIT
    acc = x
    for c in range(FF_SPLIT):
        sl = slice(c * step, (c + 1) * step)
        a = jnp.dot(h, w1_ref[:, sl], preferred_element_type=F32)
        b = jnp.dot(h, w3_ref[:, sl], preferred_element_type=F32)
        z = a * _sigmoid(a) * b
        acc = acc + _dot(z, w2_ref[sl, :])
    o_ref[...] = acc


def _ffn(x2, g, w1, w3, w2):
    T = x2.shape[0]
    tm = min(TOK_TILE, T)
    row = lambda n: pl.BlockSpec((tm, n), lambda i: (i, 0))
    return pl.pallas_call(
        _ffn_kernel,
        grid=(T // tm,),
        in_specs=[row(D_MODEL), _full((1, D_MODEL)), _full(w1.shape), _full(w3.shape),
                  _full(w2.shape)],
        out_specs=row(D_MODEL),
        out_shape=jax.ShapeDtypeStruct((T, D_MODEL), F32),
        compiler_params=_params("parallel"),
        name="ffn",
    )(x2, g, w1, w3, w2)


def _block_diag(w):
    n, i, j = w.shape
    eye = jnp.eye(n, dtype=w.dtype)
    return jnp.einsum("nij,nm->nimj", w, eye).reshape(n * i, n * j)


def _pad_heads(w, heads, width, pad_to):
    lead = w.shape[:-1]
    w = w.reshape(lead + (heads, width))
    w = jnp.pad(w, [(0, 0)] * len(lead) + [(0, 0), (0, pad_to - width)])
    return w.reshape(lead + (heads * pad_to,))


def _row(v):
    return v.reshape(1, -1).astype(F32)


def kernel(x, mem, positions, norm_mix, norm_xattn, norm_mem, norm_ffn, w_in, b_gate, rwkv_mu, rwkv_w0, rwkv_w_up, rwkv_a0, rwkv_a_up, rwkv_g_up, rwkv_k_k, rwkv_k_a, rwkv_r_k, rwkv_ln_g, rwkv_ln_b, lru_conv_w, lru_conv_b, lru_wa, lru_ba, lru_wx, lru_bx, lru_lambda, mla_q_norm, mla_w_uq, mla_kv_norm, mla_w_ukv, mla_q_gain, mla_k_gain, w_branch, w_out, xa_w_q, xa_w_kv, xa_q_gain, xa_k_gain, xa_w_o, ffn_w1, ffn_w3, ffn_w2):
    B, S, D = x.shape
    T = B * S
    depth = w_in.shape[0]
    x2 = x.reshape(T, D)
    mem2 = mem.reshape(B * N_MEM, D)
    pos = positions.reshape(T, 1)

    inv_freq = ROPE_THETA ** (-jnp.arange(0, MLA_ROPE, 2, dtype=F32) / MLA_ROPE)
    half = MLA_ROPE // 2
    freq = jnp.zeros((LANES,), F32).at[MLA_NOPE:MLA_NOPE + half].set(inv_freq)
    freq = freq.at[MLA_NOPE + half:MLA_QK].set(inv_freq).reshape(1, LANES)
    sign = jnp.zeros((LANES,), F32).at[MLA_NOPE:MLA_NOPE + half].set(-1.0)
    sign = sign.at[MLA_NOPE + half:MLA_QK].set(1.0).reshape(1, LANES)
    head_of_lane = jnp.arange(RWKV_W) // RWKV_HEAD_DIM
    bd_ones = (head_of_lane[:, None] == head_of_lane[None, :]).astype(BF16)

    for l in range(depth):
        w = w_in[l]
        w_rwkv = w[:, :RWKV_IN].astype(BF16)
        w_lru = w[:, LRU_OFF:MLA_OFF].astype(BF16)
        w_mla = jnp.concatenate(
            [w[:, MLA_OFF:MLA_OFF + Q_RANK + KV_RANK], jnp.zeros((D, MLA_NOPE), F32),
             w[:, MLA_OFF + Q_RANK + KV_RANK:GATE_OFF],
             jnp.zeros((D, LANES - MLA_QK), F32)], axis=1).astype(BF16)
        w_gate = w[:, GATE_OFF:].astype(BF16)

        p_rwkv, p_lru, p_mla = _in_proj(x2, _row(norm_mix[l]), w_rwkv, w_lru, w_mla)

        zeros_lora = jnp.zeros((W_LORA, RWKV_W), F32)
        wup_pad = jnp.concatenate([rwkv_w_up[l], zeros_lora], axis=0).astype(BF16)
        aup_pad = jnp.concatenate([zeros_lora, rwkv_a_up[l]], axis=0).astype(BF16)
        prep = _rwkv_prep(p_rwkv.reshape(B, S, RWKV_IN), _row(rwkv_mu[l]), _row(rwkv_w0[l]), wup_pad,
                          _row(rwkv_a0[l]), aup_pad, rwkv_g_up[l].astype(BF16), _row(rwkv_k_k[l]),
                          _row(rwkv_k_a[l]), _row(rwkv_r_k[l]), bd_ones)
        y_a = _rwkv_scan(*prep, _row(rwkv_ln_g[l]), _row(rwkv_ln_b[l]), bd_ones).reshape(T, RWKV_W)

        y_b = _lru(p_lru.reshape(B, S, 2 * LRU_W), lru_conv_w[l], _row(lru_conv_b[l]),
                   _block_diag(lru_wa[l]).astype(BF16), _row(lru_ba[l]),
                   _block_diag(lru_wx[l]).astype(BF16), _row(lru_bx[l]),
                   _row(lru_lambda[l])).reshape(T, LRU_W)

        wuq = _pad_heads(mla_w_uq[l], MLA_HEADS, MLA_QK, MLA_HEAD_PAD).astype(BF16)
        wukv = mla_w_ukv[l].reshape(KV_RANK, MLA_HEADS, MLA_NOPE + MLA_V)
        wuk = _pad_heads(wukv[:, :, :MLA_NOPE].reshape(KV_RANK, -1), MLA_HEADS, MLA_NOPE,
                         MLA_HEAD_PAD).astype(BF16)
        wuv = wukv[:, :, MLA_NOPE:].reshape(KV_RANK, -1).T.astype(BF16)
        pad_gain = lambda gvec: jnp.pad(gvec, (0, LANES - MLA_QK)).reshape(1, LANES)
        q, k, vt = _mla_prep(p_mla, pos, freq, sign, _row(mla_q_norm[l]), wuq, _row(mla_kv_norm[l]),
                             wuk, wuv, pad_gain(mla_q_gain[l]), pad_gain(mla_k_gain[l]), B, S)
        W = MLA_HEADS * MLA_HEAD_PAD
        y_c = _mla_attn(q.reshape(B, S, W), k.reshape(B, S, W), vt).reshape(T, MLA_HEADS * MLA_V)

        x2 = _merge(x2, y_a, y_b, y_c, _row(norm_mix[l]), w_gate, _row(b_gate[l]),
                    w_branch[l].astype(BF16), w_out[l].astype(BF16))

        wkv = xa_w_kv[l].reshape(D, XA_HEADS, 2, XA_HEAD_DIM)
        mk, mv = _mem_kv(mem2, _row(norm_mem[l]), wkv[:, :, 0].reshape(D, XA_W).astype(BF16),
                         wkv[:, :, 1].reshape(D, XA_W).astype(BF16), _row(xa_k_gain[l]))
        x2 = _xattn(x2.reshape(B, S, D), mk.reshape(B, N_MEM, XA_W), mv.reshape(B, N_MEM, XA_W),
                    _row(norm_xattn[l]), xa_w_q[l].astype(BF16), _row(xa_q_gain[l]),
                    xa_w_o[l].astype(BF16)).reshape(T, D)

        x2 = _ffn(x2, _row(norm_ffn[l]), ffn_w1[l].astype(BF16), ffn_w3[l].astype(BF16),
                  ffn_w2[l].astype(BF16))
    return x2.reshape(B, S, D)
```

```python
import functools
import math

import jax
import jax.numpy as jnp
from jax import lax
from jax.experimental import pallas as pl
from jax.experimental.pallas import tpu as pltpu

F32 = jnp.float32
BF16 = jnp.bfloat16
HIGHEST = lax.Precision.HIGHEST

D_MODEL = 1024
N_MEM = 256
RWKV_HEADS = 8
RWKV_HEAD_DIM = 64
RWKV_W = RWKV_HEADS * RWKV_HEAD_DIM
W_LORA = 64
A_LORA = 64
G_LORA = 128
RWKV_IN = 3 * RWKV_W + W_LORA + A_LORA + G_LORA
RWKV_LN_EPS = RWKV_HEAD_DIM * 1e-5
LRU_BLOCKS = 8
LRU_W = 512
CONV_WIDTH = 4
LRU_C = 8.0
MLA_HEADS = 8
MLA_NOPE = 64
MLA_ROPE = 32
MLA_QK = MLA_NOPE + MLA_ROPE
MLA_V = 64
Q_RANK = 256
KV_RANK = 128
ROPE_THETA = 10000.0
N_BRANCH = 3
BRANCH_W = 512
XA_HEADS = 4
XA_HEAD_DIM = 128
XA_W = XA_HEADS * XA_HEAD_DIM
D_FF = -(-8 * D_MODEL // (3 * 256)) * 256
LRU_OFF = RWKV_IN
MLA_OFF = LRU_OFF + 2 * LRU_W
GATE_OFF = MLA_OFF + Q_RANK + KV_RANK + MLA_ROPE

LANES = 128
SUBLANES = 8
BF16_SUBLANES = 16
MASKED_SCORE = -2.0 ** 100
VMEM_LIMIT = 56 * 1024 * 1024
MLA_PAD = 4 * LANES
MLA_HEAD_PAD = LANES
RWKV_CHUNK = 64
RWKV_GROUP = 8
RWKV_LANE_GROUP = 4
TOK_TILE = 512
ATT_TILE = 256


def _params(*sem):
    return pltpu.CompilerParams(dimension_semantics=sem, vmem_limit_bytes=VMEM_LIMIT)


def _dot(a, b):
    return jnp.dot(a.astype(BF16), b.astype(BF16), preferred_element_type=F32)


def _dot_nt(a, b):
    return lax.dot_general(a.astype(BF16), b.astype(BF16), (((1,), (1,)), ((), ())),
                           preferred_element_type=F32)


def _sigmoid(x):
    return 1.0 / (1.0 + jnp.exp(-x))


def _rms(x, g, eps=1e-6):
    return x * lax.rsqrt(jnp.mean(x * x, axis=-1, keepdims=True) + eps) * g


def _segsum(x, bd):
    hi = x.astype(BF16)
    lo = (x - hi.astype(F32)).astype(BF16)
    return (jnp.dot(hi, bd, preferred_element_type=F32)
            + jnp.dot(lo, bd, preferred_element_type=F32))


def _full(shape):
    n = len(shape)
    return pl.BlockSpec(shape, lambda *_: (0,) * n)


def _in_proj_kernel(x_ref, g_ref, wr_ref, wl_ref, wm_ref, or_ref, ol_ref, om_ref):
    h = _rms(x_ref[...], g_ref[...]).astype(BF16)
    or_ref[...] = jnp.dot(h, wr_ref[...], preferred_element_type=F32)
    ol_ref[...] = jnp.dot(h, wl_ref[...], preferred_element_type=F32)
    om_ref[...] = jnp.dot(h, wm_ref[...], preferred_element_type=F32)


def _in_proj(x2, g, w_rwkv, w_lru, w_mla):
    T = x2.shape[0]
    tm = min(TOK_TILE, T)
    row = lambda n: pl.BlockSpec((tm, n), lambda i: (i, 0))
    return pl.pallas_call(
        _in_proj_kernel,
        grid=(T // tm,),
        in_specs=[row(D_MODEL), _full((1, D_MODEL)), _full(w_rwkv.shape), _full(w_lru.shape),
                  _full(w_mla.shape)],
        out_specs=[row(RWKV_IN), row(2 * LRU_W), row(MLA_PAD)],
        out_shape=[jax.ShapeDtypeStruct((T, RWKV_IN), F32),
                   jax.ShapeDtypeStruct((T, 2 * LRU_W), F32),
                   jax.ShapeDtypeStruct((T, MLA_PAD), F32)],
        compiler_params=_params("parallel"),
        name="in_proj",
    )(x2, g, w_rwkv, w_lru, w_mla)


def _rwkv_prep_kernel(p_ref, mu_ref, w0_ref, wup_ref, a0_ref, aup_ref, gup_ref, kk_ref, ka_ref,
                      rk_ref, bd_ref, r_o, lw_o, k_o, v_o, kk_o, kka_o, g_o, bonus_o, carry):
    @pl.when(pl.program_id(1) == 0)
    def _():
        carry[...] = jnp.zeros_like(carry)

    p = p_ref[0]
    ts = p.shape[0]
    rows = lax.broadcasted_iota(jnp.int32, p.shape, 0)
    prev = jnp.where(rows == 0, carry[SUBLANES - 1:SUBLANES, :], pltpu.roll(p, 1, 0))
    carry[...] = p[ts - SUBLANES:, :]
    pm = p + (prev - p) * mu_ref[...]
    o1, o2, o3 = RWKV_W, 2 * RWKV_W, 3 * RWKV_W
    r, k, v = pm[:, :o1], pm[:, o1:o2], pm[:, o2:o3]
    wa = pm[:, o3:o3 + W_LORA + A_LORA]
    gd = pm[:, o3 + W_LORA + A_LORA:]
    z = w0_ref[...] + _dot(jnp.tanh(wa), wup_ref[...])
    lw = -math.exp(-0.5) * _sigmoid(z)
    a = _sigmoid(a0_ref[...] + _dot(wa, aup_ref[...]))
    g = _dot(_sigmoid(gd), gup_ref[...])
    bd = bd_ref[...]
    kk = k * kk_ref[...]
    kk = kk / jnp.maximum(jnp.sqrt(_segsum(kk * kk, bd)), 1e-12)
    k2 = k * (1.0 + (a - 1.0) * ka_ref[...])
    bonus = _segsum(r * k2 * rk_ref[...], bd) * v
    r_o[0] = r
    lw_o[0] = lw
    k_o[0] = k2
    v_o[0] = v
    kk_o[0] = kk
    kka_o[0] = kk * a
    g_o[0] = g
    bonus_o[0] = bonus


def _rwkv_prep(p3, mu, w0, wup_pad, a0, aup_pad, gup, k_k, k_a, r_k, bd):
    B, S, _ = p3.shape
    ts = min(TOK_TILE, S)
    vec = _full((1, RWKV_W))
    out_spec = pl.BlockSpec((1, ts, RWKV_W), lambda b, s: (b, s, 0))
    out_shape = jax.ShapeDtypeStruct((B, S, RWKV_W), F32)
    return pl.pallas_call(
        _rwkv_prep_kernel,
        grid=(B, S // ts),
        in_specs=[pl.BlockSpec((1, ts, RWKV_IN), lambda b, s: (b, s, 0)), _full((1, RWKV_IN)),
                  vec, _full(wup_pad.shape), vec, _full(aup_pad.shape), _full(gup.shape),
                  vec, vec, vec, _full(bd.shape)],
        out_specs=[out_spec] * 8,
        out_shape=[out_shape] * 8,
        scratch_shapes=[pltpu.VMEM((SUBLANES, RWKV_IN), F32)],
        compiler_params=_params("parallel", "arbitrary"),
        name="rwkv_prep",
    )(p3, mu, w0, wup_pad, a0, aup_pad, gup, k_k, k_a, r_k, bd)


def _cumsum_rows(ltri, x):
    hi = x.astype(BF16)
    r1 = x - hi.astype(F32)
    mid = r1.astype(BF16)
    lo = (r1 - mid.astype(F32)).astype(BF16)
    dot = lambda t: jnp.dot(ltri, t, preferred_element_type=F32)
    return dot(hi) + dot(mid) + dot(lo)


def _rwkv_chunk_kernel(r_ref, lw_ref, k_ref, v_ref, kk_ref, kka_ref, g_ref, bonus_ref, lng_ref,
                       lnb_ref, ltri_ref, bd_ref, y_ref, s_ref):
    @pl.when(pl.program_id(1) == 0)
    def _():
        s_ref[...] = jnp.zeros_like(s_ref)

    C, N = RWKV_CHUNK, RWKV_HEAD_DIM
    GW = RWKV_LANE_GROUP * N
    n_groups = RWKV_W // GW
    G = r_ref.shape[1] // C

    lane_head = lax.broadcasted_iota(jnp.int32, (C, GW), 1) // N
    head_sel = [lane_head == h for h in range(RWKV_LANE_GROUP)]

    def bdiag(x):
        zero = jnp.zeros_like(x)
        return jnp.concatenate([jnp.where(sel, x, zero) for sel in head_sel], axis=0)

    trow = lax.broadcasted_iota(jnp.int32, (C, GW), 0)
    tcol = lax.broadcasted_iota(jnp.int32, (C, GW), 1) % C
    strict = tcol < trow
    incl = tcol <= trow
    eye = (tcol == trow).astype(F32)
    vrow = lax.broadcasted_iota(jnp.int32, (GW, GW), 0) // N
    vcol = lax.broadcasted_iota(jnp.int32, (GW, GW), 1) // N
    same_head = vrow == vcol

    contract0 = (((0,), (0,)), ((), ()))
    contract1 = (((1,), (1,)), ((), ()))
    mm = lambda a, b: jnp.dot(a, b, preferred_element_type=F32)

    items = []
    for c in range(G):
        rows = slice(c * C, (c + 1) * C)
        lw = lw_ref[0, rows, :]
        cum = _cumsum_rows(ltri_ref[...], lw)
        ge = jnp.exp(cum)
        gi = jnp.exp(-cum)
        At_all = (-kk_ref[0, rows, :] * jnp.exp(cum - lw)).astype(BF16)
        Bt_all = (kka_ref[0, rows, :] * gi).astype(BF16)
        Kt_all = (k_ref[0, rows, :] * gi).astype(BF16)
        Rt_all = (r_ref[0, rows, :] * ge).astype(BF16)
        V_all = v_ref[0, rows, :].astype(BF16)
        for j in range(n_groups):
            sl = slice(j * GW, (j + 1) * GW)
            items.append(dict(At=At_all[:, sl], Bt=Bt_all[:, sl], Kt=Kt_all[:, sl],
                              Rt=Rt_all[:, sl], V=V_all[:, sl], g_row=ge[C - 1:C, sl]))

    for it in items:
        AR = jnp.concatenate([it["At"], it["Rt"]], axis=0)
        BK = jnp.concatenate([bdiag(it["Bt"]), bdiag(it["Kt"])], axis=0)
        sc = lax.dot_general(AR, BK, contract1, preferred_element_type=F32)
        it["L"] = jnp.where(strict, sc[:C, :GW], 0.0)
        it["akm"] = jnp.where(strict, sc[:C, GW:], 0.0).astype(BF16)
        it["rbm"] = jnp.where(incl, sc[C:, :GW], 0.0).astype(BF16)
        it["rkm"] = jnp.where(incl, sc[C:, GW:], 0.0).astype(BF16)

    blk = 2
    for it in items:
        it["T"] = eye + jnp.where(trow // blk == tcol // blk, it["L"], 0.0)
    while blk < C:
        lvl = (trow // (2 * blk) == tcol // (2 * blk)) & (trow // blk != tcol // blk)
        for it in items:
            it["Tb"] = it["T"].astype(BF16)
            it["P"] = mm(it["Tb"], bdiag(jnp.where(lvl, it["L"], 0.0).astype(BF16))).astype(BF16)
        for it in items:
            it["T"] = it["T"] + mm(it["P"], bdiag(it["Tb"]))
        blk *= 2

    for it in items:
        it["Vd"] = bdiag(it["V"])
        it["akv"] = mm(it["akm"], it["Vd"]).astype(BF16)
    for it in items:
        au = mm(it["T"].astype(BF16), jnp.concatenate([bdiag(it["At"]), bdiag(it["akv"])], axis=1))
        it["A2"] = au[:, :GW].astype(BF16)
        it["U0"] = au[:, GW:].astype(BF16)
    for it in items:
        ry = mm(it["rbm"], jnp.concatenate([bdiag(it["A2"]), bdiag(it["U0"])], axis=1))
        it["R2"] = (it["Rt"].astype(F32) + ry[:, :GW]).astype(BF16)
        it["Y0"] = ry[:, GW:] + mm(it["rkm"], it["Vd"])
        it["Mq"] = (jnp.where(same_head, lax.dot_general(it["A2"], it["Bt"], contract0,
                                                         preferred_element_type=F32), 0.0)
                    * it["g_row"]).astype(BF16)
        it["Nq"] = jnp.where(same_head,
                             lax.dot_general(jnp.concatenate([it["U0"], it["V"]], axis=0),
                                             jnp.concatenate([it["Bt"], it["Kt"]], axis=0),
                                             contract0, preferred_element_type=F32),
                             0.0) * it["g_row"]

    y_chunks = []
    for c in range(G):
        ys = []
        for j in range(n_groups):
            it = items[c * n_groups + j]
            S0 = s_ref[j]
            Sb = S0.astype(BF16)
            ys.append(it["Y0"] + lax.dot_general(it["R2"], Sb, contract1,
                                                 preferred_element_type=F32))
            s_ref[j] = S0 * it["g_row"] + mm(Sb, it["Mq"]) + it["Nq"]
        y_chunks.append(jnp.concatenate(ys, axis=1))
    Y = jnp.concatenate(y_chunks, axis=0)

    bd = bd_ref[0:GW, 0:GW]

    def head_sums(x):
        return jnp.concatenate([mm(x[:, j * GW:(j + 1) * GW].astype(BF16), bd)
                                for j in range(n_groups)], axis=1)

    inv_n = 1.0 / N
    mean = head_sums(Y) * inv_n
    yc = Y - mean
    var = head_sums(yc * yc) * inv_n
    yn = yc * lax.rsqrt(var + RWKV_LN_EPS) * lng_ref[...] + lnb_ref[...]
    y_ref[0] = (yn + bonus_ref[0]) * g_ref[0]


def _rwkv_scan(r, lw, k, v, kk, kka, g, bonus, ln_g, ln_b, bd):
    B, S, W = r.shape
    C = RWKV_CHUNK
    ts = min(RWKV_GROUP * C, S)
    tok = pl.BlockSpec((1, ts, W), lambda b, c: (b, c, 0))
    vec = _full((1, W))
    ltri = (jnp.arange(C)[None, :] <= jnp.arange(C)[:, None]).astype(BF16)
    group_w = RWKV_LANE_GROUP * RWKV_HEAD_DIM
    return pl.pallas_call(
        _rwkv_chunk_kernel,
        grid=(B, S // ts),
        in_specs=[tok] * 8 + [vec, vec, _full((C, C)), _full((W, W))],
        out_specs=tok,
        out_shape=jax.ShapeDtypeStruct((B, S, W), F32),
        scratch_shapes=[pltpu.VMEM((W // group_w, group_w, group_w), F32)],
        compiler_params=_params("parallel", "arbitrary"),
        name="rwkv_chunk",
    )(r, lw, k, v, kk, kka, g, bonus, ln_g, ln_b, ltri, bd)


def _shift_rows(x, d, fill, rows):
    return jnp.where(rows < d, fill, pltpu.roll(x, d, 0))


def _lru_kernel(p_ref, cw_ref, cb_ref, wa_ref, ba_ref, wx_ref, bx_ref, lam_ref, y_ref,
                xpad, hcarry):
    @pl.when(pl.program_id(1) == 0)
    def _():
        xpad[0:SUBLANES, :] = jnp.zeros((SUBLANES, LRU_W), F32)
        hcarry[...] = jnp.zeros_like(hcarry)

    p = p_ref[0]
    ts = p.shape[0]
    xb, gb = p[:, :LRU_W], p[:, LRU_W:]
    xpad[SUBLANES:, :] = xb
    cw = cw_ref[...]
    xc = cb_ref[...] + xb * cw[CONV_WIDTH - 1:CONV_WIDTH, :]
    for j in range(CONV_WIDTH - 1):
        lo = SUBLANES - (CONV_WIDTH - 1) + j
        xc = xc + xpad[lo:lo + ts, :] * cw[j:j + 1, :]
    xpad[0:SUBLANES, :] = xb[ts - SUBLANES:, :]
    rg = _sigmoid(_dot(xc, wa_ref[...]) + ba_ref[...])
    ig = _sigmoid(_dot(xc, wx_ref[...]) + bx_ref[...])
    lam = lam_ref[...]
    softplus_neg_lam = jnp.maximum(-lam, 0.0) + jnp.log(1.0 + jnp.exp(-jnp.abs(lam)))
    log_a = -LRU_C * rg * softplus_neg_lam
    a = jnp.exp(log_a)
    u = jnp.sqrt(1.0 - jnp.exp(2.0 * log_a)) * (ig * xc)
    rows = lax.broadcasted_iota(jnp.int32, a.shape, 0)
    d = 1
    while d < ts:
        u = u + a * _shift_rows(u, d, 0.0, rows)
        a = a * _shift_rows(a, d, 1.0, rows)
        d *= 2
    h = u + a * hcarry[SUBLANES - 1:SUBLANES, :]
    hcarry[...] = h[ts - SUBLANES:, :]
    gelu = 0.5 * gb * (1.0 + jnp.tanh(math.sqrt(2.0 / math.pi) * (gb + 0.044715 * gb * gb * gb)))
    y_ref[0] = h * gelu


def _lru(p3, conv_w, conv_b, wa_bd, ba, wx_bd, bx, lam):
    B, S, _ = p3.shape
    ts = min(TOK_TILE, S)
    vec = _full((1, LRU_W))
    return pl.pallas_call(
        _lru_kernel,
        grid=(B, S // ts),
        in_specs=[pl.BlockSpec((1, ts, 2 * LRU_W), lambda b, s: (b, s, 0)),
                  _full((CONV_WIDTH, LRU_W)), vec, _full((LRU_W, LRU_W)), vec,
                  _full((LRU_W, LRU_W)), vec, vec],
        out_specs=pl.BlockSpec((1, ts, LRU_W), lambda b, s: (b, s, 0)),
        out_shape=jax.ShapeDtypeStruct((B, S, LRU_W), F32),
        scratch_shapes=[pltpu.VMEM((ts + SUBLANES, LRU_W), F32), pltpu.VMEM((SUBLANES, LRU_W), F32)],
        compiler_params=_params("parallel", "arbitrary"),
        name="rglru",
    )(p3, conv_w, conv_b, wa_bd, ba, wx_bd, bx, lam)


def _mla_prep_kernel(p_ref, pos_ref, freq_ref, sign_ref, qn_ref, wuq_ref, kvn_ref, wuk_ref,
                     wuv_ref, qg_ref, kg_ref, q_o, k_o, vt_o):
    p = p_ref[...]
    cq = p[:, :Q_RANK]
    ckv = p[:, Q_RANK:Q_RANK + KV_RANK]
    kr = p[:, Q_RANK + KV_RANK:]
    ang = pos_ref[...].astype(F32) * freq_ref[...]
    cosf = jnp.cos(ang)
    sinf = jnp.sin(ang) * sign_ref[...]
    lane = lax.broadcasted_iota(jnp.int32, ang.shape, 1)
    first_half = lane < MLA_NOPE + MLA_ROPE // 2

    def rope(x):
        half = MLA_ROPE // 2
        swapped = jnp.where(first_half, pltpu.roll(x, LANES - half, 1), pltpu.roll(x, half, 1))
        return x * cosf + swapped * sinf

    q = _dot(_rms(cq, qn_ref[...]), wuq_ref[...])
    ckv_n = _rms(ckv, kvn_ref[...]).astype(BF16)
    kn = jnp.dot(ckv_n, wuk_ref[...], preferred_element_type=F32)
    vt_o[0] = _dot_nt(wuv_ref[...], ckv_n).astype(BF16)
    scale = MLA_QK ** -0.5 * math.log2(math.e)
    for h in range(MLA_HEADS):
        sl = slice(h * MLA_HEAD_PAD, (h + 1) * MLA_HEAD_PAD)
        qh = q[:, sl]
        qh = qh * lax.rsqrt(jnp.sum(qh * qh, axis=-1, keepdims=True) / MLA_QK + 1e-6) * qg_ref[...]
        q_o[:, sl] = (rope(qh) * scale).astype(BF16)
        kh = kn[:, sl] + kr
        kh = kh * lax.rsqrt(jnp.sum(kh * kh, axis=-1, keepdims=True) / MLA_QK + 1e-6) * kg_ref[...]
        k_o[:, sl] = rope(kh).astype(BF16)


def _mla_prep(p_mla, pos, freq, sign, q_norm, wuq, kv_norm, wuk, wuv, q_gain, k_gain, B, S):
    T = p_mla.shape[0]
    tm = min(TOK_TILE, S)
    per_row = S // tm
    W = MLA_HEADS * MLA_HEAD_PAD
    row = lambda n: pl.BlockSpec((tm, n), lambda i: (i, 0))
    return pl.pallas_call(
        _mla_prep_kernel,
        grid=(T // tm,),
        in_specs=[row(MLA_PAD), row(1), _full((1, LANES)), _full((1, LANES)), _full((1, Q_RANK)),
                  _full(wuq.shape), _full((1, KV_RANK)), _full(wuk.shape), _full(wuv.shape),
                  _full((1, LANES)), _full((1, LANES))],
        out_specs=[row(W), row(W),
                   pl.BlockSpec((1, MLA_HEADS * MLA_V, tm), lambda i: (i // per_row, 0, i % per_row))],
        out_shape=[jax.ShapeDtypeStruct((T, W), BF16), jax.ShapeDtypeStruct((T, W), BF16),
                   jax.ShapeDtypeStruct((B, MLA_HEADS * MLA_V, S), BF16)],
        compiler_params=_params("parallel"),
        name="mla_prep",
    )(p_mla, pos, freq, sign, q_norm, wuq, kv_norm, wuk, wuv, q_gain, k_gain)


def _mla_attn_kernel(q_ref, k_ref, vt_ref, o_ref, acc_ref, m_ref, l_ref, s_ref, p_ref, a_ref):
    qi = pl.program_id(1)
    tq = q_ref.shape[1]
    m_ref[...] = jnp.full(m_ref.shape, MASKED_SCORE, F32)
    l_ref[...] = jnp.zeros(l_ref.shape, F32)
    acc_ref[...] = jnp.zeros(acc_ref.shape, F32)
    ones_rows = jnp.ones((BF16_SUBLANES, tq), BF16)

    def tile(j, masked):
        start = pl.multiple_of(j * tq, tq)
        for h in range(MLA_HEADS):
            hs = slice(h * MLA_HEAD_PAD, (h + 1) * MLA_HEAD_PAD)
            kb = k_ref[0, pl.ds(start, tq), hs]
            st = lax.dot_general(kb, q_ref[0, :, hs], (((1,), (1,)), ((), ())),
                                 preferred_element_type=F32)
            if masked:
                kpos = lax.broadcasted_iota(jnp.int32, st.shape, 0)
                qpos = lax.broadcasted_iota(jnp.int32, st.shape, 1)
                st = jnp.where(kpos <= qpos, st, MASKED_SCORE)
            s_ref[h] = st.astype(BF16)
        for h in range(MLA_HEADS):
            sb = s_ref[h]
            m_old = m_ref[h:h + 1, :]
            m_new = jnp.maximum(m_old, jnp.max(sb, axis=0, keepdims=True).astype(F32))
            p_ref[h] = jnp.exp2(sb - m_new.astype(BF16))
            a_ref[h:h + 1, :] = jnp.exp2(m_old - m_new)
            m_ref[h:h + 1, :] = m_new
        for h in range(MLA_HEADS):
            vs = slice(h * MLA_V, (h + 1) * MLA_V)
            pr = p_ref[h]
            pv = jnp.dot(vt_ref[0, vs, pl.ds(start, tq)], pr,
                         preferred_element_type=F32)
            psum = jnp.dot(ones_rows, pr, preferred_element_type=F32)[0:1, :]
            alpha = a_ref[h:h + 1, :]
            l_ref[h:h + 1, :] = alpha * l_ref[h:h + 1, :] + psum
            acc_ref[vs, :] = alpha * acc_ref[vs, :] + pv

    def body(j, carry):
        tile(j, False)
        return carry

    lax.fori_loop(0, qi, body, 0)
    tile(qi, True)
    for h in range(MLA_HEADS):
        vs = slice(h * MLA_V, (h + 1) * MLA_V)
        acc_ref[vs, :] = acc_ref[vs, :] / l_ref[h:h + 1, :]
    o_ref[0] = acc_ref[...].T


def _mla_attn(q, k, vt):
    B, S, W = q.shape
    tq = min(ATT_TILE, S)
    WV = MLA_HEADS * MLA_V
    return pl.pallas_call(
        _mla_attn_kernel,
        grid=(B, S // tq),
        in_specs=[pl.BlockSpec((1, tq, W), lambda b, i: (b, i, 0)),
                  pl.BlockSpec((1, S, W), lambda b, i: (b, 0, 0)),
                  pl.BlockSpec((1, WV, S), lambda b, i: (b, 0, 0))],
        out_specs=pl.BlockSpec((1, tq, WV), lambda b, i: (b, i, 0)),
        out_shape=jax.ShapeDtypeStruct((B, S, WV), F32),
        scratch_shapes=[pltpu.VMEM((WV, tq), F32), pltpu.VMEM((MLA_HEADS, tq), F32),
                        pltpu.VMEM((MLA_HEADS, tq), F32), pltpu.VMEM((MLA_HEADS, tq, tq), BF16),
                        pltpu.VMEM((MLA_HEADS, tq, tq), BF16), pltpu.VMEM((MLA_HEADS, tq), F32)],
        compiler_params=_params("parallel", "arbitrary"),
        name="mla_attn",
    )(q, k, vt)


def _merge_kernel(x_ref, ya_ref, yb_ref, yc_ref, g_ref, wg_ref, bg_ref, wb_ref, wo_ref, o_ref):
    x = x_ref[...]
    h = _rms(x, g_ref[...]).astype(BF16)
    merged = None
    for n, y_ref in enumerate((ya_ref, yb_ref, yc_ref)):
        sl = slice(n * D_MODEL, (n + 1) * D_MODEL)
        gate = _sigmoid(jnp.dot(h, wg_ref[:, sl], preferred_element_type=F32) + bg_ref[:, sl])
        term = gate * _dot(y_ref[...], wb_ref[n])
        merged = term if merged is None else merged + term
    o_ref[...] = x + _dot(merged, wo_ref[...])


def _merge(x2, ya, yb, yc, g, w_gate, b_gate, w_branch, w_out):
    T = x2.shape[0]
    tm = min(TOK_TILE, T)
    row = lambda n: pl.BlockSpec((tm, n), lambda i: (i, 0))
    return pl.pallas_call(
        _merge_kernel,
        grid=(T // tm,),
        in_specs=[row(D_MODEL), row(BRANCH_W), row(BRANCH_W), row(BRANCH_W), _full((1, D_MODEL)),
                  _full(w_gate.shape), _full(b_gate.shape), _full(w_branch.shape),
                  _full(w_out.shape)],
        out_specs=row(D_MODEL),
        out_shape=jax.ShapeDtypeStruct((T, D_MODEL), F32),
        compiler_params=_params("parallel"),
        name="merge",
    )(x2, ya, yb, yc, g, w_gate, b_gate, w_branch, w_out)


def _mem_kv_kernel(m_ref, g_ref, wk_ref, wv_ref, kg_ref, k_o, v_o):
    h = _rms(m_ref[...], g_ref[...]).astype(BF16)
    k = jnp.dot(h, wk_ref[...], preferred_element_type=F32)
    v_o[...] = jnp.dot(h, wv_ref[...], preferred_element_type=F32).astype(BF16)
    for hd in range(XA_HEADS):
        sl = slice(hd * XA_HEAD_DIM, (hd + 1) * XA_HEAD_DIM)
        k_o[:, sl] = _rms(k[:, sl], kg_ref[...]).astype(BF16)


def _mem_kv(mem2, g, wk, wv, k_gain):
    M = mem2.shape[0]
    tm = min(TOK_TILE, M)
    row = lambda n: pl.BlockSpec((tm, n), lambda i: (i, 0))
    return pl.pallas_call(
        _mem_kv_kernel,
        grid=(M // tm,),
        in_specs=[row(D_MODEL), _full((1, D_MODEL)), _full(wk.shape), _full(wv.shape),
                  _full((1, XA_HEAD_DIM))],
        out_specs=[row(XA_W), row(XA_W)],
        out_shape=[jax.ShapeDtypeStruct((M, XA_W), BF16)] * 2,
        compiler_params=_params("parallel"),
        name="mem_kv",
    )(mem2, g, wk, wv, k_gain)


def _xattn_kernel(x_ref, k_ref, v_ref, g_ref, wq_ref, qg_ref, wo_ref, o_ref):
    x = x_ref[0]
    h = _rms(x, g_ref[...]).astype(BF16)
    q = jnp.dot(h, wq_ref[...], preferred_element_type=F32)
    scale = XA_HEAD_DIM ** -0.5
    outs = []
    for hd in range(XA_HEADS):
        sl = slice(hd * XA_HEAD_DIM, (hd + 1) * XA_HEAD_DIM)
        qh = _rms(q[:, sl], qg_ref[...]) * scale
        s = _dot_nt(qh, k_ref[0, :, sl])
        pr = jnp.exp(s - jnp.max(s, axis=-1, keepdims=True))
        pr = pr / jnp.sum(pr, axis=-1, keepdims=True)
        outs.append(_dot(pr, v_ref[0, :, sl]))
    o = jnp.concatenate(outs, axis=-1)
    o_ref[0] = x + _dot(o, wo_ref[...])


def _xattn(x3, k3, v3, g, wq, q_gain, wo):
    B, S, _ = x3.shape
    ts = min(TOK_TILE, S)
    M = k3.shape[1]
    return pl.pallas_call(
        _xattn_kernel,
        grid=(B, S // ts),
        in_specs=[pl.BlockSpec((1, ts, D_MODEL), lambda b, s: (b, s, 0)),
                  pl.BlockSpec((1, M, XA_W), lambda b, s: (b, 0, 0)),
                  pl.BlockSpec((1, M, XA_W), lambda b, s: (b, 0, 0)),
                  _full((1, D_MODEL)), _full(wq.shape), _full((1, XA_HEAD_DIM)), _full(wo.shape)],
        out_specs=pl.BlockSpec((1, ts, D_MODEL), lambda b, s: (b, s, 0)),
        out_shape=jax.ShapeDtypeStruct((B, S, D_MODEL), F32),
        compiler_params=_params("parallel", "parallel"),
        name="xattn",
    )(x3, k3, v3, g, wq, q_gain, wo)


FF_SPLIT = 2


def _ffn_kernel(x_ref, g_ref, w1_ref, w3_ref, w2_ref, o_ref):
    x = x_ref[...]
    h = _rms(x, g_ref[...]).astype(BF16)
    step = D_FF // FF_SPLIT
    acc = x
    for c in range(FF_SPLIT):
        sl = slice(c * step, (c + 1) * step)
        a = jnp.dot(h, w1_ref[:, sl], preferred_element_type=F32)
        b = jnp.dot(h, w3_ref[:, sl], preferred_element_type=F32)
        z = a * _sigmoid(a) * b
        acc = acc + _dot(z, w2_ref[sl, :])
    o_ref[...] = acc


def _ffn(x2, g, w1, w3, w2):
    T = x2.shape[0]
    tm = min(TOK_TILE, T)
    row = lambda n: pl.BlockSpec((tm, n), lambda i: (i, 0))
    return pl.pallas_call(
        _ffn_kernel,
        grid=(T // tm,),
        in_specs=[row(D_MODEL), _full((1, D_MODEL)), _full(w1.shape), _full(w3.shape),
                  _full(w2.shape)],
        out_specs=row(D_MODEL),
        out_shape=jax.ShapeDtypeStruct((T, D_MODEL), F32),
        compiler_params=_params("parallel"),
        name="ffn",
    )(x2, g, w1, w3, w2)


def _block_diag(w):
    n, i, j = w.shape
    eye = jnp.eye(n, dtype=w.dtype)
    return jnp.einsum("nij,nm->nimj", w, eye).reshape(n * i, n * j)


def _pad_heads(w, heads, width, pad_to):
    lead = w.shape[:-1]
    w = w.reshape(lead + (heads, width))
    w = jnp.pad(w, [(0, 0)] * len(lead) + [(0, 0), (0, pad_to - width)])
    return w.reshape(lead + (heads * pad_to,))


def _row(v):
    return v.reshape(1, -1).astype(F32)


def kernel(x, mem, positions, norm_mix, norm_xattn, norm_mem, norm_ffn, w_in, b_gate, rwkv_mu, rwkv_w0, rwkv_w_up, rwkv_a0, rwkv_a_up, rwkv_g_up, rwkv_k_k, rwkv_k_a, rwkv_r_k, rwkv_ln_g, rwkv_ln_b, lru_conv_w, lru_conv_b, lru_wa, lru_ba, lru_wx, lru_bx, lru_lambda, mla_q_norm, mla_w_uq, mla_kv_norm, mla_w_ukv, mla_q_gain, mla_k_gain, w_branch, w_out, xa_w_q, xa_w_kv, xa_q_gain, xa_k_gain, xa_w_o, ffn_w1, ffn_w3, ffn_w2):
    B, S, D = x.shape
    T = B * S
    depth = w_in.shape[0]
    x2 = x.reshape(T, D)
    mem2 = mem.reshape(B * N_MEM, D)
    pos = positions.reshape(T, 1)

    inv_freq = ROPE_THETA ** (-jnp.arange(0, MLA_ROPE, 2, dtype=F32) / MLA_ROPE)
    half = MLA_ROPE // 2
    freq = jnp.zeros((LANES,), F32).at[MLA_NOPE:MLA_NOPE + half].set(inv_freq)
    freq = freq.at[MLA_NOPE + half:MLA_QK].set(inv_freq).reshape(1, LANES)
    sign = jnp.zeros((LANES,), F32).at[MLA_NOPE:MLA_NOPE + half].set(-1.0)
    sign = sign.at[MLA_NOPE + half:MLA_QK].set(1.0).reshape(1, LANES)
    head_of_lane = jnp.arange(RWKV_W) // RWKV_HEAD_DIM
    bd_ones = (head_of_lane[:, None] == head_of_lane[None, :]).astype(BF16)

    for l in range(depth):
        w = w_in[l]
        w_rwkv = w[:, :RWKV_IN].astype(BF16)
        w_lru = w[:, LRU_OFF:MLA_OFF].astype(BF16)
        w_mla = jnp.concatenate(
            [w[:, MLA_OFF:MLA_OFF + Q_RANK + KV_RANK], jnp.zeros((D, MLA_NOPE), F32),
             w[:, MLA_OFF + Q_RANK + KV_RANK:GATE_OFF],
             jnp.zeros((D, LANES - MLA_QK), F32)], axis=1).astype(BF16)
        w_gate = w[:, GATE_OFF:].astype(BF16)

        p_rwkv, p_lru, p_mla = _in_proj(x2, _row(norm_mix[l]), w_rwkv, w_lru, w_mla)

        zeros_lora = jnp.zeros((W_LORA, RWKV_W), F32)
        wup_pad = jnp.concatenate([rwkv_w_up[l], zeros_lora], axis=0).astype(BF16)
        aup_pad = jnp.concatenate([zeros_lora, rwkv_a_up[l]], axis=0).astype(BF16)
        prep = _rwkv_prep(p_rwkv.reshape(B, S, RWKV_IN), _row(rwkv_mu[l]), _row(rwkv_w0[l]), wup_pad,
                          _row(rwkv_a0[l]), aup_pad, rwkv_g_up[l].astype(BF16), _row(rwkv_k_k[l]),
                          _row(rwkv_k_a[l]), _row(rwkv_r_k[l]), bd_ones)
        y_a = _rwkv_scan(*prep, _row(rwkv_ln_g[l]), _row(rwkv_ln_b[l]), bd_ones).reshape(T, RWKV_W)

        y_b = _lru(p_lru.reshape(B, S, 2 * LRU_W), lru_conv_w[l], _row(lru_conv_b[l]),
                   _block_diag(lru_wa[l]).astype(BF16), _row(lru_ba[l]),
                   _block_diag(lru_wx[l]).astype(BF16), _row(lru_bx[l]),
                   _row(lru_lambda[l])).reshape(T, LRU_W)

        wuq = _pad_heads(mla_w_uq[l], MLA_HEADS, MLA_QK, MLA_HEAD_PAD).astype(BF16)
        wukv = mla_w_ukv[l].reshape(KV_RANK, MLA_HEADS, MLA_NOPE + MLA_V)
        wuk = _pad_heads(wukv[:, :, :MLA_NOPE].reshape(KV_RANK, -1), MLA_HEADS, MLA_NOPE,
                         MLA_HEAD_PAD).astype(BF16)
        wuv = wukv[:, :, MLA_NOPE:].reshape(KV_RANK, -1).T.astype(BF16)
        pad_gain = lambda gvec: jnp.pad(gvec, (0, LANES - MLA_QK)).reshape(1, LANES)
        q, k, vt = _mla_prep(p_mla, pos, freq, sign, _row(mla_q_norm[l]), wuq, _row(mla_kv_norm[l]),
                             wuk, wuv, pad_gain(mla_q_gain[l]), pad_gain(mla_k_gain[l]), B, S)
        W = MLA_HEADS * MLA_HEAD_PAD
        y_c = _mla_attn(q.reshape(B, S, W), k.reshape(B, S, W), vt).reshape(T, MLA_HEADS * MLA_V)

        x2 = _merge(x2, y_a, y_b, y_c, _row(norm_mix[l]), w_gate, _row(b_gate[l]),
                    w_branch[l].astype(BF16), w_out[l].astype(BF16))

        wkv = xa_w_kv[l].reshape(D, XA_HEADS, 2, XA_HEAD_DIM)
        mk, mv = _mem_kv(mem2, _row(norm_mem[l]), wkv[:, :, 0].reshape(D, XA_W).astype(BF16),
                         wkv[:, :, 1].reshape(D, XA_W).astype(BF16), _row(xa_k_gain[l]))
        x2 = _xattn(x2.reshape(B, S, D), mk.reshape(B, N_MEM, XA_W), mv.reshape(B, N_MEM, XA_W),
                    _row(norm_xattn[l]), xa_w_q[l].astype(BF16), _row(xa_q_gain[l]),
                    xa_w_o[l].astype(BF16)).reshape(T, D)

        x2 = _ffn(x2, _row(norm_ffn[l]), ffn_w1[l].astype(BF16), ffn_w3[l].astype(BF16),
                  ffn_w2[l].astype(BF16))
    return x2.reshape(B, S, D)
```

```python
import math

import jax
import jax.numpy as jnp
from jax import lax
from jax.experimental import pallas as pl
from jax.experimental.pallas import tpu as pltpu

F32 = jnp.float32
BF16 = jnp.bfloat16

D_MODEL = 1024
N_MEM = 256
RWKV_HEADS = 8
RWKV_HEAD_DIM = 64
RWKV_W = RWKV_HEADS * RWKV_HEAD_DIM
W_LORA = 64
A_LORA = 64
G_LORA = 128
RWKV_IN = 3 * RWKV_W + W_LORA + A_LORA + G_LORA
RWKV_LN_EPS = RWKV_HEAD_DIM * 1e-5
LRU_BLOCKS = 8
LRU_W = 512
CONV_WIDTH = 4
LRU_C = 8.0
MLA_HEADS = 8
MLA_NOPE = 64
MLA_ROPE = 32
MLA_QK = MLA_NOPE + MLA_ROPE
MLA_V = 64
Q_RANK = 256
KV_RANK = 128
ROPE_THETA = 10000.0
N_BRANCH = 3
BRANCH_W = 512
XA_HEADS = 4
XA_HEAD_DIM = 128
XA_W = XA_HEADS * XA_HEAD_DIM
D_FF = -(-8 * D_MODEL // (3 * 256)) * 256
LRU_OFF = RWKV_IN
MLA_OFF = LRU_OFF + 2 * LRU_W
GATE_OFF = MLA_OFF + Q_RANK + KV_RANK + MLA_ROPE

LANES = 128
SUBLANES = 8
BF16_SUBLANES = 16
MXU_TILE = 256
VMEM_LIMIT = 56 * 1024 * 1024
MLA_HEAD_PAD = LANES
MLA_PAD = Q_RANK + KV_RANK + 2 * LANES
MASKED_SCORE = -2.0 ** 100
RWKV_CHUNK = 64
RWKV_GROUP = 8
RWKV_LANE_GROUP = MXU_TILE // RWKV_HEAD_DIM
TOK_TILE = 512
ATT_TILE = 256


def _params(*sem):
    return pltpu.CompilerParams(dimension_semantics=sem, vmem_limit_bytes=VMEM_LIMIT)


def _mm(a, b):
    return jnp.dot(a, b, preferred_element_type=F32)


def _dot(a, b):
    return _mm(a.astype(BF16), b.astype(BF16))


def _dot_nt(a, b):
    return lax.dot_general(a.astype(BF16), b.astype(BF16), (((1,), (1,)), ((), ())),
                           preferred_element_type=F32)


def _sigmoid(x):
    return 1.0 / (1.0 + jnp.exp(-x))


def _rms(x, g, eps=1e-6):
    return x * lax.rsqrt(jnp.mean(x * x, axis=-1, keepdims=True) + eps) * g


def _full(shape):
    n = len(shape)
    return pl.BlockSpec(shape, lambda *_: (0,) * n)


def _layer(arr, l):
    tail = arr.shape[1:]
    return pl.BlockSpec((None,) + tail, lambda *_: (l,) + (0,) * len(tail))


def _rows(tm, n):
    return pl.BlockSpec((tm, n), lambda i: (i, 0))


def _rope_kernel(pos_ref, freq_ref, cos_o, sin_o, nsin_o):
    ang = pos_ref[...].astype(F32) * freq_ref[...]
    s = jnp.sin(ang)
    cos_o[...] = jnp.cos(ang)
    sin_o[...] = s
    nsin_o[...] = -s


def _rope_tables(pos_rep, freq_tile):
    R = pos_rep.shape[0]
    tm = min(TOK_TILE, R)
    out = jax.ShapeDtypeStruct((R, LANES), F32)
    return pl.pallas_call(
        _rope_kernel,
        grid=(R // tm,),
        in_specs=[_rows(tm, LANES), _full((1, LANES))],
        out_specs=[_rows(tm, LANES)] * 3,
        out_shape=[out] * 3,
        compiler_params=_params("parallel"),
        name="rope_tables",
    )(pos_rep, freq_tile)


def _in_proj_kernel(x_ref, g_ref, wrl_ref, wm_ref, or_ref, ol_ref, om_ref):
    h = _rms(x_ref[...], g_ref[...]).astype(BF16)
    p = _mm(h, wrl_ref[...])
    or_ref[...] = p[:, :RWKV_IN]
    ol_ref[...] = p[:, RWKV_IN:]
    om_ref[...] = _mm(h, wm_ref[...])


def _in_proj(x2, g, w_in_b, w_mla, l):
    T = x2.shape[0]
    tm = min(TOK_TILE, T)
    n_rl = RWKV_IN + 2 * LRU_W
    return pl.pallas_call(
        _in_proj_kernel,
        grid=(T // tm,),
        in_specs=[_rows(tm, D_MODEL), _layer(g, l),
                  pl.BlockSpec((None, D_MODEL, n_rl), lambda i: (l, 0, 0)),
                  _layer(w_mla, l)],
        out_specs=[_rows(tm, RWKV_IN), _rows(tm, 2 * LRU_W), _rows(tm, MLA_PAD)],
        out_shape=[jax.ShapeDtypeStruct((T, RWKV_IN), F32),
                   jax.ShapeDtypeStruct((T, 2 * LRU_W), F32),
                   jax.ShapeDtypeStruct((T, MLA_PAD), F32)],
        compiler_params=_params("parallel"),
        name="in_proj",
    )(x2, g, w_in_b, w_mla)


def _cumsum_rows(ltri, x):
    hi = x.astype(BF16)
    r1 = x - hi.astype(F32)
    mid = r1.astype(BF16)
    lo = (r1 - mid.astype(F32)).astype(BF16)
    return _mm(ltri, hi) + _mm(ltri, mid) + _mm(ltri, lo)


def _rwkv_kernel(p_ref, mu_ref, w0_ref, wup_ref, a0_ref, aup_ref, gup_ref, kkw_ref, ka_ref,
                 rk_ref, lng_ref, lnb_ref, ltri_ref, bd_ref, y_ref, carry, s_ref):
    @pl.when(pl.program_id(1) == 0)
    def _():
        carry[...] = jnp.zeros_like(carry)
        s_ref[...] = jnp.zeros_like(s_ref)

    C, N = RWKV_CHUNK, RWKV_HEAD_DIM
    GW = RWKV_LANE_GROUP * N
    n_groups = RWKV_W // GW
    bd = bd_ref[...]

    def head_sums(x):
        return jnp.concatenate([_mm(x[:, j * GW:(j + 1) * GW].astype(BF16), bd)
                                for j in range(n_groups)], axis=1)

    p = p_ref[0]
    ts = p.shape[0]
    G = ts // C
    prow = lax.broadcasted_iota(jnp.int32, p.shape, 0)
    prev = jnp.where(prow == 0, carry[SUBLANES - 1:SUBLANES, :], pltpu.roll(p, 1, 0))
    carry[...] = p[ts - SUBLANES:, :]
    pm = p + (prev - p) * mu_ref[...]
    o1, o2, o3 = RWKV_W, 2 * RWKV_W, 3 * RWKV_W
    r, k, v = pm[:, :o1], pm[:, o1:o2], pm[:, o2:o3]
    wa = pm[:, o3:o3 + W_LORA + A_LORA]
    gd = pm[:, o3 + W_LORA + A_LORA:]
    z = w0_ref[...] + _dot(jnp.tanh(wa), wup_ref[...])
    lw_all = -math.exp(-0.5) * _sigmoid(z)
    a = _sigmoid(a0_ref[...] + _dot(wa, aup_ref[...]))
    gate = _dot(_sigmoid(gd), gup_ref[...])
    kk = k * kkw_ref[...]
    kk = kk / jnp.maximum(jnp.sqrt(head_sums(kk * kk)), 1e-12)
    k2 = k * (1.0 + (a - 1.0) * ka_ref[...])
    bonus = head_sums(r * k2 * rk_ref[...]) * v
    kka = kk * a

    lane_head = lax.broadcasted_iota(jnp.int32, (C, GW), 1) // N
    head_sel = [lane_head == h for h in range(RWKV_LANE_GROUP)]

    def bdiag(x):
        zero = jnp.zeros_like(x)
        return jnp.concatenate([jnp.where(sel, x, zero) for sel in head_sel], axis=0)

    trow = lax.broadcasted_iota(jnp.int32, (C, GW), 0)
    tcol = lax.broadcasted_iota(jnp.int32, (C, GW), 1) % C
    strict = tcol < trow
    incl = tcol <= trow
    eye = (tcol == trow).astype(F32)
    vrow = lax.broadcasted_iota(jnp.int32, (GW, GW), 0) // N
    vcol = lax.broadcasted_iota(jnp.int32, (GW, GW), 1) // N
    same_head = vrow == vcol
    contract0 = (((0,), (0,)), ((), ()))
    contract1 = (((1,), (1,)), ((), ()))

    items = []
    for c in range(G):
        rows = slice(c * C, (c + 1) * C)
        lw = lw_all[rows]
        cum = _cumsum_rows(ltri_ref[...], lw)
        ge = jnp.exp(cum)
        gi = jnp.exp(-cum)
        At_all = (-kk[rows] * jnp.exp(cum - lw)).astype(BF16)
        Bt_all = (kka[rows] * gi).astype(BF16)
        Kt_all = (k2[rows] * gi).astype(BF16)
        Rt_all = (r[rows] * ge).astype(BF16)
        V_all = v[rows].astype(BF16)
        for j in range(n_groups):
            sl = slice(j * GW, (j + 1) * GW)
            items.append(dict(At=At_all[:, sl], Bt=Bt_all[:, sl], Kt=Kt_all[:, sl],
                              Rt=Rt_all[:, sl], V=V_all[:, sl], g_row=ge[C - 1:C, sl]))

    for it in items:
        AR = jnp.concatenate([it["At"], it["Rt"]], axis=0)
        BK = jnp.concatenate([bdiag(it["Bt"]), bdiag(it["Kt"])], axis=0)
        sc = lax.dot_general(AR, BK, contract1, preferred_element_type=F32)
        it["L"] = jnp.where(strict, sc[:C, :GW], 0.0)
        it["akm"] = jnp.where(strict, sc[:C, GW:], 0.0).astype(BF16)
        it["rbm"] = jnp.where(incl, sc[C:, :GW], 0.0).astype(BF16)
        it["rkm"] = jnp.where(incl, sc[C:, GW:], 0.0).astype(BF16)

    blk = 2
    for it in items:
        it["T"] = eye + jnp.where(trow // blk == tcol // blk, it["L"], 0.0)
    while blk < C:
        lvl = (trow // (2 * blk) == tcol // (2 * blk)) & (trow // blk != tcol // blk)
        for it in items:
            it["Tb"] = it["T"].astype(BF16)
            it["P"] = _mm(it["Tb"], bdiag(jnp.where(lvl, it["L"], 0.0).astype(BF16))).astype(BF16)
        for it in items:
            it["T"] = it["T"] + _mm(it["P"], bdiag(it["Tb"]))
        blk *= 2

    for it in items:
        it["Vd"] = bdiag(it["V"])
        it["akv"] = _mm(it["akm"], it["Vd"]).astype(BF16)
    for it in items:
        au = _mm(it["T"].astype(BF16), jnp.concatenate([bdiag(it["At"]), bdiag(it["akv"])], axis=1))
        it["A2"] = au[:, :GW].astype(BF16)
        it["U0"] = au[:, GW:].astype(BF16)
    for it in items:
        ry = _mm(it["rbm"], jnp.concatenate([bdiag(it["A2"]), bdiag(it["U0"])], axis=1))
        it["R2"] = (it["Rt"].astype(F32) + ry[:, :GW]).astype(BF16)
        it["Y0"] = ry[:, GW:] + _mm(it["rkm"], it["Vd"])
        it["Mq"] = (jnp.where(same_head, lax.dot_general(it["A2"], it["Bt"], contract0,
                                                         preferred_element_type=F32), 0.0)
                    * it["g_row"]).astype(BF16)
        it["Nq"] = jnp.where(same_head,
                             lax.dot_general(jnp.concatenate([it["U0"], it["V"]], axis=0),
                                             jnp.concatenate([it["Bt"], it["Kt"]], axis=0),
                                             contract0, preferred_element_type=F32),
                             0.0) * it["g_row"]

    y_chunks = []
    for c in range(G):
        ys = []
        for j in range(n_groups):
            it = items[c * n_groups + j]
            S0 = s_ref[j]
            Sb = S0.astype(BF16)
            ys.append(it["Y0"] + lax.dot_general(it["R2"], Sb, contract1,
                                                 preferred_element_type=F32))
            s_ref[j] = S0 * it["g_row"] + _mm(Sb, it["Mq"]) + it["Nq"]
        y_chunks.append(jnp.concatenate(ys, axis=1))
    Y = jnp.concatenate(y_chunks, axis=0)

    inv_n = 1.0 / N
    mean = head_sums(Y) * inv_n
    yc = Y - mean
    var = head_sums(yc * yc) * inv_n
    yn = yc * lax.rsqrt(var + RWKV_LN_EPS) * lng_ref[...] + lnb_ref[...]
    y_ref[0] = ((yn + bonus) * gate).astype(y_ref.dtype)


def _rwkv(p3, mu, w0, wup_pad, a0, aup_pad, gup, k_k, k_a, r_k, ln_g, ln_b, l):
    B, S, _ = p3.shape
    C = RWKV_CHUNK
    ts = min(RWKV_GROUP * C, S)
    group_w = RWKV_LANE_GROUP * RWKV_HEAD_DIM
    ltri = (jnp.arange(C)[None, :] <= jnp.arange(C)[:, None]).astype(BF16)
    head_of_lane = jnp.arange(group_w) // RWKV_HEAD_DIM
    bd = (head_of_lane[:, None] == head_of_lane[None, :]).astype(BF16)
    lay = lambda arr: _layer(arr, l)
    return pl.pallas_call(
        _rwkv_kernel,
        grid=(B, S // ts),
        in_specs=[pl.BlockSpec((1, ts, RWKV_IN), lambda b, s: (b, s, 0)), lay(mu), lay(w0),
                  lay(wup_pad), lay(a0), lay(aup_pad), lay(gup), lay(k_k), lay(k_a), lay(r_k),
                  lay(ln_g), lay(ln_b), _full((C, C)), _full((group_w, group_w))],
        out_specs=pl.BlockSpec((1, ts, RWKV_W), lambda b, s: (b, s, 0)),
        out_shape=jax.ShapeDtypeStruct((B, S, RWKV_W), BF16),
        scratch_shapes=[pltpu.VMEM((SUBLANES, RWKV_IN), F32),
                        pltpu.VMEM((RWKV_W // group_w, group_w, group_w), F32)],
        compiler_params=_params("parallel", "arbitrary"),
        name="rwkv",
    )(p3, mu, w0, wup_pad, a0, aup_pad, gup, k_k, k_a, r_k, ln_g, ln_b, ltri, bd)


def _shift_rows(x, d, fill, rows):
    return jnp.where(rows < d, fill, pltpu.roll(x, d, 0))


def _lru_kernel(p_ref, cw_ref, cb_ref, wa_ref, ba_ref, wx_ref, bx_ref, lam_ref, y_ref,
                xpad, hcarry):
    @pl.when(pl.program_id(1) == 0)
    def _():
        xpad[0:SUBLANES, :] = jnp.zeros((SUBLANES, LRU_W), F32)
        hcarry[...] = jnp.zeros_like(hcarry)

    p = p_ref[0]
    ts = p.shape[0]
    xb, gb = p[:, :LRU_W], p[:, LRU_W:]
    xpad[SUBLANES:, :] = xb
    cw = cw_ref[...]
    xc = cb_ref[...] + xb * cw[CONV_WIDTH - 1:CONV_WIDTH, :]
    for j in range(CONV_WIDTH - 1):
        lo = SUBLANES - (CONV_WIDTH - 1) + j
        xc = xc + xpad[lo:lo + ts, :] * cw[j:j + 1, :]
    xpad[0:SUBLANES, :] = xb[ts - SUBLANES:, :]
    rg = _sigmoid(_dot(xc, wa_ref[...]) + ba_ref[...])
    ig = _sigmoid(_dot(xc, wx_ref[...]) + bx_ref[...])
    lam = lam_ref[...]
    softplus_neg_lam = jnp.maximum(-lam, 0.0) + jnp.log(1.0 + jnp.exp(-jnp.abs(lam)))
    log_a = -LRU_C * rg * softplus_neg_lam
    a = jnp.exp(log_a)
    u = jnp.sqrt(1.0 - jnp.exp(2.0 * log_a)) * (ig * xc)
    rows = lax.broadcasted_iota(jnp.int32, a.shape, 0)
    d = 1
    while d < ts:
        u = u + a * _shift_rows(u, d, 0.0, rows)
        a = a * _shift_rows(a, d, 1.0, rows)
        d *= 2
    h = u + a * hcarry[SUBLANES - 1:SUBLANES, :]
    hcarry[...] = h[ts - SUBLANES:, :]
    gelu = 0.5 * gb * (1.0 + jnp.tanh(math.sqrt(2.0 / math.pi) * (gb + 0.044715 * gb * gb * gb)))
    y_ref[0] = (h * gelu).astype(y_ref.dtype)


def _lru(p3, conv_w, conv_b, wa_bd, ba, wx_bd, bx, lam, l):
    B, S, _ = p3.shape
    ts = min(TOK_TILE, S)
    lay = lambda arr: _layer(arr, l)
    return pl.pallas_call(
        _lru_kernel,
        grid=(B, S // ts),
        in_specs=[pl.BlockSpec((1, ts, 2 * LRU_W), lambda b, s: (b, s, 0)),
                  lay(conv_w), lay(conv_b), lay(wa_bd), lay(ba), lay(wx_bd), lay(bx), lay(lam)],
        out_specs=pl.BlockSpec((1, ts, LRU_W), lambda b, s: (b, s, 0)),
        out_shape=jax.ShapeDtypeStruct((B, S, LRU_W), BF16),
        scratch_shapes=[pltpu.VMEM((ts + SUBLANES, LRU_W), F32), pltpu.VMEM((SUBLANES, LRU_W), F32)],
        compiler_params=_params("parallel", "arbitrary"),
        name="rglru",
    )(p3, conv_w, conv_b, wa_bd, ba, wx_bd, bx, lam)


def _mla_prep_kernel(p_ref, cos_ref, sin_ref, qn_ref, wuq_ref, kvn_ref, wuk_ref, wuv_ref,
                     qg_ref, qgs_ref, kg_ref, kgs_ref, q_o, k_o, vt_o):
    W = MLA_HEADS * MLA_HEAD_PAD
    p = p_ref[...]
    cq = p[:, :Q_RANK]
    ckv = p[:, Q_RANK:Q_RANK + KV_RANK]
    kr = p[:, Q_RANK + KV_RANK:Q_RANK + KV_RANK + LANES]
    kr_sw = p[:, Q_RANK + KV_RANK + LANES:]
    cosf = cos_ref[...]
    sinf = sin_ref[...]
    scale = MLA_QK ** -0.5 * math.log2(math.e)
    cq_tab = cosf * (qg_ref[...] * scale)
    sq_tab = sinf * (qgs_ref[...] * scale)
    ck_tab = cosf * kg_ref[...]
    sk_tab = sinf * kgs_ref[...]

    qq = _dot(_rms(cq, qn_ref[...]), wuq_ref[...])
    ckv_n = _rms(ckv, kvn_ref[...]).astype(BF16)
    kn = _mm(ckv_n, wuk_ref[...])
    vt_o[0] = _dot_nt(wuv_ref[...], ckv_n).astype(BF16)
    kr_rot = kr * ck_tab + kr_sw * sk_tab
    kr_ss = jnp.sum(kr * kr, axis=-1, keepdims=True)
    for h in range(MLA_HEADS):
        sl = slice(h * MLA_HEAD_PAD, (h + 1) * MLA_HEAD_PAD)
        qh = qq[:, sl]
        q_rs = lax.rsqrt(jnp.sum(qh * qh, axis=-1, keepdims=True) / MLA_QK + 1e-6)
        q_o[:, sl] = ((qh * cq_tab + qq[:, W + h * MLA_HEAD_PAD:W + (h + 1) * MLA_HEAD_PAD] * sq_tab)
                      * q_rs).astype(BF16)
        kh = kn[:, sl]
        k_rs = lax.rsqrt((jnp.sum(kh * kh, axis=-1, keepdims=True) + kr_ss) / MLA_QK + 1e-6)
        k_o[:, sl] = ((kh * ck_tab + kr_rot) * k_rs).astype(BF16)


def _mla_prep(p_mla, cosf, sinf, q_norm, wuq2, kv_norm, wuk, wuvt, qg, qgs, kg, kgs, B, S, l):
    T = p_mla.shape[0]
    tm = min(TOK_TILE, S)
    per_row = S // tm
    W = MLA_HEADS * MLA_HEAD_PAD
    lay = lambda arr: _layer(arr, l)
    return pl.pallas_call(
        _mla_prep_kernel,
        grid=(T // tm,),
        in_specs=[_rows(tm, MLA_PAD), _rows(tm, LANES), _rows(tm, LANES), lay(q_norm), lay(wuq2),
                  lay(kv_norm), lay(wuk), lay(wuvt), lay(qg), lay(qgs), lay(kg), lay(kgs)],
        out_specs=[_rows(tm, W), _rows(tm, W),
                   pl.BlockSpec((1, MLA_HEADS * MLA_V, tm), lambda i: (i // per_row, 0, i % per_row))],
        out_shape=[jax.ShapeDtypeStruct((T, W), BF16), jax.ShapeDtypeStruct((T, W), BF16),
                   jax.ShapeDtypeStruct((B, MLA_HEADS * MLA_V, S), BF16)],
        compiler_params=_params("parallel"),
        name="mla_prep",
    )(p_mla, cosf, sinf, q_norm, wuq2, kv_norm, wuk, wuvt, qg, qgs, kg, kgs)


def _mla_attn_kernel(q_ref, k_ref, vt_ref, o_ref, acc_ref, m_ref, l_ref, s_ref, p_ref, a_ref):
    qi = pl.program_id(1)
    tq = q_ref.shape[1]
    m_ref[...] = jnp.full(m_ref.shape, MASKED_SCORE, F32)
    l_ref[...] = jnp.zeros(l_ref.shape, F32)
    acc_ref[...] = jnp.zeros(acc_ref.shape, F32)
    ones_rows = jnp.ones((BF16_SUBLANES, tq), BF16)

    def tile(j, masked, slot):
        start = pl.multiple_of(j * tq, tq)
        for h in range(MLA_HEADS):
            hs = slice(h * MLA_HEAD_PAD, (h + 1) * MLA_HEAD_PAD)
            kb = k_ref[0, pl.ds(start, tq), hs]
            st = lax.dot_general(kb, q_ref[0, :, hs], (((1,), (1,)), ((), ())),
                                 preferred_element_type=F32)
            if masked:
                kpos = lax.broadcasted_iota(jnp.int32, st.shape, 0)
                qpos = lax.broadcasted_iota(jnp.int32, st.shape, 1)
                st = jnp.where(kpos <= qpos, st, MASKED_SCORE)
            s_ref[slot, h] = st.astype(BF16)
        for h in range(MLA_HEADS):
            sb = s_ref[slot, h]
            m_old = m_ref[h:h + 1, :]
            m_new = jnp.maximum(m_old, jnp.max(sb, axis=0, keepdims=True).astype(F32))
            p_ref[slot, h] = jnp.exp2(sb - m_new.astype(BF16))
            a_ref[slot, h:h + 1, :] = jnp.exp2(m_old - m_new)
            m_ref[h:h + 1, :] = m_new
        for h in range(MLA_HEADS):
            vs = slice(h * MLA_V, (h + 1) * MLA_V)
            pr = p_ref[slot, h]
            pv = _mm(vt_ref[0, vs, pl.ds(start, tq)], pr)
            psum = _mm(ones_rows, pr)[0:1, :]
            alpha = a_ref[slot, h:h + 1, :]
            l_ref[h:h + 1, :] = alpha * l_ref[h:h + 1, :] + psum
            acc_ref[vs, :] = alpha * acc_ref[vs, :] + pv

    def body(i, carry):
        tile(2 * i, False, 0)
        tile(2 * i + 1, False, 1)
        return carry

    lax.fori_loop(0, qi // 2, body, 0)

    @pl.when(qi % 2 == 1)
    def _():
        tile(qi - 1, False, 0)

    tile(qi, True, 1)
    for h in range(MLA_HEADS):
        vs = slice(h * MLA_V, (h + 1) * MLA_V)
        acc_ref[vs, :] = acc_ref[vs, :] / l_ref[h:h + 1, :]
    o_ref[0] = acc_ref[...].T.astype(o_ref.dtype)


def _mla_attn(q, k, vt):
    B, S, W = q.shape
    tq = min(ATT_TILE, S)
    WV = MLA_HEADS * MLA_V
    return pl.pallas_call(
        _mla_attn_kernel,
        grid=(B, S // tq),
        in_specs=[pl.BlockSpec((1, tq, W), lambda b, i: (b, i, 0)),
                  pl.BlockSpec((1, S, W), lambda b, i: (b, 0, 0)),
                  pl.BlockSpec((1, WV, S), lambda b, i: (b, 0, 0))],
        out_specs=pl.BlockSpec((1, tq, WV), lambda b, i: (b, i, 0)),
        out_shape=jax.ShapeDtypeStruct((B, S, WV), BF16),
        scratch_shapes=[pltpu.VMEM((WV, tq), F32), pltpu.VMEM((MLA_HEADS, tq), F32),
                        pltpu.VMEM((MLA_HEADS, tq), F32), pltpu.VMEM((2, MLA_HEADS, tq, tq), BF16),
                        pltpu.VMEM((2, MLA_HEADS, tq, tq), BF16),
                        pltpu.VMEM((2, MLA_HEADS, tq), F32)],
        compiler_params=_params("parallel", "arbitrary"),
        name="mla_attn",
    )(q, k, vt)


def _merge_kernel(x_ref, ya_ref, yb_ref, yc_ref, g_ref, wg_ref, bg_ref, wb_ref, wo_ref, o_ref):
    x = x_ref[...]
    h = _rms(x, g_ref[...]).astype(BF16)
    merged = None
    for n, y_ref in enumerate((ya_ref, yb_ref, yc_ref)):
        sl = slice(n * D_MODEL, (n + 1) * D_MODEL)
        gate = _sigmoid(_mm(h, wg_ref[:, sl]) + bg_ref[:, sl])
        term = gate * _mm(y_ref[...], wb_ref[n])
        merged = term if merged is None else merged + term
    o_ref[...] = x + _dot(merged, wo_ref[...])


def _merge(x2, ya, yb, yc, g, w_gate, b_gate, w_branch, w_out, l):
    T = x2.shape[0]
    tm = min(TOK_TILE, T)
    lay = lambda arr: _layer(arr, l)
    return pl.pallas_call(
        _merge_kernel,
        grid=(T // tm,),
        in_specs=[_rows(tm, D_MODEL), _rows(tm, BRANCH_W), _rows(tm, BRANCH_W), _rows(tm, BRANCH_W),
                  lay(g), lay(w_gate), lay(b_gate), lay(w_branch), lay(w_out)],
        out_specs=_rows(tm, D_MODEL),
        out_shape=jax.ShapeDtypeStruct((T, D_MODEL), F32),
        compiler_params=_params("parallel"),
        name="merge",
    )(x2, ya, yb, yc, g, w_gate, b_gate, w_branch, w_out)


def _mem_kv_kernel(m_ref, g_ref, wk_ref, wv_ref, kg_ref, k_o, v_o):
    h = _rms(m_ref[...], g_ref[...]).astype(BF16)
    k = _mm(h, wk_ref[...])
    v_o[...] = _mm(h, wv_ref[...]).astype(BF16)
    for hd in range(XA_HEADS):
        sl = slice(hd * XA_HEAD_DIM, (hd + 1) * XA_HEAD_DIM)
        k_o[:, sl] = _rms(k[:, sl], kg_ref[...]).astype(BF16)


def _mem_kv(mem2, g, wk, wv, k_gain, l):
    M = mem2.shape[0]
    tm = min(TOK_TILE, M)
    lay = lambda arr: _layer(arr, l)
    return pl.pallas_call(
        _mem_kv_kernel,
        grid=(M // tm,),
        in_specs=[_rows(tm, D_MODEL), lay(g), lay(wk), lay(wv), lay(k_gain)],
        out_specs=[_rows(tm, XA_W), _rows(tm, XA_W)],
        out_shape=[jax.ShapeDtypeStruct((M, XA_W), BF16)] * 2,
        compiler_params=_params("parallel"),
        name="mem_kv",
    )(mem2, g, wk, wv, k_gain)


def _xattn_kernel(x_ref, k_ref, v_ref, g_ref, wq_ref, qg_ref, wo_ref, o_ref):
    x = x_ref[0]
    h = _rms(x, g_ref[...]).astype(BF16)
    q = _mm(h, wq_ref[...])
    scale = XA_HEAD_DIM ** -0.5
    outs = []
    for hd in range(XA_HEADS):
        sl = slice(hd * XA_HEAD_DIM, (hd + 1) * XA_HEAD_DIM)
        qh = _rms(q[:, sl], qg_ref[...]) * scale
        s = _dot_nt(qh, k_ref[0, :, sl])
        pr = jnp.exp(s - jnp.max(s, axis=-1, keepdims=True))
        pr = pr / jnp.sum(pr, axis=-1, keepdims=True)
        outs.append(_dot(pr, v_ref[0, :, sl]))
    o = jnp.concatenate(outs, axis=-1)
    o_ref[0] = x + _dot(o, wo_ref[...])


def _xattn(x3, k3, v3, g, wq, q_gain, wo, l):
    B, S, _ = x3.shape
    ts = min(TOK_TILE, S)
    M = k3.shape[1]
    lay = lambda arr: _layer(arr, l)
    return pl.pallas_call(
        _xattn_kernel,
        grid=(B, S // ts),
        in_specs=[pl.BlockSpec((1, ts, D_MODEL), lambda b, s: (b, s, 0)),
                  pl.BlockSpec((1, M, XA_W), lambda b, s: (b, 0, 0)),
                  pl.BlockSpec((1, M, XA_W), lambda b, s: (b, 0, 0)),
                  lay(g), lay(wq), lay(q_gain), lay(wo)],
        out_specs=pl.BlockSpec((1, ts, D_MODEL), lambda b, s: (b, s, 0)),
        out_shape=jax.ShapeDtypeStruct((B, S, D_MODEL), F32),
        compiler_params=_params("parallel", "parallel"),
        name="xattn",
    )(x3, k3, v3, g, wq, q_gain, wo)


FF_SPLIT = 2


def _ffn_kernel(x_ref, g_ref, w1_ref, w3_ref, w2_ref, o_ref):
    x = x_ref[...]
    h = _rms(x, g_ref[...]).astype(BF16)
    step = D_FF // FF_SPLIT
    acc = x
    for c in range(FF_SPLIT):
        sl = slice(c * step, (c + 1) * step)
        a = _mm(h, w1_ref[:, sl])
        b = _mm(h, w3_ref[:, sl])
        z = a * _sigmoid(a) * b
        acc = acc + _dot(z, w2_ref[sl, :])
    o_ref[...] = acc


def _ffn(x2, g, w1, w3, w2, l):
    T = x2.shape[0]
    tm = min(TOK_TILE, T)
    lay = lambda arr: _layer(arr, l)
    return pl.pallas_call(
        _ffn_kernel,
        grid=(T // tm,),
        in_specs=[_rows(tm, D_MODEL), lay(g), lay(w1), lay(w3), lay(w2)],
        out_specs=_rows(tm, D_MODEL),
        out_shape=jax.ShapeDtypeStruct((T, D_MODEL), F32),
        compiler_params=_params("parallel"),
        name="ffn",
    )(x2, g, w1, w3, w2)


def _block_diag(w):
    L, n, i, j = w.shape
    eye = jnp.eye(n, dtype=w.dtype)
    return jnp.einsum("lnij,nm->lnimj", w, eye).reshape(L, n * i, n * j)


def _pad_heads(w, heads, width, pad_to):
    lead = w.shape[:-1]
    w = w.reshape(lead + (heads, width))
    w = jnp.pad(w, [(0, 0)] * len(lead) + [(0, 0), (0, pad_to - width)])
    return w.reshape(lead + (heads * pad_to,))


def _rope_partner(w):
    half = MLA_ROPE // 2
    return jnp.concatenate([jnp.zeros_like(w[..., :MLA_NOPE]), w[..., MLA_NOPE + half:],
                            w[..., MLA_NOPE:MLA_NOPE + half]], axis=-1)


def _vec(v):
    return v.reshape(v.shape[0], 1, -1).astype(F32)


def kernel(x, mem, positions, norm_mix, norm_xattn, norm_mem, norm_ffn, w_in, b_gate, rwkv_mu, rwkv_w0, rwkv_w_up, rwkv_a0, rwkv_a_up, rwkv_g_up, rwkv_k_k, rwkv_k_a, rwkv_r_k, rwkv_ln_g, rwkv_ln_b, lru_conv_w, lru_conv_b, lru_wa, lru_ba, lru_wx, lru_bx, lru_lambda, mla_q_norm, mla_w_uq, mla_kv_norm, mla_w_ukv, mla_q_gain, mla_k_gain, w_branch, w_out, xa_w_q, xa_w_kv, xa_q_gain, xa_k_gain, xa_w_o, ffn_w1, ffn_w3, ffn_w2):
    B, S, D = x.shape
    T = B * S
    depth = w_in.shape[0]
    x2 = x.reshape(T, D)
    mem2 = mem.reshape(B * N_MEM, D)
    half = MLA_ROPE // 2

    inv_freq = ROPE_THETA ** (-jnp.arange(0, MLA_ROPE, 2, dtype=F32) / MLA_ROPE)
    per_row = LANES // half
    pos_rep = jnp.repeat(positions.reshape(T // per_row, per_row), half, axis=1)
    cos_c, sin_c, nsin_c = _rope_tables(pos_rep, jnp.tile(inv_freq, per_row).reshape(1, LANES))
    cos_c, sin_c, nsin_c = (t.reshape(T, half) for t in (cos_c, sin_c, nsin_c))
    ones = jnp.ones((T, MLA_NOPE), F32)
    zeros = jnp.zeros((T, MLA_NOPE), F32)
    cosf = jnp.concatenate([ones, cos_c, cos_c, ones[:, :LANES - MLA_QK]], axis=1)
    sinf = jnp.concatenate([zeros, nsin_c, sin_c, zeros[:, :LANES - MLA_QK]], axis=1)

    w_in_b = w_in.astype(BF16)
    w_kr = w_in_b[:, :, MLA_OFF + Q_RANK + KV_RANK:GATE_OFF]
    zero_cols = lambda n: jnp.zeros((depth, D, n), BF16)
    w_mla = jnp.concatenate(
        [w_in_b[:, :, MLA_OFF:MLA_OFF + Q_RANK + KV_RANK], zero_cols(MLA_NOPE), w_kr,
         zero_cols(LANES - MLA_QK), zero_cols(MLA_NOPE), w_kr[:, :, half:], w_kr[:, :, :half],
         zero_cols(LANES - MLA_QK)], axis=2)
    w_gate = w_in_b[:, :, GATE_OFF:]

    zeros_lora = jnp.zeros((depth, W_LORA, RWKV_W), F32)
    wup_pad = jnp.concatenate([rwkv_w_up, zeros_lora], axis=1).astype(BF16)
    aup_pad = jnp.concatenate([zeros_lora, rwkv_a_up], axis=1).astype(BF16)
    gup = rwkv_g_up.astype(BF16)
    r_k = rwkv_r_k.reshape(depth, RWKV_W)

    wa_bd = _block_diag(lru_wa).astype(BF16)
    wx_bd = _block_diag(lru_wx).astype(BF16)

    wuq_h = mla_w_uq.reshape(depth, Q_RANK, MLA_HEADS, MLA_QK)
    wuq2 = jnp.concatenate(
        [_pad_heads(mla_w_uq, MLA_HEADS, MLA_QK, MLA_HEAD_PAD),
         _pad_heads(_rope_partner(wuq_h).reshape(depth, Q_RANK, -1), MLA_HEADS, MLA_QK, MLA_HEAD_PAD)],
        axis=2).astype(BF16)
    wukv = mla_w_ukv.reshape(depth, KV_RANK, MLA_HEADS, MLA_NOPE + MLA_V)
    wuk = _pad_heads(wukv[..., :MLA_NOPE].reshape(depth, KV_RANK, -1), MLA_HEADS, MLA_NOPE,
                     MLA_HEAD_PAD).astype(BF16)
    wuvt = jnp.swapaxes(wukv[..., MLA_NOPE:].reshape(depth, KV_RANK, -1), 1, 2).astype(BF16)
    pad_gain = lambda gv: _vec(jnp.pad(gv, ((0, 0), (0, LANES - MLA_QK))))
    qg, qgs = pad_gain(mla_q_gain), pad_gain(_rope_partner(mla_q_gain))
    kg, kgs = pad_gain(mla_k_gain), pad_gain(_rope_partner(mla_k_gain))

    w_branch_b = w_branch.astype(BF16)
    w_out_b = w_out.astype(BF16)
    wkv = xa_w_kv.reshape(depth, D, XA_HEADS, 2, XA_HEAD_DIM)
    xa_wk = wkv[:, :, :, 0].reshape(depth, D, XA_W).astype(BF16)
    xa_wv = wkv[:, :, :, 1].reshape(depth, D, XA_W).astype(BF16)
    xa_wq = xa_w_q.astype(BF16)
    xa_wo = xa_w_o.astype(BF16)
    w1, w3, w2 = ffn_w1.astype(BF16), ffn_w3.astype(BF16), ffn_w2.astype(BF16)

    n_mix, n_xa, n_mem, n_ffn = _vec(norm_mix), _vec(norm_xattn), _vec(norm_mem), _vec(norm_ffn)
    rw = [_vec(t) for t in (rwkv_mu, rwkv_w0, rwkv_a0, rwkv_k_k, rwkv_k_a, r_k, rwkv_ln_g, rwkv_ln_b)]
    mu, w0, a0, k_k, k_a, r_kv, ln_g, ln_b = rw
    conv_b, ba, bx, lam = _vec(lru_conv_b), _vec(lru_ba), _vec(lru_bx), _vec(lru_lambda)
    q_norm, kv_norm = _vec(mla_q_norm), _vec(mla_kv_norm)
    b_gate_v, xa_qg, xa_kg = _vec(b_gate), _vec(xa_q_gain), _vec(xa_k_gain)

    W = MLA_HEADS * MLA_HEAD_PAD
    for l in range(depth):
        p_rwkv, p_lru, p_mla = _in_proj(x2, n_mix, w_in_b, w_mla, l)

        y_a = _rwkv(p_rwkv.reshape(B, S, RWKV_IN), mu, w0, wup_pad, a0, aup_pad, gup, k_k, k_a,
                    r_kv, ln_g, ln_b, l).reshape(T, RWKV_W)
        y_b = _lru(p_lru.reshape(B, S, 2 * LRU_W), lru_conv_w, conv_b, wa_bd, ba, wx_bd, bx, lam,
                   l).reshape(T, LRU_W)
        q, k, vt = _mla_prep(p_mla, cosf, sinf, q_norm, wuq2, kv_norm, wuk, wuvt, qg, qgs, kg, kgs,
                             B, S, l)
        y_c = _mla_attn(q.reshape(B, S, W), k.reshape(B, S, W), vt).reshape(T, MLA_HEADS * MLA_V)

        x2 = _merge(x2, y_a, y_b, y_c, n_mix, w_gate, b_gate_v, w_branch_b, w_out_b, l)

        mk, mv = _mem_kv(mem2, n_mem, xa_wk, xa_wv, xa_kg, l)
        x2 = _xattn(x2.reshape(B, S, D), mk.reshape(B, N_MEM, XA_W), mv.reshape(B, N_MEM, XA_W),
                    n_xa, xa_wq, xa_qg, xa_wo, l).reshape(T, D)

        x2 = _ffn(x2, n_ffn, w1, w3, w2, l)
    return x2.reshape(B, S, D)
```

```python
import functools
import math

import jax
import jax.numpy as jnp
from jax import lax
from jax.experimental import pallas as pl
from jax.experimental.pallas import tpu as pltpu

F32 = jnp.float32
BF16 = jnp.bfloat16

D_MODEL = 1024
N_MEM = 256
RWKV_HEADS = 8
RWKV_HEAD_DIM = 64
RWKV_W = RWKV_HEADS * RWKV_HEAD_DIM
W_LORA = 64
A_LORA = 64
G_LORA = 128
RWKV_IN = 3 * RWKV_W + W_LORA + A_LORA + G_LORA
RWKV_LN_EPS = RWKV_HEAD_DIM * 1e-5
LRU_BLOCKS = 8
LRU_W = 512
CONV_WIDTH = 4
LRU_C = 8.0
MLA_HEADS = 8
MLA_NOPE = 64
MLA_ROPE = 32
MLA_QK = MLA_NOPE + MLA_ROPE
MLA_V = 64
Q_RANK = 256
KV_RANK = 128
ROPE_THETA = 10000.0
N_BRANCH = 3
BRANCH_W = 512
XA_HEADS = 4
XA_HEAD_DIM = 128
XA_W = XA_HEADS * XA_HEAD_DIM
D_FF = -(-8 * D_MODEL // (3 * 256)) * 256
LRU_OFF = RWKV_IN
MLA_OFF = LRU_OFF + 2 * LRU_W
GATE_OFF = MLA_OFF + Q_RANK + KV_RANK + MLA_ROPE

LANES = 128
SUBLANES = 8
BF16_SUBLANES = 16
MXU_TILE = 256
VMEM_LIMIT = 56 * 1024 * 1024
MLA_HEAD_PAD = LANES
MLA_PAD = Q_RANK + KV_RANK + 2 * LANES
MASKED_SCORE = -2.0 ** 100
RWKV_CHUNK = 64
RWKV_GROUP = 8
RWKV_LANE_GROUP = MXU_TILE // RWKV_HEAD_DIM
LRU_SCAN_ROWS = 64
TOK_TILE = 512
ATT_TILE = 256


def _params(*sem):
    return pltpu.CompilerParams(dimension_semantics=sem, vmem_limit_bytes=VMEM_LIMIT)


def _mm(a, b):
    return jnp.dot(a, b, preferred_element_type=F32)


def _dot(a, b):
    return _mm(a.astype(BF16), b.astype(BF16))


def _dot_nt(a, b):
    return lax.dot_general(a.astype(BF16), b.astype(BF16), (((1,), (1,)), ((), ())),
                           preferred_element_type=F32)


def _sigmoid(x):
    return 1.0 / (1.0 + jnp.exp(-x))


def _rms(x, g, eps=1e-6):
    return x * lax.rsqrt(jnp.mean(x * x, axis=-1, keepdims=True) + eps) * g


def _full(shape):
    n = len(shape)
    return pl.BlockSpec(shape, lambda *_: (0,) * n)


def _layer(arr, l):
    tail = arr.shape[1:]
    return pl.BlockSpec((None,) + tail, lambda *_: (l,) + (0,) * len(tail))


def _rows(tm, n):
    return pl.BlockSpec((tm, n), lambda i: (i, 0))


def _rope_kernel(pos_ref, freq_ref, cos_o, sin_o, nsin_o):
    ang = pos_ref[...].astype(F32) * freq_ref[...]
    s = jnp.sin(ang)
    cos_o[...] = jnp.cos(ang)
    sin_o[...] = s
    nsin_o[...] = -s


def _rope_tables(pos_rep, freq_tile):
    R = pos_rep.shape[0]
    tm = min(TOK_TILE, R)
    out = jax.ShapeDtypeStruct((R, LANES), F32)
    return pl.pallas_call(
        _rope_kernel,
        grid=(R // tm,),
        in_specs=[_rows(tm, LANES), _full((1, LANES))],
        out_specs=[_rows(tm, LANES)] * 3,
        out_shape=[out] * 3,
        compiler_params=_params("parallel"),
        name="rope_tables",
    )(pos_rep, freq_tile)


def _in_proj_kernel(x_ref, g_ref, wrl_ref, wm_ref,
                    cw_ref, cb_ref, wa_ref, ba_ref, wx_ref, bx_ref, lam_ref,
                    cos_ref, sin_ref, qn_ref, wuq_ref, kvn_ref, wuk_ref, wuv_ref,
                    qg_ref, qgs_ref, kg_ref, kgs_ref,
                    or_ref, yb_ref, q_o, k_o, vt_o, xpad, hcarry, *, tiles_per_row):
    first = pl.program_id(0) % tiles_per_row == 0
    h = _rms(x_ref[...], g_ref[...]).astype(BF16)
    p_lru = _mm(h, wrl_ref[:, RWKV_IN:])

    def rwkv_cols(c0):
        def run():
            or_ref[:, c0:c0 + MXU_TILE] = _mm(h, wrl_ref[:, c0:c0 + MXU_TILE])
        return run

    yb_ref[...] = _lru_tile(p_lru, cw_ref, cb_ref, wa_ref, ba_ref, wx_ref, bx_ref, lam_ref,
                            xpad, hcarry, first,
                            background=[rwkv_cols(c0) for c0 in range(0, RWKV_IN, MXU_TILE)]
                            ).astype(yb_ref.dtype)
    p_mla = _mm(h, wm_ref[...])
    _mla_tile(p_mla, cos_ref, sin_ref, qn_ref, wuq_ref, kvn_ref, wuk_ref, wuv_ref,
              qg_ref, qgs_ref, kg_ref, kgs_ref, q_o, k_o, vt_o)


def _in_proj(x2, g, w_in_b, w_mla, lru_w, mla_w, cosf, sinf, B, S, l):
    T = x2.shape[0]
    tm = min(TOK_TILE, S)
    per_row = S // tm
    W = MLA_HEADS * MLA_HEAD_PAD
    lay = lambda arr: _layer(arr, l)
    return pl.pallas_call(
        functools.partial(_in_proj_kernel, tiles_per_row=per_row),
        grid=(T // tm,),
        in_specs=[_rows(tm, D_MODEL), lay(g), lay(w_in_b), lay(w_mla)] + [lay(w) for w in lru_w]
                 + [_rows(tm, LANES), _rows(tm, LANES)] + [lay(w) for w in mla_w],
        out_specs=[_rows(tm, RWKV_IN), _rows(tm, LRU_W), _rows(tm, W), _rows(tm, W),
                   pl.BlockSpec((1, MLA_HEADS * MLA_V, tm), lambda i: (i // per_row, 0, i % per_row))],
        out_shape=[jax.ShapeDtypeStruct((T, RWKV_IN), F32),
                   jax.ShapeDtypeStruct((T, LRU_W), BF16),
                   jax.ShapeDtypeStruct((T, W), BF16), jax.ShapeDtypeStruct((T, W), BF16),
                   jax.ShapeDtypeStruct((B, MLA_HEADS * MLA_V, S), BF16)],
        scratch_shapes=[pltpu.VMEM((tm + SUBLANES, LRU_W), F32), pltpu.VMEM((SUBLANES, LRU_W), F32)],
        compiler_params=_params("arbitrary"),
        name="in_proj",
    )(x2, g, w_in_b, w_mla, *lru_w, cosf, sinf, *mla_w)


def _cumsum_rows(ltri, x):
    hi = x.astype(BF16)
    r1 = x - hi.astype(F32)
    mid = r1.astype(BF16)
    lo = (r1 - mid.astype(F32)).astype(BF16)
    return _mm(ltri, hi) + _mm(ltri, mid) + _mm(ltri, lo)


def _rwkv_kernel(p_ref, mu_ref, w0_ref, wup_ref, a0_ref, aup_ref, gup_ref, kkw_ref, ka_ref,
                 rk_ref, lng_ref, lnb_ref, ltri_ref, bd_ref, y_ref, carry, s_ref):
    @pl.when(pl.program_id(1) == 0)
    def _():
        carry[...] = jnp.zeros_like(carry)
        s_ref[...] = jnp.zeros_like(s_ref)

    C, N = RWKV_CHUNK, RWKV_HEAD_DIM
    GW = RWKV_LANE_GROUP * N
    n_groups = RWKV_W // GW
    bd = bd_ref[...]

    def head_sums(x):
        return jnp.concatenate([_mm(x[:, j * GW:(j + 1) * GW].astype(BF16), bd)
                                for j in range(n_groups)], axis=1)

    p = p_ref[0]
    ts = p.shape[0]
    G = ts // C
    prow = lax.broadcasted_iota(jnp.int32, p.shape, 0)
    prev = jnp.where(prow == 0, carry[SUBLANES - 1:SUBLANES, :], pltpu.roll(p, 1, 0))
    carry[...] = p[ts - SUBLANES:, :]
    pm = p + (prev - p) * mu_ref[...]
    o1, o2, o3 = RWKV_W, 2 * RWKV_W, 3 * RWKV_W
    r, k, v = pm[:, :o1], pm[:, o1:o2], pm[:, o2:o3]
    wa = pm[:, o3:o3 + W_LORA + A_LORA]
    gd = pm[:, o3 + W_LORA + A_LORA:]
    z = w0_ref[...] + _dot(jnp.tanh(wa), wup_ref[...])
    lw_all = -math.exp(-0.5) * _sigmoid(z)
    a = _sigmoid(a0_ref[...] + _dot(wa, aup_ref[...]))
    gate = _dot(_sigmoid(gd), gup_ref[...])
    kk = k * kkw_ref[...]
    kk = kk / jnp.maximum(jnp.sqrt(head_sums(kk * kk)), 1e-12)
    k2 = k * (1.0 + (a - 1.0) * ka_ref[...])
    bonus = head_sums(r * k2 * rk_ref[...]) * v
    kka = kk * a

    lane_head = lax.broadcasted_iota(jnp.int32, (C, GW), 1) // N
    head_sel = [lane_head == h for h in range(RWKV_LANE_GROUP)]

    def bdiag(x):
        zero = jnp.zeros_like(x)
        return jnp.concatenate([jnp.where(sel, x, zero) for sel in head_sel], axis=0)

    trow = lax.broadcasted_iota(jnp.int32, (C, GW), 0)
    tcol = lax.broadcasted_iota(jnp.int32, (C, GW), 1) % C
    strict = tcol < trow
    incl = tcol <= trow
    eye = (tcol == trow).astype(F32)
    vrow = lax.broadcasted_iota(jnp.int32, (GW, GW), 0) // N
    vcol = lax.broadcasted_iota(jnp.int32, (GW, GW), 1) // N
    same_head = vrow == vcol
    contract0 = (((0,), (0,)), ((), ()))
    contract1 = (((1,), (1,)), ((), ()))

    items = []
    for c in range(G):
        rows = slice(c * C, (c + 1) * C)
        lw = lw_all[rows]
        cum = _cumsum_rows(ltri_ref[...], lw)
        ge = jnp.exp(cum)
        gi = jnp.exp(-cum)
        At_all = (-kk[rows] * jnp.exp(cum - lw)).astype(BF16)
        Bt_all = (kka[rows] * gi).astype(BF16)
        Kt_all = (k2[rows] * gi).astype(BF16)
        Rt_all = (r[rows] * ge).astype(BF16)
        V_all = v[rows].astype(BF16)
        for j in range(n_groups):
            sl = slice(j * GW, (j + 1) * GW)
            items.append(dict(At=At_all[:, sl], Bt=Bt_all[:, sl], Kt=Kt_all[:, sl],
                              Rt=Rt_all[:, sl], V=V_all[:, sl], g_row=ge[C - 1:C, sl]))

    for it in items:
        AR = jnp.concatenate([it["At"], it["Rt"]], axis=0)
        BK = jnp.concatenate([bdiag(it["Bt"]), bdiag(it["Kt"])], axis=0)
        sc = lax.dot_general(AR, BK, contract1, preferred_element_type=F32)
        it["L"] = jnp.where(strict, sc[:C, :GW], 0.0)
        it["akm"] = jnp.where(strict, sc[:C, GW:], 0.0).astype(BF16)
        it["rbm"] = jnp.where(incl, sc[C:, :GW], 0.0).astype(BF16)
        it["rkm"] = jnp.where(incl, sc[C:, GW:], 0.0).astype(BF16)

    blk = 2
    for it in items:
        it["T"] = eye + jnp.where(trow // blk == tcol // blk, it["L"], 0.0)
    while blk < C:
        lvl = (trow // (2 * blk) == tcol // (2 * blk)) & (trow // blk != tcol // blk)
        for it in items:
            it["Tb"] = it["T"].astype(BF16)
            it["P"] = _mm(it["Tb"], bdiag(jnp.where(lvl, it["L"], 0.0).astype(BF16))).astype(BF16)
        for it in items:
            it["T"] = it["T"] + _mm(it["P"], bdiag(it["Tb"]))
        blk *= 2

    for it in items:
        it["Vd"] = bdiag(it["V"])
        it["akv"] = _mm(it["akm"], it["Vd"]).astype(BF16)
    for it in items:
        au = _mm(it["T"].astype(BF16), jnp.concatenate([bdiag(it["At"]), bdiag(it["akv"])], axis=1))
        it["A2"] = au[:, :GW].astype(BF16)
        it["U0"] = au[:, GW:].astype(BF16)
    for it in items:
        ry = _mm(it["rbm"], jnp.concatenate([bdiag(it["A2"]), bdiag(it["U0"])], axis=1))
        it["R2"] = (it["Rt"].astype(F32) + ry[:, :GW]).astype(BF16)
        it["Y0"] = ry[:, GW:] + _mm(it["rkm"], it["Vd"])
        it["Mq"] = (jnp.where(same_head, lax.dot_general(it["A2"], it["Bt"], contract0,
                                                         preferred_element_type=F32), 0.0)
                    * it["g_row"]).astype(BF16)
        it["Nq"] = jnp.where(same_head,
                             lax.dot_general(jnp.concatenate([it["U0"], it["V"]], axis=0),
                                             jnp.concatenate([it["Bt"], it["Kt"]], axis=0),
                                             contract0, preferred_element_type=F32),
                             0.0) * it["g_row"]

    y_chunks = []
    for c in range(G):
        ys = []
        for j in range(n_groups):
            it = items[c * n_groups + j]
            S0 = s_ref[j]
            Sb = S0.astype(BF16)
            ys.append(it["Y0"] + lax.dot_general(it["R2"], Sb, contract1,
                                                 preferred_element_type=F32))
            s_ref[j] = S0 * it["g_row"] + _mm(Sb, it["Mq"]) + it["Nq"]
        y_chunks.append(jnp.concatenate(ys, axis=1))
    Y = jnp.concatenate(y_chunks, axis=0)

    inv_n = 1.0 / N
    mean = head_sums(Y) * inv_n
    yc = Y - mean
    var = head_sums(yc * yc) * inv_n
    yn = yc * lax.rsqrt(var + RWKV_LN_EPS) * lng_ref[...] + lnb_ref[...]
    y_ref[0] = ((yn + bonus) * gate).astype(y_ref.dtype)


def _rwkv(p3, mu, w0, wup_pad, a0, aup_pad, gup, k_k, k_a, r_k, ln_g, ln_b, l):
    B, S, _ = p3.shape
    C = RWKV_CHUNK
    ts = min(RWKV_GROUP * C, S)
    group_w = RWKV_LANE_GROUP * RWKV_HEAD_DIM
    ltri = (jnp.arange(C)[None, :] <= jnp.arange(C)[:, None]).astype(BF16)
    head_of_lane = jnp.arange(group_w) // RWKV_HEAD_DIM
    bd = (head_of_lane[:, None] == head_of_lane[None, :]).astype(BF16)
    lay = lambda arr: _layer(arr, l)
    return pl.pallas_call(
        _rwkv_kernel,
        grid=(B, S // ts),
        in_specs=[pl.BlockSpec((1, ts, RWKV_IN), lambda b, s: (b, s, 0)), lay(mu), lay(w0),
                  lay(wup_pad), lay(a0), lay(aup_pad), lay(gup), lay(k_k), lay(k_a), lay(r_k),
                  lay(ln_g), lay(ln_b), _full((C, C)), _full((group_w, group_w))],
        out_specs=pl.BlockSpec((1, ts, RWKV_W), lambda b, s: (b, s, 0)),
        out_shape=jax.ShapeDtypeStruct((B, S, RWKV_W), BF16),
        scratch_shapes=[pltpu.VMEM((SUBLANES, RWKV_IN), F32),
                        pltpu.VMEM((RWKV_W // group_w, group_w, group_w), F32)],
        compiler_params=_params("parallel", "arbitrary"),
        name="rwkv",
    )(p3, mu, w0, wup_pad, a0, aup_pad, gup, k_k, k_a, r_k, ln_g, ln_b, ltri, bd)


def _shift_rows(x, d, fill):
    n = x.shape[0]
    if d % SUBLANES == 0:
        return jnp.concatenate([jnp.full((d,) + x.shape[1:], fill, x.dtype), x[:n - d]], axis=0)
    rows = lax.broadcasted_iota(jnp.int32, x.shape, 0)
    return jnp.where(rows < d, fill, pltpu.roll(x, d, 0))


def _scan_block(a, u):
    d = 1
    while d < a.shape[0]:
        u = u + a * _shift_rows(u, d, 0.0)
        a = a * _shift_rows(a, d, 1.0)
        d *= 2
    return a, u


def _lru_tile(p, cw_ref, cb_ref, wa_ref, ba_ref, wx_ref, bx_ref, lam_ref, xpad, hcarry, first,
              background=()):
    @pl.when(first)
    def _():
        xpad[0:SUBLANES, :] = jnp.zeros((SUBLANES, LRU_W), F32)
        hcarry[...] = jnp.zeros_like(hcarry)

    ts = p.shape[0]
    xb, gb = p[:, :LRU_W], p[:, LRU_W:]
    xpad[SUBLANES:, :] = xb
    cw = cw_ref[...]
    xc = cb_ref[...] + xb * cw[CONV_WIDTH - 1:CONV_WIDTH, :]
    for j in range(CONV_WIDTH - 1):
        lo = SUBLANES - (CONV_WIDTH - 1) + j
        xc = xc + xpad[lo:lo + ts, :] * cw[j:j + 1, :]
    xpad[0:SUBLANES, :] = xb[ts - SUBLANES:, :]
    rg = _sigmoid(_dot(xc, wa_ref[...]) + ba_ref[...])
    ig = _sigmoid(_dot(xc, wx_ref[...]) + bx_ref[...])
    lam = lam_ref[...]
    softplus_neg_lam = jnp.maximum(-lam, 0.0) + jnp.log(1.0 + jnp.exp(-jnp.abs(lam)))
    log_a = -LRU_C * rg * softplus_neg_lam
    a = jnp.exp(log_a)
    u = jnp.sqrt(1.0 - a * a) * (ig * xc)
    pending = list(background)
    n_row_blocks = ts // LRU_SCAN_ROWS
    stride = max(1, (LRU_W // LANES) * n_row_blocks // max(1, len(pending)))
    strips = []
    for ls in range(LRU_W // LANES):
        cols = slice(ls * LANES, (ls + 1) * LANES)
        h_in = hcarry[SUBLANES - 1:SUBLANES, cols]
        blocks = []
        for rt in range(n_row_blocks):
            if pending and (ls * n_row_blocks + rt) % stride == 0:
                pending.pop(0)()
            rs = slice(rt * LRU_SCAN_ROWS, (rt + 1) * LRU_SCAN_ROWS)
            a_blk, u_blk = _scan_block(a[rs, cols], u[rs, cols])
            h_blk = u_blk + a_blk * h_in
            h_in = h_blk[LRU_SCAN_ROWS - 1:, :]
            blocks.append(h_blk)
        strips.append(jnp.concatenate(blocks, axis=0))
    for thunk in pending:
        thunk()
    h = jnp.concatenate(strips, axis=1)
    hcarry[...] = h[ts - SUBLANES:, :]
    gelu = 0.5 * gb * (1.0 + jnp.tanh(math.sqrt(2.0 / math.pi) * (gb + 0.044715 * gb * gb * gb)))
    return h * gelu


def _mla_tile(p, cos_ref, sin_ref, qn_ref, wuq_ref, kvn_ref, wuk_ref, wuv_ref,
              qg_ref, qgs_ref, kg_ref, kgs_ref, q_o, k_o, vt_o):
    W = MLA_HEADS * MLA_HEAD_PAD
    cq = p[:, :Q_RANK]
    ckv = p[:, Q_RANK:Q_RANK + KV_RANK]
    kr = p[:, Q_RANK + KV_RANK:Q_RANK + KV_RANK + LANES]
    kr_sw = p[:, Q_RANK + KV_RANK + LANES:]
    cosf = cos_ref[...]
    sinf = sin_ref[...]
    scale = MLA_QK ** -0.5 * math.log2(math.e)
    cq_tab = cosf * (qg_ref[...] * scale)
    sq_tab = sinf * (qgs_ref[...] * scale)
    ck_tab = cosf * kg_ref[...]
    sk_tab = sinf * kgs_ref[...]

    qq = _dot(_rms(cq, qn_ref[...]), wuq_ref[...])
    ckv_n = _rms(ckv, kvn_ref[...]).astype(BF16)
    kn = _mm(ckv_n, wuk_ref[...])
    vt_o[0] = _dot_nt(wuv_ref[...], ckv_n).astype(BF16)
    kr_rot = kr * ck_tab + kr_sw * sk_tab
    kr_ss = jnp.sum(kr * kr, axis=-1, keepdims=True)
    for h in range(MLA_HEADS):
        sl = slice(h * MLA_HEAD_PAD, (h + 1) * MLA_HEAD_PAD)
        qh = qq[:, sl]
        q_rs = lax.rsqrt(jnp.sum(qh * qh, axis=-1, keepdims=True) / MLA_QK + 1e-6)
        q_o[:, sl] = ((qh * cq_tab + qq[:, W + h * MLA_HEAD_PAD:W + (h + 1) * MLA_HEAD_PAD] * sq_tab)
                      * q_rs).astype(BF16)
        kh = kn[:, sl]
        k_rs = lax.rsqrt((jnp.sum(kh * kh, axis=-1, keepdims=True) + kr_ss) / MLA_QK + 1e-6)
        k_o[:, sl] = ((kh * ck_tab + kr_rot) * k_rs).astype(BF16)


def _mla_attn_kernel(q_ref, k_ref, vt_ref, o_ref, acc_ref, m_ref, l_ref, s_ref, p_ref, a_ref):
    qi = pl.program_id(1)
    tq = q_ref.shape[1]
    m_ref[...] = jnp.full(m_ref.shape, MASKED_SCORE, F32)
    l_ref[...] = jnp.zeros(l_ref.shape, F32)
    acc_ref[...] = jnp.zeros(acc_ref.shape, F32)
    ones_rows = jnp.ones((BF16_SUBLANES, tq), BF16)

    def tile(j, masked, slot):
        start = pl.multiple_of(j * tq, tq)
        for h in range(MLA_HEADS):
            hs = slice(h * MLA_HEAD_PAD, (h + 1) * MLA_HEAD_PAD)
            kb = k_ref[0, pl.ds(start, tq), hs]
            st = lax.dot_general(kb, q_ref[0, :, hs], (((1,), (1,)), ((), ())),
                                 preferred_element_type=F32)
            if masked:
                kpos = lax.broadcasted_iota(jnp.int32, st.shape, 0)
                qpos = lax.broadcasted_iota(jnp.int32, st.shape, 1)
                st = jnp.where(kpos <= qpos, st, MASKED_SCORE)
            s_ref[slot, h] = st.astype(BF16)
        for h in range(MLA_HEADS):
            sb = s_ref[slot, h]
            m_old = m_ref[h:h + 1, :]
            m_new = jnp.maximum(m_old, jnp.max(sb, axis=0, keepdims=True).astype(F32))
            p_ref[slot, h] = jnp.exp2(sb - m_new.astype(BF16))
            a_ref[slot, h:h + 1, :] = jnp.exp2(m_old - m_new)
            m_ref[h:h + 1, :] = m_new
        for h in range(MLA_HEADS):
            vs = slice(h * MLA_V, (h + 1) * MLA_V)
            pr = p_ref[slot, h]
            pv = _mm(vt_ref[0, vs, pl.ds(start, tq)], pr)
            psum = _mm(ones_rows, pr)[0:1, :]
            alpha = a_ref[slot, h:h + 1, :]
            l_ref[h:h + 1, :] = alpha * l_ref[h:h + 1, :] + psum
            acc_ref[vs, :] = alpha * acc_ref[vs, :] + pv

    def body(i, carry):
        tile(2 * i, False, 0)
        tile(2 * i + 1, False, 1)
        return carry

    lax.fori_loop(0, qi // 2, body, 0)

    @pl.when(qi % 2 == 1)
    def _():
        tile(qi - 1, False, 0)

    tile(qi, True, 1)
    for h in range(MLA_HEADS):
        vs = slice(h * MLA_V, (h + 1) * MLA_V)
        acc_ref[vs, :] = acc_ref[vs, :] / l_ref[h:h + 1, :]
    o_ref[0] = acc_ref[...].T.astype(o_ref.dtype)


def _mla_attn(q, k, vt):
    B, S, W = q.shape
    tq = min(ATT_TILE, S)
    WV = MLA_HEADS * MLA_V
    return pl.pallas_call(
        _mla_attn_kernel,
        grid=(B, S // tq),
        in_specs=[pl.BlockSpec((1, tq, W), lambda b, i: (b, i, 0)),
                  pl.BlockSpec((1, S, W), lambda b, i: (b, 0, 0)),
                  pl.BlockSpec((1, WV, S), lambda b, i: (b, 0, 0))],
        out_specs=pl.BlockSpec((1, tq, WV), lambda b, i: (b, i, 0)),
        out_shape=jax.ShapeDtypeStruct((B, S, WV), BF16),
        scratch_shapes=[pltpu.VMEM((WV, tq), F32), pltpu.VMEM((MLA_HEADS, tq), F32),
                        pltpu.VMEM((MLA_HEADS, tq), F32), pltpu.VMEM((2, MLA_HEADS, tq, tq), BF16),
                        pltpu.VMEM((2, MLA_HEADS, tq, tq), BF16),
                        pltpu.VMEM((2, MLA_HEADS, tq), F32)],
        compiler_params=_params("parallel", "arbitrary"),
        name="mla_attn",
    )(q, k, vt)


def _merge_kernel(x_ref, ya_ref, yb_ref, yc_ref, g_ref, wg_ref, bg_ref, wb_ref, wo_ref, o_ref):
    x = x_ref[...]
    h = _rms(x, g_ref[...]).astype(BF16)
    merged = None
    for n, y_ref in enumerate((ya_ref, yb_ref, yc_ref)):
        sl = slice(n * D_MODEL, (n + 1) * D_MODEL)
        gate = _sigmoid(_mm(h, wg_ref[:, sl]) + bg_ref[:, sl])
        term = gate * _mm(y_ref[...], wb_ref[n])
        merged = term if merged is None else merged + term
    o_ref[...] = x + _dot(merged, wo_ref[...])


def _merge(x2, ya, yb, yc, g, w_gate, b_gate, w_branch, w_out, l):
    T = x2.shape[0]
    tm = min(TOK_TILE, T)
    lay = lambda arr: _layer(arr, l)
    return pl.pallas_call(
        _merge_kernel,
        grid=(T // tm,),
        in_specs=[_rows(tm, D_MODEL), _rows(tm, BRANCH_W), _rows(tm, BRANCH_W), _rows(tm, BRANCH_W),
                  lay(g), lay(w_gate), lay(b_gate), lay(w_branch), lay(w_out)],
        out_specs=_rows(tm, D_MODEL),
        out_shape=jax.ShapeDtypeStruct((T, D_MODEL), F32),
        compiler_params=_params("parallel"),
        name="merge",
    )(x2, ya, yb, yc, g, w_gate, b_gate, w_branch, w_out)


def _mem_kv_kernel(m_ref, g_ref, wkv_ref, kg_ref, k_o, v_o):
    h = _rms(m_ref[...], g_ref[...]).astype(BF16)
    kv = _mm(h, wkv_ref[...])
    for hd in range(XA_HEADS):
        sl = slice(hd * XA_HEAD_DIM, (hd + 1) * XA_HEAD_DIM)
        base = 2 * hd * XA_HEAD_DIM
        k_o[:, sl] = _rms(kv[:, base:base + XA_HEAD_DIM], kg_ref[...]).astype(BF16)
        v_o[:, sl] = kv[:, base + XA_HEAD_DIM:base + 2 * XA_HEAD_DIM].astype(BF16)


def _mem_kv(mem2, g, wkv, k_gain, l):
    M = mem2.shape[0]
    tm = min(TOK_TILE, M)
    lay = lambda arr: _layer(arr, l)
    return pl.pallas_call(
        _mem_kv_kernel,
        grid=(M // tm,),
        in_specs=[_rows(tm, D_MODEL), lay(g), lay(wkv), lay(k_gain)],
        out_specs=[_rows(tm, XA_W), _rows(tm, XA_W)],
        out_shape=[jax.ShapeDtypeStruct((M, XA_W), BF16)] * 2,
        compiler_params=_params("parallel"),
        name="mem_kv",
    )(mem2, g, wkv, k_gain)


def _xattn_kernel(x_ref, k_ref, v_ref, g_ref, wq_ref, qg_ref, wo_ref, o_ref):
    x = x_ref[0]
    h = _rms(x, g_ref[...]).astype(BF16)
    q = _mm(h, wq_ref[...])
    scale = XA_HEAD_DIM ** -0.5
    outs = []
    for hd in range(XA_HEADS):
        sl = slice(hd * XA_HEAD_DIM, (hd + 1) * XA_HEAD_DIM)
        qh = _rms(q[:, sl], qg_ref[...]) * scale
        s = _dot_nt(qh, k_ref[0, :, sl])
        pr = jnp.exp(s - jnp.max(s, axis=-1, keepdims=True))
        pr = pr / jnp.sum(pr, axis=-1, keepdims=True)
        outs.append(_dot(pr, v_ref[0, :, sl]))
    o = jnp.concatenate(outs, axis=-1)
    o_ref[0] = x + _dot(o, wo_ref[...])


def _xattn(x3, k3, v3, g, wq, q_gain, wo, l):
    B, S, _ = x3.shape
    ts = min(TOK_TILE, S)
    M = k3.shape[1]
    lay = lambda arr: _layer(arr, l)
    return pl.pallas_call(
        _xattn_kernel,
        grid=(B, S // ts),
        in_specs=[pl.BlockSpec((1, ts, D_MODEL), lambda b, s: (b, s, 0)),
                  pl.BlockSpec((1, M, XA_W), lambda b, s: (b, 0, 0)),
                  pl.BlockSpec((1, M, XA_W), lambda b, s: (b, 0, 0)),
                  lay(g), lay(wq), lay(q_gain), lay(wo)],
        out_specs=pl.BlockSpec((1, ts, D_MODEL), lambda b, s: (b, s, 0)),
        out_shape=jax.ShapeDtypeStruct((B, S, D_MODEL), F32),
        compiler_params=_params("parallel", "parallel"),
        name="xattn",
    )(x3, k3, v3, g, wq, q_gain, wo)


FF_SPLIT = 2


def _ffn_kernel(x_ref, g_ref, w1_ref, w3_ref, w2_ref, o_ref):
    x = x_ref[...]
    h = _rms(x, g_ref[...]).astype(BF16)
    step = D_FF // FF_SPLIT
    acc = x
    for c in range(FF_SPLIT):
        sl = slice(c * step, (c + 1) * step)
        a = _mm(h, w1_ref[:, sl])
        b = _mm(h, w3_ref[:, sl])
        z = a * _sigmoid(a) * b
        acc = acc + _dot(z, w2_ref[sl, :])
    o_ref[...] = acc


def _ffn(x2, g, w1, w3, w2, l):
    T = x2.shape[0]
    tm = min(TOK_TILE, T)
    lay = lambda arr: _layer(arr, l)
    return pl.pallas_call(
        _ffn_kernel,
        grid=(T // tm,),
        in_specs=[_rows(tm, D_MODEL), lay(g), lay(w1), lay(w3), lay(w2)],
        out_specs=_rows(tm, D_MODEL),
        out_shape=jax.ShapeDtypeStruct((T, D_MODEL), F32),
        compiler_params=_params("parallel"),
        name="ffn",
    )(x2, g, w1, w3, w2)


def _block_diag(w):
    L, n, i, j = w.shape
    eye = jnp.eye(n, dtype=w.dtype)
    return jnp.einsum("lnij,nm->lnimj", w, eye).reshape(L, n * i, n * j)


def _pad_heads(w, heads, width, pad_to):
    lead = w.shape[:-1]
    w = w.reshape(lead + (heads, width))
    w = jnp.pad(w, [(0, 0)] * len(lead) + [(0, 0), (0, pad_to - width)])
    return w.reshape(lead + (heads * pad_to,))


def _rope_partner(w):
    half = MLA_ROPE // 2
    return jnp.concatenate([jnp.zeros_like(w[..., :MLA_NOPE]), w[..., MLA_NOPE + half:],
                            w[..., MLA_NOPE:MLA_NOPE + half]], axis=-1)


def _vec(v):
    return v.reshape(v.shape[0], 1, -1).astype(F32)


def kernel(x, mem, positions, norm_mix, norm_xattn, norm_mem, norm_ffn, w_in, b_gate, rwkv_mu, rwkv_w0, rwkv_w_up, rwkv_a0, rwkv_a_up, rwkv_g_up, rwkv_k_k, rwkv_k_a, rwkv_r_k, rwkv_ln_g, rwkv_ln_b, lru_conv_w, lru_conv_b, lru_wa, lru_ba, lru_wx, lru_bx, lru_lambda, mla_q_norm, mla_w_uq, mla_kv_norm, mla_w_ukv, mla_q_gain, mla_k_gain, w_branch, w_out, xa_w_q, xa_w_kv, xa_q_gain, xa_k_gain, xa_w_o, ffn_w1, ffn_w3, ffn_w2):
    B, S, D = x.shape
    T = B * S
    depth = w_in.shape[0]
    x2 = x.reshape(T, D)
    mem2 = mem.reshape(B * N_MEM, D)
    half = MLA_ROPE // 2

    inv_freq = ROPE_THETA ** (-jnp.arange(0, MLA_ROPE, 2, dtype=F32) / MLA_ROPE)
    per_row = LANES // half
    pos_rep = jnp.repeat(positions.reshape(T // per_row, per_row), half, axis=1)
    cos_c, sin_c, nsin_c = _rope_tables(pos_rep, jnp.tile(inv_freq, per_row).reshape(1, LANES))
    cos_c, sin_c, nsin_c = (t.reshape(T, half) for t in (cos_c, sin_c, nsin_c))
    ones = jnp.ones((T, MLA_NOPE), F32)
    zeros = jnp.zeros((T, MLA_NOPE), F32)
    cosf = jnp.concatenate([ones, cos_c, cos_c, ones[:, :LANES - MLA_QK]], axis=1)
    sinf = jnp.concatenate([zeros, nsin_c, sin_c, zeros[:, :LANES - MLA_QK]], axis=1)

    w_in_b = w_in[:, :, :MLA_OFF].astype(BF16)
    w_kr = w_in[:, :, MLA_OFF + Q_RANK + KV_RANK:GATE_OFF].astype(BF16)
    zero_cols = lambda n: jnp.zeros((depth, D, n), BF16)
    w_mla = jnp.concatenate(
        [w_in[:, :, MLA_OFF:MLA_OFF + Q_RANK + KV_RANK].astype(BF16), zero_cols(MLA_NOPE), w_kr,
         zero_cols(LANES - MLA_QK), zero_cols(MLA_NOPE), w_kr[:, :, half:], w_kr[:, :, :half],
         zero_cols(LANES - MLA_QK)], axis=2)
    w_gate = w_in[:, :, GATE_OFF:].astype(BF16)

    zeros_lora = jnp.zeros((depth, W_LORA, RWKV_W), F32)
    wup_pad = jnp.concatenate([rwkv_w_up, zeros_lora], axis=1).astype(BF16)
    aup_pad = jnp.concatenate([zeros_lora, rwkv_a_up], axis=1).astype(BF16)
    gup = rwkv_g_up.astype(BF16)
    r_k = rwkv_r_k.reshape(depth, RWKV_W)

    wa_bd = _block_diag(lru_wa).astype(BF16)
    wx_bd = _block_diag(lru_wx).astype(BF16)

    wuq_h = mla_w_uq.reshape(depth, Q_RANK, MLA_HEADS, MLA_QK)
    wuq2 = jnp.concatenate(
        [_pad_heads(mla_w_uq, MLA_HEADS, MLA_QK, MLA_HEAD_PAD),
         _pad_heads(_rope_partner(wuq_h).reshape(depth, Q_RANK, -1), MLA_HEADS, MLA_QK, MLA_HEAD_PAD)],
        axis=2).astype(BF16)
    wukv = mla_w_ukv.reshape(depth, KV_RANK, MLA_HEADS, MLA_NOPE + MLA_V)
    wuk = _pad_heads(wukv[..., :MLA_NOPE].reshape(depth, KV_RANK, -1), MLA_HEADS, MLA_NOPE,
                     MLA_HEAD_PAD).astype(BF16)
    wuvt = jnp.swapaxes(wukv[..., MLA_NOPE:].reshape(depth, KV_RANK, -1), 1, 2).astype(BF16)
    pad_gain = lambda gv: _vec(jnp.pad(gv, ((0, 0), (0, LANES - MLA_QK))))
    qg, qgs = pad_gain(mla_q_gain), pad_gain(_rope_partner(mla_q_gain))
    kg, kgs = pad_gain(mla_k_gain), pad_gain(_rope_partner(mla_k_gain))

    w_branch_b = w_branch.astype(BF16)
    w_out_b = w_out.astype(BF16)
    xa_wkv = xa_w_kv.astype(BF16)
    xa_wq = xa_w_q.astype(BF16)
    xa_wo = xa_w_o.astype(BF16)
    w1, w3, w2 = ffn_w1.astype(BF16), ffn_w3.astype(BF16), ffn_w2.astype(BF16)

    n_mix, n_xa, n_mem, n_ffn = _vec(norm_mix), _vec(norm_xattn), _vec(norm_mem), _vec(norm_ffn)
    rw = [_vec(t) for t in (rwkv_mu, rwkv_w0, rwkv_a0, rwkv_k_k, rwkv_k_a, r_k, rwkv_ln_g, rwkv_ln_b)]
    mu, w0, a0, k_k, k_a, r_kv, ln_g, ln_b = rw
    conv_b, ba, bx, lam = _vec(lru_conv_b), _vec(lru_ba), _vec(lru_bx), _vec(lru_lambda)
    q_norm, kv_norm = _vec(mla_q_norm), _vec(mla_kv_norm)
    b_gate_v, xa_qg, xa_kg = _vec(b_gate), _vec(xa_q_gain), _vec(xa_k_gain)

    lru_w = (lru_conv_w, conv_b, wa_bd, ba, wx_bd, bx, lam)
    mla_w = (q_norm, wuq2, kv_norm, wuk, wuvt, qg, qgs, kg, kgs)
    W = MLA_HEADS * MLA_HEAD_PAD
    for l in range(depth):
        p_rwkv, y_b, q, k, vt = _in_proj(x2, n_mix, w_in_b, w_mla, lru_w, mla_w, cosf, sinf, B, S, l)

        y_a = _rwkv(p_rwkv.reshape(B, S, RWKV_IN), mu, w0, wup_pad, a0, aup_pad, gup, k_k, k_a,
                    r_kv, ln_g, ln_b, l).reshape(T, RWKV_W)
        y_c = _mla_attn(q.reshape(B, S, W), k.reshape(B, S, W), vt).reshape(T, MLA_HEADS * MLA_V)

        x2 = _merge(x2, y_a, y_b, y_c, n_mix, w_gate, b_gate_v, w_branch_b, w_out_b, l)

        mk, mv = _mem_kv(mem2, n_mem, xa_wkv, xa_kg, l)
        x2 = _xattn(x2.reshape(B, S, D), mk.reshape(B, N_MEM, XA_W), mv.reshape(B, N_MEM, XA_W),
                    n_xa, xa_wq, xa_qg, xa_wo, l).reshape(T, D)

        x2 = _ffn(x2, n_ffn, w1, w3, w2, l)
    return x2.reshape(B, S, D)
```

```python
import functools
import math

import jax
import jax.numpy as jnp
from jax import lax
from jax.experimental import pallas as pl
from jax.experimental.pallas import tpu as pltpu

F32 = jnp.float32
BF16 = jnp.bfloat16

D_MODEL = 1024
N_MEM = 256
RWKV_HEADS = 8
RWKV_HEAD_DIM = 64
RWKV_W = RWKV_HEADS * RWKV_HEAD_DIM
W_LORA = 64
A_LORA = 64
G_LORA = 128
RWKV_IN = 3 * RWKV_W + W_LORA + A_LORA + G_LORA
RWKV_LN_EPS = RWKV_HEAD_DIM * 1e-5
LRU_BLOCKS = 8
LRU_W = 512
CONV_WIDTH = 4
LRU_C = 8.0
MLA_HEADS = 8
MLA_NOPE = 64
MLA_ROPE = 32
MLA_QK = MLA_NOPE + MLA_ROPE
MLA_V = 64
Q_RANK = 256
KV_RANK = 128
ROPE_THETA = 10000.0
N_BRANCH = 3
BRANCH_W = 512
XA_HEADS = 4
XA_HEAD_DIM = 128
XA_W = XA_HEADS * XA_HEAD_DIM
D_FF = -(-8 * D_MODEL // (3 * 256)) * 256
LRU_OFF = RWKV_IN
MLA_OFF = LRU_OFF + 2 * LRU_W
GATE_OFF = MLA_OFF + Q_RANK + KV_RANK + MLA_ROPE

LANES = 128
SUBLANES = 8
BF16_SUBLANES = 16
MXU_TILE = 256
VMEM_LIMIT = 56 * 1024 * 1024
MLA_HEAD_PAD = LANES
MLA_V_EXT = MLA_V + BF16_SUBLANES
MLA_PAD = Q_RANK + KV_RANK + 2 * LANES
MASKED_SCORE = -2.0 ** 100
RWKV_CHUNK = 64
RWKV_GROUP = 8
RWKV_LANE_GROUP = MXU_TILE // RWKV_HEAD_DIM
LRU_SCAN_ROWS = 64
TOK_TILE = 512
ATT_TILE = 256


def _params(*sem):
    return pltpu.CompilerParams(dimension_semantics=sem, vmem_limit_bytes=VMEM_LIMIT)


def _mm(a, b):
    return jnp.dot(a, b, preferred_element_type=F32)


def _dot(a, b):
    return _mm(a.astype(BF16), b.astype(BF16))


def _dot_nt(a, b):
    return lax.dot_general(a.astype(BF16), b.astype(BF16), (((1,), (1,)), ((), ())),
                           preferred_element_type=F32)


def _sigmoid(x):
    return 1.0 / (1.0 + jnp.exp(-x))


def _rms(x, g, eps=1e-6):
    return x * lax.rsqrt(jnp.mean(x * x, axis=-1, keepdims=True) + eps) * g


def _full(shape):
    n = len(shape)
    return pl.BlockSpec(shape, lambda *_: (0,) * n)


def _layer(arr, l):
    tail = arr.shape[1:]
    return pl.BlockSpec((None,) + tail, lambda *_: (l,) + (0,) * len(tail))


def _rows(tm, n):
    return pl.BlockSpec((tm, n), lambda i: (i, 0))


def _rope_kernel(pos_ref, freq_ref, cos_o, sin_o, nsin_o):
    ang = pos_ref[...].astype(F32) * freq_ref[...]
    s = jnp.sin(ang)
    cos_o[...] = jnp.cos(ang)
    sin_o[...] = s
    nsin_o[...] = -s


def _rope_tables(pos_rep, freq_tile):
    R = pos_rep.shape[0]
    tm = min(TOK_TILE, R)
    out = jax.ShapeDtypeStruct((R, LANES), F32)
    return pl.pallas_call(
        _rope_kernel,
        grid=(R // tm,),
        in_specs=[_rows(tm, LANES), _full((1, LANES))],
        out_specs=[_rows(tm, LANES)] * 3,
        out_shape=[out] * 3,
        compiler_params=_params("parallel"),
        name="rope_tables",
    )(pos_rep, freq_tile)


def _in_proj_kernel(x_ref, g_ref, wrl_ref, wm_ref,
                    cw_ref, cb_ref, wa_ref, ba_ref, wx_ref, bx_ref, lam_ref,
                    cos_ref, sin_ref, qn_ref, wuq_ref, kvn_ref, wuk_ref, wuv_ref,
                    qg_ref, qgs_ref, kg_ref, kgs_ref,
                    or_ref, yb_ref, q_o, k_o, vt_o, xpad, hcarry, *, tiles_per_row):
    first = pl.program_id(0) % tiles_per_row == 0
    h = _rms(x_ref[...], g_ref[...]).astype(BF16)
    p_lru = _mm(h, wrl_ref[:, RWKV_IN:])

    def rwkv_cols(c0):
        def run():
            res = _mm(h, wrl_ref[:, c0:c0 + MXU_TILE])
            or_ref[:, c0:c0 + MXU_TILE] = res
            return jnp.minimum(jnp.abs(res[0:1, 0:LANES]), 0.0)
        return run

    mla = {}

    def mla_dots():
        p_mla = _mm(h, wm_ref[...])
        qq, kn = _mla_project(p_mla, qn_ref, wuq_ref, kvn_ref, wuk_ref, wuv_ref, vt_o)
        mla.update(p=p_mla, qq=qq, kn=kn)
        return jnp.minimum(jnp.abs(qq[0:1, 0:LANES] + kn[0:1, 0:LANES]), 0.0)

    rwkv_dots = [rwkv_cols(c0) for c0 in range(0, RWKV_IN, MXU_TILE)]
    n_lru = len(rwkv_dots)
    yb_ref[...] = _lru_tile(p_lru, cw_ref, cb_ref, wa_ref, ba_ref, wx_ref, bx_ref, lam_ref,
                            xpad, hcarry, first,
                            background=[mla_dots] + rwkv_dots[:n_lru]).astype(yb_ref.dtype)
    _mla_rotate(mla["p"], mla["qq"], mla["kn"], cos_ref, sin_ref, qg_ref, qgs_ref, kg_ref, kgs_ref,
                q_o, k_o, background=rwkv_dots[n_lru:])


def _in_proj(x2, g, w_in_b, w_mla, lru_w, mla_w, cosf, sinf, B, S, l):
    T = x2.shape[0]
    tm = min(TOK_TILE, S)
    per_row = S // tm
    W = MLA_HEADS * MLA_HEAD_PAD
    lay = lambda arr: _layer(arr, l)
    return pl.pallas_call(
        functools.partial(_in_proj_kernel, tiles_per_row=per_row),
        grid=(T // tm,),
        in_specs=[_rows(tm, D_MODEL), lay(g), lay(w_in_b), lay(w_mla)] + [lay(w) for w in lru_w]
                 + [_rows(tm, LANES), _rows(tm, LANES)] + [lay(w) for w in mla_w],
        out_specs=[_rows(tm, RWKV_IN), _rows(tm, LRU_W), _rows(tm, W), _rows(tm, W),
                   pl.BlockSpec((1, MLA_HEADS * MLA_V_EXT, tm),
                                lambda i: (i // per_row, 0, i % per_row))],
        out_shape=[jax.ShapeDtypeStruct((T, RWKV_IN), F32),
                   jax.ShapeDtypeStruct((T, LRU_W), BF16),
                   jax.ShapeDtypeStruct((T, W), BF16), jax.ShapeDtypeStruct((T, W), BF16),
                   jax.ShapeDtypeStruct((B, MLA_HEADS * MLA_V_EXT, S), BF16)],
        scratch_shapes=[pltpu.VMEM((tm + SUBLANES, LRU_W), F32), pltpu.VMEM((SUBLANES, LRU_W), F32)],
        compiler_params=_params("arbitrary"),
        name="in_proj",
    )(x2, g, w_in_b, w_mla, *lru_w, cosf, sinf, *mla_w)


def _cumsum_rows(ltri, x):
    hi = x.astype(BF16)
    r1 = x - hi.astype(F32)
    mid = r1.astype(BF16)
    lo = (r1 - mid.astype(F32)).astype(BF16)
    return _mm(ltri, hi) + _mm(ltri, mid) + _mm(ltri, lo)


def _rwkv_kernel(p_ref, mu_ref, w0_ref, wup_ref, a0_ref, aup_ref, gup_ref, kkw_ref, ka_ref,
                 rk_ref, lng_ref, lnb_ref, ltri_ref, bd_ref, y_ref, carry, s_ref):
    @pl.when(pl.program_id(1) == 0)
    def _():
        carry[...] = jnp.zeros_like(carry)
        s_ref[...] = jnp.zeros_like(s_ref)

    C, N = RWKV_CHUNK, RWKV_HEAD_DIM
    GW = RWKV_LANE_GROUP * N
    n_groups = RWKV_W // GW
    bd = bd_ref[...]

    def head_sums(x):
        return jnp.concatenate([_mm(x[:, j * GW:(j + 1) * GW].astype(BF16), bd)
                                for j in range(n_groups)], axis=1)

    p = p_ref[0]
    ts = p.shape[0]
    G = ts // C
    prow = lax.broadcasted_iota(jnp.int32, p.shape, 0)
    prev = jnp.where(prow == 0, carry[SUBLANES - 1:SUBLANES, :], pltpu.roll(p, 1, 0))
    carry[...] = p[ts - SUBLANES:, :]
    pm = p + (prev - p) * mu_ref[...]
    o1, o2, o3 = RWKV_W, 2 * RWKV_W, 3 * RWKV_W
    r, k, v = pm[:, :o1], pm[:, o1:o2], pm[:, o2:o3]
    wa = pm[:, o3:o3 + W_LORA + A_LORA]
    gd = pm[:, o3 + W_LORA + A_LORA:]
    z = w0_ref[...] + _dot(jnp.tanh(wa), wup_ref[...])
    lw_all = -math.exp(-0.5) * _sigmoid(z)
    a = _sigmoid(a0_ref[...] + _dot(wa, aup_ref[...]))
    gate = _dot(_sigmoid(gd), gup_ref[...])
    kk = k * kkw_ref[...]
    kk = kk / jnp.maximum(jnp.sqrt(head_sums(kk * kk)), 1e-12)
    k2 = k * (1.0 + (a - 1.0) * ka_ref[...])
    bonus = head_sums(r * k2 * rk_ref[...]) * v
    kka = kk * a

    lane_head = lax.broadcasted_iota(jnp.int32, (C, GW), 1) // N
    head_sel = [lane_head == h for h in range(RWKV_LANE_GROUP)]

    def bdiag(x):
        zero = jnp.zeros_like(x)
        return jnp.concatenate([jnp.where(sel, x, zero) for sel in head_sel], axis=0)

    trow = lax.broadcasted_iota(jnp.int32, (C, GW), 0)
    tcol = lax.broadcasted_iota(jnp.int32, (C, GW), 1) % C
    strict = tcol < trow
    incl = tcol <= trow
    eye = (tcol == trow).astype(F32)
    vrow = lax.broadcasted_iota(jnp.int32, (GW, GW), 0) // N
    vcol = lax.broadcasted_iota(jnp.int32, (GW, GW), 1) // N
    same_head = vrow == vcol
    contract0 = (((0,), (0,)), ((), ()))
    contract1 = (((1,), (1,)), ((), ()))

    items = []
    for c in range(G):
        rows = slice(c * C, (c + 1) * C)
        lw = lw_all[rows]
        cum = _cumsum_rows(ltri_ref[...], lw)
        ge = jnp.exp(cum)
        gi = jnp.exp(-cum)
        At_all = (-kk[rows] * jnp.exp(cum - lw)).astype(BF16)
        Bt_all = (kka[rows] * gi).astype(BF16)
        Kt_all = (k2[rows] * gi).astype(BF16)
        Rt_all = (r[rows] * ge).astype(BF16)
        V_all = v[rows].astype(BF16)
        for j in range(n_groups):
            sl = slice(j * GW, (j + 1) * GW)
            items.append(dict(At=At_all[:, sl], Bt=Bt_all[:, sl], Kt=Kt_all[:, sl],
                              Rt=Rt_all[:, sl], V=V_all[:, sl], g_row=ge[C - 1:C, sl]))

    for it in items:
        AR = jnp.concatenate([it["At"], it["Rt"]], axis=0)
        BK = jnp.concatenate([bdiag(it["Bt"]), bdiag(it["Kt"])], axis=0)
        sc = lax.dot_general(AR, BK, contract1, preferred_element_type=F32)
        it["L"] = jnp.where(strict, sc[:C, :GW], 0.0)
        it["akm"] = jnp.where(strict, sc[:C, GW:], 0.0).astype(BF16)
        it["rbm"] = jnp.where(incl, sc[C:, :GW], 0.0).astype(BF16)
        it["rkm"] = jnp.where(incl, sc[C:, GW:], 0.0).astype(BF16)

    blk = 2
    for it in items:
        it["T"] = eye + jnp.where(trow // blk == tcol // blk, it["L"], 0.0)
    while blk < C:
        lvl = (trow // (2 * blk) == tcol // (2 * blk)) & (trow // blk != tcol // blk)
        for it in items:
            it["Tb"] = it["T"].astype(BF16)
            it["P"] = _mm(it["Tb"], bdiag(jnp.where(lvl, it["L"], 0.0).astype(BF16))).astype(BF16)
        for it in items:
            it["T"] = it["T"] + _mm(it["P"], bdiag(it["Tb"]))
        blk *= 2

    for it in items:
        it["Vd"] = bdiag(it["V"])
        it["akv"] = _mm(it["akm"], it["Vd"]).astype(BF16)
    for it in items:
        au = _mm(it["T"].astype(BF16), jnp.concatenate([bdiag(it["At"]), bdiag(it["akv"])], axis=1))
        it["A2"] = au[:, :GW].astype(BF16)
        it["U0"] = au[:, GW:].astype(BF16)
    for it in items:
        ry = _mm(it["rbm"], jnp.concatenate([bdiag(it["A2"]), bdiag(it["U0"])], axis=1))
        it["R2"] = (it["Rt"].astype(F32) + ry[:, :GW]).astype(BF16)
        it["Y0"] = ry[:, GW:] + _mm(it["rkm"], it["Vd"])
        it["Mq"] = (jnp.where(same_head, lax.dot_general(it["A2"], it["Bt"], contract0,
                                                         preferred_element_type=F32), 0.0)
                    * it["g_row"]).astype(BF16)
        it["Nq"] = jnp.where(same_head,
                             lax.dot_general(jnp.concatenate([it["U0"], it["V"]], axis=0),
                                             jnp.concatenate([it["Bt"], it["Kt"]], axis=0),
                                             contract0, preferred_element_type=F32),
                             0.0) * it["g_row"]

    y_chunks = []
    for c in range(G):
        ys = []
        for j in range(n_groups):
            it = items[c * n_groups + j]
            S0 = s_ref[j]
            Sb = S0.astype(BF16)
            ys.append(it["Y0"] + lax.dot_general(it["R2"], Sb, contract1,
                                                 preferred_element_type=F32))
            s_ref[j] = S0 * it["g_row"] + _mm(Sb, it["Mq"]) + it["Nq"]
        y_chunks.append(jnp.concatenate(ys, axis=1))
    Y = jnp.concatenate(y_chunks, axis=0)

    inv_n = 1.0 / N
    mean = head_sums(Y) * inv_n
    yc = Y - mean
    var = head_sums(yc * yc) * inv_n
    yn = yc * lax.rsqrt(var + RWKV_LN_EPS) * lng_ref[...] + lnb_ref[...]
    y_ref[0] = ((yn + bonus) * gate).astype(y_ref.dtype)


def _rwkv(p3, mu, w0, wup_pad, a0, aup_pad, gup, k_k, k_a, r_k, ln_g, ln_b, l):
    B, S, _ = p3.shape
    C = RWKV_CHUNK
    ts = min(RWKV_GROUP * C, S)
    group_w = RWKV_LANE_GROUP * RWKV_HEAD_DIM
    ltri = (jnp.arange(C)[None, :] <= jnp.arange(C)[:, None]).astype(BF16)
    head_of_lane = jnp.arange(group_w) // RWKV_HEAD_DIM
    bd = (head_of_lane[:, None] == head_of_lane[None, :]).astype(BF16)
    lay = lambda arr: _layer(arr, l)
    return pl.pallas_call(
        _rwkv_kernel,
        grid=(B, S // ts),
        in_specs=[pl.BlockSpec((1, ts, RWKV_IN), lambda b, s: (b, s, 0)), lay(mu), lay(w0),
                  lay(wup_pad), lay(a0), lay(aup_pad), lay(gup), lay(k_k), lay(k_a), lay(r_k),
                  lay(ln_g), lay(ln_b), _full((C, C)), _full((group_w, group_w))],
        out_specs=pl.BlockSpec((1, ts, RWKV_W), lambda b, s: (b, s, 0)),
        out_shape=jax.ShapeDtypeStruct((B, S, RWKV_W), BF16),
        scratch_shapes=[pltpu.VMEM((SUBLANES, RWKV_IN), F32),
                        pltpu.VMEM((RWKV_W // group_w, group_w, group_w), F32)],
        compiler_params=_params("parallel", "arbitrary"),
        name="rwkv",
    )(p3, mu, w0, wup_pad, a0, aup_pad, gup, k_k, k_a, r_k, ln_g, ln_b, ltri, bd)


def _shift_rows(x, d, fill):
    n = x.shape[0]
    if d % SUBLANES == 0:
        return jnp.concatenate([jnp.full((d,) + x.shape[1:], fill, x.dtype), x[:n - d]], axis=0)
    rows = lax.broadcasted_iota(jnp.int32, x.shape, 0)
    return jnp.where(rows < d, fill, pltpu.roll(x, d, 0))


def _scan_block(a, u):
    d = 1
    while d < a.shape[0]:
        u = u + a * _shift_rows(u, d, 0.0)
        a = a * _shift_rows(a, d, 1.0)
        d *= 2
    return a, u


def _lru_tile(p, cw_ref, cb_ref, wa_ref, ba_ref, wx_ref, bx_ref, lam_ref, xpad, hcarry, first,
              background=()):
    @pl.when(first)
    def _():
        xpad[0:SUBLANES, :] = jnp.zeros((SUBLANES, LRU_W), F32)
        hcarry[...] = jnp.zeros_like(hcarry)

    ts = p.shape[0]
    xb, gb = p[:, :LRU_W], p[:, LRU_W:]
    xpad[SUBLANES:, :] = xb
    cw = cw_ref[...]
    xc = cb_ref[...] + xb * cw[CONV_WIDTH - 1:CONV_WIDTH, :]
    for j in range(CONV_WIDTH - 1):
        lo = SUBLANES - (CONV_WIDTH - 1) + j
        xc = xc + xpad[lo:lo + ts, :] * cw[j:j + 1, :]
    xpad[0:SUBLANES, :] = xb[ts - SUBLANES:, :]
    rg = _sigmoid(_dot(xc, wa_ref[...]) + ba_ref[...])
    ig = _sigmoid(_dot(xc, wx_ref[...]) + bx_ref[...])
    lam = lam_ref[...]
    softplus_neg_lam = jnp.maximum(-lam, 0.0) + jnp.log(1.0 + jnp.exp(-jnp.abs(lam)))
    log_a = -LRU_C * rg * softplus_neg_lam
    a = jnp.exp(log_a)
    u = jnp.sqrt(1.0 - a * a) * (ig * xc)
    pending = list(background)
    n_row_blocks = ts // LRU_SCAN_ROWS
    stride = max(1, (LRU_W // LANES) * n_row_blocks // max(1, len(pending)))
    strips = []
    for ls in range(LRU_W // LANES):
        cols = slice(ls * LANES, (ls + 1) * LANES)
        h_in = hcarry[SUBLANES - 1:SUBLANES, cols]
        blocks = []
        for rt in range(n_row_blocks):
            if pending and (ls * n_row_blocks + rt) % stride == 0:
                h_in = h_in + pending.pop(0)()
            rs = slice(rt * LRU_SCAN_ROWS, (rt + 1) * LRU_SCAN_ROWS)
            a_blk, u_blk = _scan_block(a[rs, cols], u[rs, cols])
            h_blk = u_blk + a_blk * h_in
            h_in = h_blk[LRU_SCAN_ROWS - 1:, :]
            blocks.append(h_blk)
        strips.append(jnp.concatenate(blocks, axis=0))
    for thunk in pending:
        thunk()
    h = jnp.concatenate(strips, axis=1)
    hcarry[...] = h[ts - SUBLANES:, :]
    gelu = 0.5 * gb * (1.0 + jnp.tanh(math.sqrt(2.0 / math.pi) * (gb + 0.044715 * gb * gb * gb)))
    return h * gelu


def _mla_project(p, qn_ref, wuq_ref, kvn_ref, wuk_ref, wuv_ref, vt_o):
    cq = p[:, :Q_RANK]
    ckv = p[:, Q_RANK:Q_RANK + KV_RANK]
    qq = _dot(_rms(cq, qn_ref[...]), wuq_ref[...])
    ckv_n = _rms(ckv, kvn_ref[...]).astype(BF16)
    kn = _mm(ckv_n, wuk_ref[...])
    one_lane = (lax.broadcasted_iota(jnp.int32, ckv_n.shape, 1) == 0).astype(BF16)
    vt_o[0] = _dot_nt(wuv_ref[...], jnp.concatenate([ckv_n, one_lane], axis=1)).astype(BF16)
    return qq, kn


def _mla_rotate(p, qq, kn, cos_ref, sin_ref, qg_ref, qgs_ref, kg_ref, kgs_ref, q_o, k_o,
                background=()):
    W = MLA_HEADS * MLA_HEAD_PAD
    kr = p[:, Q_RANK + KV_RANK:Q_RANK + KV_RANK + LANES]
    kr_sw = p[:, Q_RANK + KV_RANK + LANES:]
    cosf = cos_ref[...]
    sinf = sin_ref[...]
    scale = MLA_QK ** -0.5 * math.log2(math.e)
    cq_tab = cosf * (qg_ref[...] * scale)
    sq_tab = sinf * (qgs_ref[...] * scale)
    ck_tab = cosf * kg_ref[...]
    sk_tab = sinf * kgs_ref[...]
    kr_rot = kr * ck_tab + kr_sw * sk_tab
    kr_ss = jnp.sum(kr * kr, axis=-1, keepdims=True)
    pending = list(background)
    for h in range(MLA_HEADS):
        sl = slice(h * MLA_HEAD_PAD, (h + 1) * MLA_HEAD_PAD)
        qh = qq[:, sl]
        q_rs = lax.rsqrt(jnp.sum(qh * qh, axis=-1, keepdims=True) / MLA_QK + 1e-6)
        if pending:
            q_rs = q_rs + pending.pop(0)()[:, 0:1]
        q_o[:, sl] = ((qh * cq_tab + qq[:, W + h * MLA_HEAD_PAD:W + (h + 1) * MLA_HEAD_PAD] * sq_tab)
                      * q_rs).astype(BF16)
        kh = kn[:, sl]
        k_rs = lax.rsqrt((jnp.sum(kh * kh, axis=-1, keepdims=True) + kr_ss) / MLA_QK + 1e-6)
        k_o[:, sl] = ((kh * ck_tab + kr_rot) * k_rs).astype(BF16)
    for thunk in pending:
        thunk()


def _mla_attn_kernel(q_ref, k_ref, vt_ref, o_ref, acc_ref, m_ref, l_ref, s_ref, p_ref, a_ref):
    qi = pl.program_id(1)
    tq = q_ref.shape[1]
    m_ref[...] = jnp.full(m_ref.shape, MASKED_SCORE, F32)
    l_ref[...] = jnp.zeros(l_ref.shape, F32)
    acc_ref[...] = jnp.zeros(acc_ref.shape, F32)

    def scores(j, masked, slot):
        start = pl.multiple_of(j * tq, tq)
        ties = []
        for h in range(MLA_HEADS):
            hs = slice(h * MLA_HEAD_PAD, (h + 1) * MLA_HEAD_PAD)
            kb = k_ref[0, pl.ds(start, tq), hs]
            st = lax.dot_general(kb, q_ref[0, :, hs], (((1,), (1,)), ((), ())),
                                 preferred_element_type=F32)
            if masked:
                kpos = lax.broadcasted_iota(jnp.int32, st.shape, 0)
                qpos = lax.broadcasted_iota(jnp.int32, st.shape, 1)
                st = jnp.where(kpos <= qpos, st, MASKED_SCORE)
            s_ref[slot, h] = st.astype(BF16)
            ties.append(jnp.minimum(jnp.abs(st[0:1, :]), 0.0))
        return ties

    def softmax(slot, ties=None):
        for h in range(MLA_HEADS):
            sb = s_ref[slot, h]
            m_old = m_ref[h:h + 1, :]
            if ties is not None:
                m_old = m_old + ties[h]
            m_new = jnp.maximum(m_old, jnp.max(sb, axis=0, keepdims=True).astype(F32))
            p_ref[slot, h] = jnp.exp2(sb - m_new.astype(BF16))
            a_ref[slot, h:h + 1, :] = jnp.exp2(m_old - m_new)
            m_ref[h:h + 1, :] = m_new

    def values(j, slot):
        start = pl.multiple_of(j * tq, tq)
        for h in range(MLA_HEADS):
            vs = slice(h * MLA_V, (h + 1) * MLA_V)
            ve = slice(h * MLA_V_EXT, (h + 1) * MLA_V_EXT)
            pvx = _mm(vt_ref[0, ve, pl.ds(start, tq)], p_ref[slot, h])
            alpha = a_ref[slot, h:h + 1, :]
            l_ref[h:h + 1, :] = alpha * l_ref[h:h + 1, :] + pvx[MLA_V:MLA_V + 1, :]
            acc_ref[vs, :] = alpha * acc_ref[vs, :] + pvx[:MLA_V, :]

    def tile(j, masked, slot):
        scores(j, masked, slot)
        softmax(slot)
        values(j, slot)

    def body(i, carry):
        scores(2 * i, False, 0)
        ties = scores(2 * i + 1, False, 1)
        softmax(0, ties)
        values(2 * i, 0)
        softmax(1)
        values(2 * i + 1, 1)
        return carry

    lax.fori_loop(0, qi // 2, body, 0)

    @pl.when(qi % 2 == 1)
    def _():
        tile(qi - 1, False, 0)

    tile(qi, True, 1)
    for h in range(MLA_HEADS):
        vs = slice(h * MLA_V, (h + 1) * MLA_V)
        acc_ref[vs, :] = acc_ref[vs, :] / l_ref[h:h + 1, :]
    o_ref[0] = acc_ref[...].T.astype(o_ref.dtype)


def _mla_attn(q, k, vt):
    B, S, W = q.shape
    tq = min(ATT_TILE, S)
    WV = MLA_HEADS * MLA_V
    return pl.pallas_call(
        _mla_attn_kernel,
        grid=(B, S // tq),
        in_specs=[pl.BlockSpec((1, tq, W), lambda b, i: (b, i, 0)),
                  pl.BlockSpec((1, S, W), lambda b, i: (b, 0, 0)),
                  pl.BlockSpec((1, MLA_HEADS * MLA_V_EXT, S), lambda b, i: (b, 0, 0))],
        out_specs=pl.BlockSpec((1, tq, WV), lambda b, i: (b, i, 0)),
        out_shape=jax.ShapeDtypeStruct((B, S, WV), BF16),
        scratch_shapes=[pltpu.VMEM((WV, tq), F32), pltpu.VMEM((MLA_HEADS, tq), F32),
                        pltpu.VMEM((MLA_HEADS, tq), F32), pltpu.VMEM((2, MLA_HEADS, tq, tq), BF16),
                        pltpu.VMEM((2, MLA_HEADS, tq, tq), BF16),
                        pltpu.VMEM((2, MLA_HEADS, tq), F32)],
        compiler_params=_params("parallel", "arbitrary"),
        name="mla_attn",
    )(q, k, vt)


def _merge_kernel(x_ref, ya_ref, yb_ref, yc_ref, g_ref, wg_ref, bg_ref, wb_ref, wo_ref, o_ref):
    x = x_ref[...]
    h = _rms(x, g_ref[...]).astype(BF16)
    merged = None
    for n, y_ref in enumerate((ya_ref, yb_ref, yc_ref)):
        sl = slice(n * D_MODEL, (n + 1) * D_MODEL)
        gate = _sigmoid(_mm(h, wg_ref[:, sl]) + bg_ref[:, sl])
        term = gate * _mm(y_ref[...], wb_ref[n])
        merged = term if merged is None else merged + term
    o_ref[...] = x + _dot(merged, wo_ref[...])


def _merge(x2, ya, yb, yc, g, w_gate, b_gate, w_branch, w_out, l):
    T = x2.shape[0]
    tm = min(TOK_TILE, T)
    lay = lambda arr: _layer(arr, l)
    return pl.pallas_call(
        _merge_kernel,
        grid=(T // tm,),
        in_specs=[_rows(tm, D_MODEL), _rows(tm, BRANCH_W), _rows(tm, BRANCH_W), _rows(tm, BRANCH_W),
                  lay(g), lay(w_gate), lay(b_gate), lay(w_branch), lay(w_out)],
        out_specs=_rows(tm, D_MODEL),
        out_shape=jax.ShapeDtypeStruct((T, D_MODEL), F32),
        compiler_params=_params("parallel"),
        name="merge",
    )(x2, ya, yb, yc, g, w_gate, b_gate, w_branch, w_out)


def _mem_kv_kernel(m_ref, g_ref, wkv_ref, kg_ref, k_o, v_o):
    h = _rms(m_ref[...], g_ref[...]).astype(BF16)
    kv = _mm(h, wkv_ref[...])
    for hd in range(XA_HEADS):
        sl = slice(hd * XA_HEAD_DIM, (hd + 1) * XA_HEAD_DIM)
        base = 2 * hd * XA_HEAD_DIM
        k_o[:, sl] = _rms(kv[:, base:base + XA_HEAD_DIM], kg_ref[...]).astype(BF16)
        v_o[:, sl] = kv[:, base + XA_HEAD_DIM:base + 2 * XA_HEAD_DIM].astype(BF16)


def _mem_kv(mem2, g, wkv, k_gain, l):
    M = mem2.shape[0]
    tm = min(TOK_TILE, M)
    lay = lambda arr: _layer(arr, l)
    return pl.pallas_call(
        _mem_kv_kernel,
        grid=(M // tm,),
        in_specs=[_rows(tm, D_MODEL), lay(g), lay(wkv), lay(k_gain)],
        out_specs=[_rows(tm, XA_W), _rows(tm, XA_W)],
        out_shape=[jax.ShapeDtypeStruct((M, XA_W), BF16)] * 2,
        compiler_params=_params("parallel"),
        name="mem_kv",
    )(mem2, g, wkv, k_gain)


def _xattn_kernel(x_ref, k_ref, v_ref, g_ref, wq_ref, qg_ref, wo_ref, o_ref):
    x = x_ref[0]
    h = _rms(x, g_ref[...]).astype(BF16)
    q = _mm(h, wq_ref[...])
    scale = XA_HEAD_DIM ** -0.5
    outs = []
    for hd in range(XA_HEADS):
        sl = slice(hd * XA_HEAD_DIM, (hd + 1) * XA_HEAD_DIM)
        qh = _rms(q[:, sl], qg_ref[...]) * scale
        s = _dot_nt(qh, k_ref[0, :, sl])
        pr = jnp.exp(s - jnp.max(s, axis=-1, keepdims=True))
        pr = pr / jnp.sum(pr, axis=-1, keepdims=True)
        outs.append(_dot(pr, v_ref[0, :, sl]))
    o = jnp.concatenate(outs, axis=-1)
    o_ref[0] = x + _dot(o, wo_ref[...])


def _xattn(x3, k3, v3, g, wq, q_gain, wo, l):
    B, S, _ = x3.shape
    ts = min(TOK_TILE, S)
    M = k3.shape[1]
    lay = lambda arr: _layer(arr, l)
    return pl.pallas_call(
        _xattn_kernel,
        grid=(B, S // ts),
        in_specs=[pl.BlockSpec((1, ts, D_MODEL), lambda b, s: (b, s, 0)),
                  pl.BlockSpec((1, M, XA_W), lambda b, s: (b, 0, 0)),
                  pl.BlockSpec((1, M, XA_W), lambda b, s: (b, 0, 0)),
                  lay(g), lay(wq), lay(q_gain), lay(wo)],
        out_specs=pl.BlockSpec((1, ts, D_MODEL), lambda b, s: (b, s, 0)),
        out_shape=jax.ShapeDtypeStruct((B, S, D_MODEL), F32),
        compiler_params=_params("parallel", "parallel"),
        name="xattn",
    )(x3, k3, v3, g, wq, q_gain, wo)


FF_SPLIT = 2


def _ffn_kernel(x_ref, g_ref, w1_ref, w3_ref, w2_ref, o_ref):
    x = x_ref[...]
    h = _rms(x, g_ref[...]).astype(BF16)
    step = D_FF // FF_SPLIT
    acc = x
    for c in range(FF_SPLIT):
        sl = slice(c * step, (c + 1) * step)
        a = _mm(h, w1_ref[:, sl])
        b = _mm(h, w3_ref[:, sl])
        z = a * _sigmoid(a) * b
        acc = acc + _dot(z, w2_ref[sl, :])
    o_ref[...] = acc


def _ffn(x2, g, w1, w3, w2, l):
    T = x2.shape[0]
    tm = min(TOK_TILE, T)
    lay = lambda arr: _layer(arr, l)
    return pl.pallas_call(
        _ffn_kernel,
        grid=(T // tm,),
        in_specs=[_rows(tm, D_MODEL), lay(g), lay(w1), lay(w3), lay(w2)],
        out_specs=_rows(tm, D_MODEL),
        out_shape=jax.ShapeDtypeStruct((T, D_MODEL), F32),
        compiler_params=_params("parallel"),
        name="ffn",
    )(x2, g, w1, w3, w2)


def _block_diag(w):
    L, n, i, j = w.shape
    eye = jnp.eye(n, dtype=w.dtype)
    return jnp.einsum("lnij,nm->lnimj", w, eye).reshape(L, n * i, n * j)


def _pad_heads(w, heads, width, pad_to):
    lead = w.shape[:-1]
    w = w.reshape(lead + (heads, width))
    w = jnp.pad(w, [(0, 0)] * len(lead) + [(0, 0), (0, pad_to - width)])
    return w.reshape(lead + (heads * pad_to,))


def _rope_partner(w):
    half = MLA_ROPE // 2
    return jnp.concatenate([jnp.zeros_like(w[..., :MLA_NOPE]), w[..., MLA_NOPE + half:],
                            w[..., MLA_NOPE:MLA_NOPE + half]], axis=-1)


def _vec(v):
    return v.reshape(v.shape[0], 1, -1).astype(F32)


def kernel(x, mem, positions, norm_mix, norm_xattn, norm_mem, norm_ffn, w_in, b_gate, rwkv_mu, rwkv_w0, rwkv_w_up, rwkv_a0, rwkv_a_up, rwkv_g_up, rwkv_k_k, rwkv_k_a, rwkv_r_k, rwkv_ln_g, rwkv_ln_b, lru_conv_w, lru_conv_b, lru_wa, lru_ba, lru_wx, lru_bx, lru_lambda, mla_q_norm, mla_w_uq, mla_kv_norm, mla_w_ukv, mla_q_gain, mla_k_gain, w_branch, w_out, xa_w_q, xa_w_kv, xa_q_gain, xa_k_gain, xa_w_o, ffn_w1, ffn_w3, ffn_w2):
    B, S, D = x.shape
    T = B * S
    depth = w_in.shape[0]
    x2 = x.reshape(T, D)
    mem2 = mem.reshape(B * N_MEM, D)
    half = MLA_ROPE // 2

    inv_freq = ROPE_THETA ** (-jnp.arange(0, MLA_ROPE, 2, dtype=F32) / MLA_ROPE)
    per_row = LANES // half
    pos_rep = jnp.repeat(positions.reshape(T // per_row, per_row), half, axis=1)
    cos_c, sin_c, nsin_c = _rope_tables(pos_rep, jnp.tile(inv_freq, per_row).reshape(1, LANES))
    cos_c, sin_c, nsin_c = (t.reshape(T, half) for t in (cos_c, sin_c, nsin_c))
    ones = jnp.ones((T, MLA_NOPE), F32)
    zeros = jnp.zeros((T, MLA_NOPE), F32)
    cosf = jnp.concatenate([ones, cos_c, cos_c, ones[:, :LANES - MLA_QK]], axis=1)
    sinf = jnp.concatenate([zeros, nsin_c, sin_c, zeros[:, :LANES - MLA_QK]], axis=1)

    w_in_b = w_in[:, :, :MLA_OFF].astype(BF16)
    w_kr = w_in[:, :, MLA_OFF + Q_RANK + KV_RANK:GATE_OFF].astype(BF16)
    zero_cols = lambda n: jnp.zeros((depth, D, n), BF16)
    w_mla = jnp.concatenate(
        [w_in[:, :, MLA_OFF:MLA_OFF + Q_RANK + KV_RANK].astype(BF16), zero_cols(MLA_NOPE), w_kr,
         zero_cols(LANES - MLA_QK), zero_cols(MLA_NOPE), w_kr[:, :, half:], w_kr[:, :, :half],
         zero_cols(LANES - MLA_QK)], axis=2)
    w_gate = w_in[:, :, GATE_OFF:].astype(BF16)

    zeros_lora = jnp.zeros((depth, W_LORA, RWKV_W), F32)
    wup_pad = jnp.concatenate([rwkv_w_up, zeros_lora], axis=1).astype(BF16)
    aup_pad = jnp.concatenate([zeros_lora, rwkv_a_up], axis=1).astype(BF16)
    gup = rwkv_g_up.astype(BF16)
    r_k = rwkv_r_k.reshape(depth, RWKV_W)

    wa_bd = _block_diag(lru_wa).astype(BF16)
    wx_bd = _block_diag(lru_wx).astype(BF16)

    wuq_h = mla_w_uq.reshape(depth, Q_RANK, MLA_HEADS, MLA_QK)
    wuq2 = jnp.concatenate(
        [_pad_heads(mla_w_uq, MLA_HEADS, MLA_QK, MLA_HEAD_PAD),
         _pad_heads(_rope_partner(wuq_h).reshape(depth, Q_RANK, -1), MLA_HEADS, MLA_QK, MLA_HEAD_PAD)],
        axis=2).astype(BF16)
    wukv = mla_w_ukv.reshape(depth, KV_RANK, MLA_HEADS, MLA_NOPE + MLA_V)
    wuk = _pad_heads(wukv[..., :MLA_NOPE].reshape(depth, KV_RANK, -1), MLA_HEADS, MLA_NOPE,
                     MLA_HEAD_PAD).astype(BF16)
    wuvt = jnp.swapaxes(wukv[..., MLA_NOPE:], 1, 2).transpose(0, 1, 3, 2)
    wuvt = jnp.pad(wuvt, ((0, 0), (0, 0), (0, MLA_V_EXT - MLA_V), (0, KV_RANK)))
    wuvt = wuvt.at[:, :, MLA_V, KV_RANK].set(1.0)
    wuvt = wuvt.reshape(depth, MLA_HEADS * MLA_V_EXT, 2 * KV_RANK).astype(BF16)
    pad_gain = lambda gv: _vec(jnp.pad(gv, ((0, 0), (0, LANES - MLA_QK))))
    qg, qgs = pad_gain(mla_q_gain), pad_gain(_rope_partner(mla_q_gain))
    kg, kgs = pad_gain(mla_k_gain), pad_gain(_rope_partner(mla_k_gain))

    w_branch_b = w_branch.astype(BF16)
    w_out_b = w_out.astype(BF16)
    xa_wkv = xa_w_kv.astype(BF16)
    xa_wq = xa_w_q.astype(BF16)
    xa_wo = xa_w_o.astype(BF16)
    w1, w3, w2 = ffn_w1.astype(BF16), ffn_w3.astype(BF16), ffn_w2.astype(BF16)

    n_mix, n_xa, n_mem, n_ffn = _vec(norm_mix), _vec(norm_xattn), _vec(norm_mem), _vec(norm_ffn)
    rw = [_vec(t) for t in (rwkv_mu, rwkv_w0, rwkv_a0, rwkv_k_k, rwkv_k_a, r_k, rwkv_ln_g, rwkv_ln_b)]
    mu, w0, a0, k_k, k_a, r_kv, ln_g, ln_b = rw
    conv_b, ba, bx, lam = _vec(lru_conv_b), _vec(lru_ba), _vec(lru_bx), _vec(lru_lambda)
    q_norm, kv_norm = _vec(mla_q_norm), _vec(mla_kv_norm)
    b_gate_v, xa_qg, xa_kg = _vec(b_gate), _vec(xa_q_gain), _vec(xa_k_gain)

    lru_w = (lru_conv_w, conv_b, wa_bd, ba, wx_bd, bx, lam)
    mla_w = (q_norm, wuq2, kv_norm, wuk, wuvt, qg, qgs, kg, kgs)
    W = MLA_HEADS * MLA_HEAD_PAD
    for l in range(depth):
        p_rwkv, y_b, q, k, vt = _in_proj(x2, n_mix, w_in_b, w_mla, lru_w, mla_w, cosf, sinf, B, S, l)

        y_a = _rwkv(p_rwkv.reshape(B, S, RWKV_IN), mu, w0, wup_pad, a0, aup_pad, gup, k_k, k_a,
                    r_kv, ln_g, ln_b, l).reshape(T, RWKV_W)
        y_c = _mla_attn(q.reshape(B, S, W), k.reshape(B, S, W), vt).reshape(T, MLA_HEADS * MLA_V)

        x2 = _merge(x2, y_a, y_b, y_c, n_mix, w_gate, b_gate_v, w_branch_b, w_out_b, l)

        mk, mv = _mem_kv(mem2, n_mem, xa_wkv, xa_kg, l)
        x2 = _xattn(x2.reshape(B, S, D), mk.reshape(B, N_MEM, XA_W), mv.reshape(B, N_MEM, XA_W),
                    n_xa, xa_wq, xa_qg, xa_wo, l).reshape(T, D)

        x2 = _ffn(x2, n_ffn, w1, w3, w2, l)
    return x2.reshape(B, S, D)
```

```python
import functools
import math

import jax
import jax.numpy as jnp
from jax import lax
from jax.experimental import pallas as pl
from jax.experimental.pallas import tpu as pltpu

F32 = jnp.float32
BF16 = jnp.bfloat16

D_MODEL = 1024
N_MEM = 256
RWKV_HEADS = 8
RWKV_HEAD_DIM = 64
RWKV_W = RWKV_HEADS * RWKV_HEAD_DIM
W_LORA = 64
A_LORA = 64
G_LORA = 128
RWKV_IN = 3 * RWKV_W + W_LORA + A_LORA + G_LORA
RWKV_LN_EPS = RWKV_HEAD_DIM * 1e-5
LRU_BLOCKS = 8
LRU_W = 512
CONV_WIDTH = 4
LRU_C = 8.0
MLA_HEADS = 8
MLA_NOPE = 64
MLA_ROPE = 32
MLA_QK = MLA_NOPE + MLA_ROPE
MLA_V = 64
Q_RANK = 256
KV_RANK = 128
ROPE_THETA = 10000.0
N_BRANCH = 3
BRANCH_W = 512
XA_HEADS = 4
XA_HEAD_DIM = 128
XA_W = XA_HEADS * XA_HEAD_DIM
D_FF = -(-8 * D_MODEL // (3 * 256)) * 256
LRU_OFF = RWKV_IN
MLA_OFF = LRU_OFF + 2 * LRU_W
GATE_OFF = MLA_OFF + Q_RANK + KV_RANK + MLA_ROPE

LANES = 128
SUBLANES = 8
BF16_SUBLANES = 16
MXU_TILE = 256
VMEM_LIMIT = 56 * 1024 * 1024
MLA_HEAD_PAD = LANES
MLA_V_EXT = MLA_V + BF16_SUBLANES
MLA_PAD = Q_RANK + KV_RANK + 2 * LANES
MASKED_SCORE = -2.0 ** 100
RWKV_CHUNK = 64
RWKV_GROUP = 8
RWKV_WAVE = 4
RWKV_LANE_GROUP = MXU_TILE // RWKV_HEAD_DIM
LRU_SCAN_ROWS = 64
TOK_TILE = 512
ATT_TILE = 256


def _params(*sem):
    return pltpu.CompilerParams(dimension_semantics=sem, vmem_limit_bytes=VMEM_LIMIT)


def _mm(a, b):
    return jnp.dot(a, b, preferred_element_type=F32)


def _dot(a, b):
    return _mm(a.astype(BF16), b.astype(BF16))


def _dot_nt(a, b):
    return lax.dot_general(a.astype(BF16), b.astype(BF16), (((1,), (1,)), ((), ())),
                           preferred_element_type=F32)


def _sigmoid(x):
    return 1.0 / (1.0 + jnp.exp(-x))


def _rms(x, g, eps=1e-6):
    return x * lax.rsqrt(jnp.mean(x * x, axis=-1, keepdims=True) + eps) * g


def _full(shape):
    n = len(shape)
    return pl.BlockSpec(shape, lambda *_: (0,) * n)


def _layer(arr, l):
    tail = arr.shape[1:]
    return pl.BlockSpec((None,) + tail, lambda *_: (l,) + (0,) * len(tail))


def _rows(tm, n):
    return pl.BlockSpec((tm, n), lambda i: (i, 0))


def _rope_kernel(pos_ref, freq_ref, cos_o, sin_o, nsin_o):
    ang = pos_ref[...].astype(F32) * freq_ref[...]
    s = jnp.sin(ang)
    cos_o[...] = jnp.cos(ang)
    sin_o[...] = s
    nsin_o[...] = -s


def _rope_tables(pos_rep, freq_tile):
    R = pos_rep.shape[0]
    tm = min(TOK_TILE, R)
    out = jax.ShapeDtypeStruct((R, LANES), F32)
    return pl.pallas_call(
        _rope_kernel,
        grid=(R // tm,),
        in_specs=[_rows(tm, LANES), _full((1, LANES))],
        out_specs=[_rows(tm, LANES)] * 3,
        out_shape=[out] * 3,
        compiler_params=_params("parallel"),
        name="rope_tables",
    )(pos_rep, freq_tile)


def _in_proj_kernel(x_ref, g_ref, wrl_ref, wm_ref,
                    cw_ref, cb_ref, wa_ref, ba_ref, wx_ref, bx_ref, lam_ref,
                    cos_ref, sin_ref, qn_ref, wuq_ref, kvn_ref, wuk_ref, wuv_ref,
                    qg_ref, qgs_ref, kg_ref, kgs_ref,
                    or_ref, yb_ref, q_o, k_o, vt_o, xpad, hcarry, *, tiles_per_row):
    first = pl.program_id(0) % tiles_per_row == 0
    h = _rms(x_ref[...], g_ref[...]).astype(BF16)
    p_lru = _mm(h, wrl_ref[:, RWKV_IN:])

    def rwkv_cols(c0):
        def run():
            res = _mm(h, wrl_ref[:, c0:c0 + MXU_TILE])
            or_ref[:, c0:c0 + MXU_TILE] = res
            return jnp.minimum(jnp.abs(res[0:1, 0:LANES]), 0.0)
        return run

    mla = {}

    def mla_dots():
        p_mla = _mm(h, wm_ref[...])
        qq, kn = _mla_project(p_mla, qn_ref, wuq_ref, kvn_ref, wuk_ref, wuv_ref, vt_o)
        mla.update(p=p_mla, qq=qq, kn=kn)
        return jnp.minimum(jnp.abs(qq[0:1, 0:LANES] + kn[0:1, 0:LANES]), 0.0)

    rwkv_dots = [rwkv_cols(c0) for c0 in range(0, RWKV_IN, MXU_TILE)]
    n_lru = len(rwkv_dots)
    yb_ref[...] = _lru_tile(p_lru, cw_ref, cb_ref, wa_ref, ba_ref, wx_ref, bx_ref, lam_ref,
                            xpad, hcarry, first,
                            background=[mla_dots] + rwkv_dots[:n_lru]).astype(yb_ref.dtype)
    _mla_rotate(mla["p"], mla["qq"], mla["kn"], cos_ref, sin_ref, qg_ref, qgs_ref, kg_ref, kgs_ref,
                q_o, k_o, background=rwkv_dots[n_lru:])


def _in_proj(x2, g, w_in_b, w_mla, lru_w, mla_w, cosf, sinf, B, S, l):
    T = x2.shape[0]
    tm = min(TOK_TILE, S)
    per_row = S // tm
    W = MLA_HEADS * MLA_HEAD_PAD
    lay = lambda arr: _layer(arr, l)
    return pl.pallas_call(
        functools.partial(_in_proj_kernel, tiles_per_row=per_row),
        grid=(T // tm,),
        in_specs=[_rows(tm, D_MODEL), lay(g), lay(w_in_b), lay(w_mla)] + [lay(w) for w in lru_w]
                 + [_rows(tm, LANES), _rows(tm, LANES)] + [lay(w) for w in mla_w],
        out_specs=[_rows(tm, RWKV_IN), _rows(tm, LRU_W), _rows(tm, W), _rows(tm, W),
                   pl.BlockSpec((1, MLA_HEADS * MLA_V_EXT, tm),
                                lambda i: (i // per_row, 0, i % per_row))],
        out_shape=[jax.ShapeDtypeStruct((T, RWKV_IN), F32),
                   jax.ShapeDtypeStruct((T, LRU_W), BF16),
                   jax.ShapeDtypeStruct((T, W), BF16), jax.ShapeDtypeStruct((T, W), BF16),
                   jax.ShapeDtypeStruct((B, MLA_HEADS * MLA_V_EXT, S), BF16)],
        scratch_shapes=[pltpu.VMEM((tm + SUBLANES, LRU_W), F32), pltpu.VMEM((SUBLANES, LRU_W), F32)],
        compiler_params=_params("arbitrary"),
        name="in_proj",
    )(x2, g, w_in_b, w_mla, *lru_w, cosf, sinf, *mla_w)


def _cumsum_rows(ltri, x):
    hi = x.astype(BF16)
    r1 = x - hi.astype(F32)
    mid = r1.astype(BF16)
    lo = (r1 - mid.astype(F32)).astype(BF16)
    return _mm(ltri, hi) + _mm(ltri, mid) + _mm(ltri, lo)


def _rwkv_kernel(p_ref, mu_ref, w0_ref, wup_ref, a0_ref, aup_ref, gup_ref, kkw_ref, ka_ref,
                 rk_ref, lng_ref, lnb_ref, ltri_ref, bd_ref, y_ref, carry, s_ref):
    @pl.when(pl.program_id(1) == 0)
    def _():
        carry[...] = jnp.zeros_like(carry)
        s_ref[...] = jnp.zeros_like(s_ref)

    C, N = RWKV_CHUNK, RWKV_HEAD_DIM
    GW = RWKV_LANE_GROUP * N
    n_groups = RWKV_W // GW
    bd = bd_ref[...]

    def head_sums(x):
        return jnp.concatenate([_mm(x[:, j * GW:(j + 1) * GW].astype(BF16), bd)
                                for j in range(n_groups)], axis=1)

    p = p_ref[0]
    ts = p.shape[0]
    G = ts // C
    prow = lax.broadcasted_iota(jnp.int32, p.shape, 0)
    prev = jnp.where(prow == 0, carry[SUBLANES - 1:SUBLANES, :], pltpu.roll(p, 1, 0))
    carry[...] = p[ts - SUBLANES:, :]
    pm = p + (prev - p) * mu_ref[...]
    o1, o2, o3 = RWKV_W, 2 * RWKV_W, 3 * RWKV_W

    def exact_zero(x):
        return jnp.minimum(jnp.abs(x[0:1, 0:GW]), 0.0)

    def token_terms(rows, out):
        pr = pm[rows]
        r, k, v = pr[:, :o1], pr[:, o1:o2], pr[:, o2:o3]
        wa = pr[:, o3:o3 + W_LORA + A_LORA]
        gd = pr[:, o3 + W_LORA + A_LORA:]
        out.update(r=r, v=v)

        def decay():
            z = w0_ref[...] + _dot(jnp.tanh(wa), wup_ref[...])
            out["lw"] = -math.exp(-0.5) * _sigmoid(z)
            return exact_zero(out["lw"])

        def rates():
            out["a"] = _sigmoid(a0_ref[...] + _dot(wa, aup_ref[...]))
            out["gate"] = _dot(_sigmoid(gd), gup_ref[...])
            return exact_zero(out["a"] + out["gate"])

        def unit_keys():
            kk = k * kkw_ref[...]
            out["kk"] = kk / jnp.maximum(jnp.sqrt(head_sums(kk * kk)), 1e-12)
            return exact_zero(out["kk"])

        def keys():
            out["k2"] = k * (1.0 + (out["a"] - 1.0) * ka_ref[...])
            out["bonus"] = head_sums(r * out["k2"] * rk_ref[...]) * v
            out["kka"] = out["kk"] * out["a"]
            return exact_zero(out["bonus"] + out["kka"])

        return [decay, rates, unit_keys, keys]

    lane_head = lax.broadcasted_iota(jnp.int32, (C, GW), 1) // N
    head_sel = [lane_head == h for h in range(RWKV_LANE_GROUP)]

    def bdiag(x):
        zero = jnp.zeros_like(x)
        return jnp.concatenate([jnp.where(sel, x, zero) for sel in head_sel], axis=0)

    trow = lax.broadcasted_iota(jnp.int32, (C, GW), 0)
    tcol = lax.broadcasted_iota(jnp.int32, (C, GW), 1) % C
    strict = tcol < trow
    incl = tcol <= trow
    eye = (tcol == trow).astype(F32)
    vrow = lax.broadcasted_iota(jnp.int32, (GW, GW), 0) // N
    vcol = lax.broadcasted_iota(jnp.int32, (GW, GW), 1) // N
    same_head = vrow == vcol
    contract0 = (((0,), (0,)), ((), ()))
    contract1 = (((1,), (1,)), ((), ()))

    def local_terms(chunk_ids, tok, background):
        pending = list(background)
        items = []
        for c in chunk_ids:
            rows = slice((c - chunk_ids[0]) * C, (c - chunk_ids[0] + 1) * C)
            lw = tok["lw"][rows]
            cum = _cumsum_rows(ltri_ref[...], lw)
            ge = jnp.exp(cum)
            gi = jnp.exp(-cum)
            At_all = (-tok["kk"][rows] * jnp.exp(cum - lw)).astype(BF16)
            Bt_all = (tok["kka"][rows] * gi).astype(BF16)
            Kt_all = (tok["k2"][rows] * gi).astype(BF16)
            Rt_all = (tok["r"][rows] * ge).astype(BF16)
            V_all = tok["v"][rows].astype(BF16)
            for j in range(n_groups):
                sl = slice(j * GW, (j + 1) * GW)
                items.append(dict(At=At_all[:, sl], Bt=Bt_all[:, sl], Kt=Kt_all[:, sl],
                                  Rt=Rt_all[:, sl], V=V_all[:, sl], g_row=ge[C - 1:C, sl]))

        for it in items:
            AR = jnp.concatenate([it["At"], it["Rt"]], axis=0)
            BK = jnp.concatenate([bdiag(it["Bt"]), bdiag(it["Kt"])], axis=0)
            sc = lax.dot_general(AR, BK, contract1, preferred_element_type=F32)
            it["L"] = jnp.where(strict, sc[:C, :GW], 0.0)
            it["akm"] = jnp.where(strict, sc[:C, GW:], 0.0).astype(BF16)
            it["rbm"] = jnp.where(incl, sc[C:, :GW], 0.0).astype(BF16)
            it["rkm"] = jnp.where(incl, sc[C:, GW:], 0.0).astype(BF16)

        blk = 2
        for it in items:
            it["T"] = eye + jnp.where(trow // blk == tcol // blk, it["L"], 0.0)
            it["Ld"] = bdiag(it["L"].astype(BF16))
        while blk < C:
            lvl = (trow // (2 * blk) == tcol // (2 * blk)) & (trow // blk != tcol // blk)
            if pending:
                items[0]["T"] = items[0]["T"] + pending.pop(0)()
            for it in items:
                it["Tb"] = it["T"].astype(BF16)
                it["P"] = jnp.where(lvl, _mm(it["Tb"], it["Ld"]), 0.0).astype(BF16)
            for it in items:
                it["T"] = it["T"] + _mm(it["P"], bdiag(it["Tb"]))
            blk *= 2

        for it in items:
            it["Vd"] = bdiag(it["V"])
            it["akv"] = _mm(it["akm"], it["Vd"]).astype(BF16)
        for it in items:
            au = _mm(it["T"].astype(BF16),
                     jnp.concatenate([bdiag(it["At"]), bdiag(it["akv"])], axis=1))
            it["A2"] = au[:, :GW].astype(BF16)
            it["U0"] = au[:, GW:].astype(BF16)
        for it in items:
            ry = _mm(it["rbm"], jnp.concatenate([bdiag(it["A2"]), bdiag(it["U0"])], axis=1))
            it["R2"] = (it["Rt"].astype(F32) + ry[:, :GW]).astype(BF16)
            it["Y0"] = ry[:, GW:] + _mm(it["rkm"], it["Vd"])
            it["Mq"] = (jnp.where(same_head, lax.dot_general(it["A2"], it["Bt"], contract0,
                                                             preferred_element_type=F32), 0.0)
                        * it["g_row"]).astype(BF16)
            it["Nq"] = jnp.where(same_head,
                                 lax.dot_general(jnp.concatenate([it["U0"], it["V"]], axis=0),
                                                 jnp.concatenate([it["Bt"], it["Kt"]], axis=0),
                                                 contract0, preferred_element_type=F32),
                                 0.0) * it["g_row"]
        for step in pending:
            step()
        return [{key: it[key] for key in ("R2", "Y0", "Mq", "Nq", "g_row")} for it in items]

    waves = [range(w0, min(w0 + RWKV_WAVE, G)) for w0 in range(0, G, RWKV_WAVE)]
    toks = [dict() for _ in waves]
    steps = [token_terms(slice(w[0] * C, (w[-1] + 1) * C), tok) for w, tok in zip(waves, toks)]
    for step in steps[0]:
        step()
    items = []
    for i, w in enumerate(waves):
        items += local_terms(w, toks[i], steps[i + 1] if i + 1 < len(waves) else [])
    gate = jnp.concatenate([tok["gate"] for tok in toks], axis=0)
    bonus = jnp.concatenate([tok["bonus"] for tok in toks], axis=0)

    y_chunks = []
    for c in range(G):
        ys = []
        for j in range(n_groups):
            it = items[c * n_groups + j]
            S0 = s_ref[j]
            Sb = S0.astype(BF16)
            ys.append(it["Y0"] + lax.dot_general(it["R2"], Sb, contract1,
                                                 preferred_element_type=F32))
            s_ref[j] = S0 * it["g_row"] + _mm(Sb, it["Mq"]) + it["Nq"]
        y_chunks.append(jnp.concatenate(ys, axis=1))
    Y = jnp.concatenate(y_chunks, axis=0)

    inv_n = 1.0 / N
    mean = head_sums(Y) * inv_n
    yc = Y - mean
    var = head_sums(yc * yc) * inv_n
    yn = yc * lax.rsqrt(var + RWKV_LN_EPS) * lng_ref[...] + lnb_ref[...]
    y_ref[0] = ((yn + bonus) * gate).astype(y_ref.dtype)


def _rwkv(p3, mu, w0, wup_pad, a0, aup_pad, gup, k_k, k_a, r_k, ln_g, ln_b, l):
    B, S, _ = p3.shape
    C = RWKV_CHUNK
    ts = min(RWKV_GROUP * C, S)
    group_w = RWKV_LANE_GROUP * RWKV_HEAD_DIM
    ltri = (jnp.arange(C)[None, :] <= jnp.arange(C)[:, None]).astype(BF16)
    head_of_lane = jnp.arange(group_w) // RWKV_HEAD_DIM
    bd = (head_of_lane[:, None] == head_of_lane[None, :]).astype(BF16)
    lay = lambda arr: _layer(arr, l)
    return pl.pallas_call(
        _rwkv_kernel,
        grid=(B, S // ts),
        in_specs=[pl.BlockSpec((1, ts, RWKV_IN), lambda b, s: (b, s, 0)), lay(mu), lay(w0),
                  lay(wup_pad), lay(a0), lay(aup_pad), lay(gup), lay(k_k), lay(k_a), lay(r_k),
                  lay(ln_g), lay(ln_b), _full((C, C)), _full((group_w, group_w))],
        out_specs=pl.BlockSpec((1, ts, RWKV_W), lambda b, s: (b, s, 0)),
        out_shape=jax.ShapeDtypeStruct((B, S, RWKV_W), BF16),
        scratch_shapes=[pltpu.VMEM((SUBLANES, RWKV_IN), F32),
                        pltpu.VMEM((RWKV_W // group_w, group_w, group_w), F32)],
        compiler_params=_params("parallel", "arbitrary"),
        name="rwkv",
    )(p3, mu, w0, wup_pad, a0, aup_pad, gup, k_k, k_a, r_k, ln_g, ln_b, ltri, bd)


def _shift_rows(x, d, fill):
    n = x.shape[0]
    if d % SUBLANES == 0:
        return jnp.concatenate([jnp.full((d,) + x.shape[1:], fill, x.dtype), x[:n - d]], axis=0)
    rows = lax.broadcasted_iota(jnp.int32, x.shape, 0)
    return jnp.where(rows < d, fill, pltpu.roll(x, d, 0))


def _scan_block(a, u):
    d = 1
    while d < a.shape[0]:
        u = u + a * _shift_rows(u, d, 0.0)
        a = a * _shift_rows(a, d, 1.0)
        d *= 2
    return a, u


def _lru_tile(p, cw_ref, cb_ref, wa_ref, ba_ref, wx_ref, bx_ref, lam_ref, xpad, hcarry, first,
              background=()):
    @pl.when(first)
    def _():
        xpad[0:SUBLANES, :] = jnp.zeros((SUBLANES, LRU_W), F32)
        hcarry[...] = jnp.zeros_like(hcarry)

    ts = p.shape[0]
    xb, gb = p[:, :LRU_W], p[:, LRU_W:]
    xpad[SUBLANES:, :] = xb
    cw = cw_ref[...]
    xc = cb_ref[...] + xb * cw[CONV_WIDTH - 1:CONV_WIDTH, :]
    for j in range(CONV_WIDTH - 1):
        lo = SUBLANES - (CONV_WIDTH - 1) + j
        xc = xc + xpad[lo:lo + ts, :] * cw[j:j + 1, :]
    xpad[0:SUBLANES, :] = xb[ts - SUBLANES:, :]
    rg = _sigmoid(_dot(xc, wa_ref[...]) + ba_ref[...])
    ig = _sigmoid(_dot(xc, wx_ref[...]) + bx_ref[...])
    lam = lam_ref[...]
    softplus_neg_lam = jnp.maximum(-lam, 0.0) + jnp.log(1.0 + jnp.exp(-jnp.abs(lam)))
    log_a = -LRU_C * rg * softplus_neg_lam
    a = jnp.exp(log_a)
    u = jnp.sqrt(1.0 - a * a) * (ig * xc)
    pending = list(background)
    n_row_blocks = ts // LRU_SCAN_ROWS
    stride = max(1, (LRU_W // LANES) * n_row_blocks // max(1, len(pending)))
    strips = []
    for ls in range(LRU_W // LANES):
        cols = slice(ls * LANES, (ls + 1) * LANES)
        h_in = hcarry[SUBLANES - 1:SUBLANES, cols]
        blocks = []
        for rt in range(n_row_blocks):
            if pending and (ls * n_row_blocks + rt) % stride == 0:
                h_in = h_in + pending.pop(0)()
            rs = slice(rt * LRU_SCAN_ROWS, (rt + 1) * LRU_SCAN_ROWS)
            a_blk, u_blk = _scan_block(a[rs, cols], u[rs, cols])
            h_blk = u_blk + a_blk * h_in
            h_in = h_blk[LRU_SCAN_ROWS - 1:, :]
            blocks.append(h_blk)
        strips.append(jnp.concatenate(blocks, axis=0))
    for thunk in pending:
        thunk()
    h = jnp.concatenate(strips, axis=1)
    hcarry[...] = h[ts - SUBLANES:, :]
    gelu = 0.5 * gb * (1.0 + jnp.tanh(math.sqrt(2.0 / math.pi) * (gb + 0.044715 * gb * gb * gb)))
    return h * gelu


def _mla_project(p, qn_ref, wuq_ref, kvn_ref, wuk_ref, wuv_ref, vt_o):
    cq = p[:, :Q_RANK]
    ckv = p[:, Q_RANK:Q_RANK + KV_RANK]
    qq = _dot(_rms(cq, qn_ref[...]), wuq_ref[...])
    ckv_n = _rms(ckv, kvn_ref[...]).astype(BF16)
    kn = _mm(ckv_n, wuk_ref[...])
    one_lane = (lax.broadcasted_iota(jnp.int32, ckv_n.shape, 1) == 0).astype(BF16)
    vt_o[0] = _dot_nt(wuv_ref[...], jnp.concatenate([ckv_n, one_lane], axis=1)).astype(BF16)
    return qq, kn


def _mla_rotate(p, qq, kn, cos_ref, sin_ref, qg_ref, qgs_ref, kg_ref, kgs_ref, q_o, k_o,
                background=()):
    W = MLA_HEADS * MLA_HEAD_PAD
    kr = p[:, Q_RANK + KV_RANK:Q_RANK + KV_RANK + LANES]
    kr_sw = p[:, Q_RANK + KV_RANK + LANES:]
    cosf = cos_ref[...]
    sinf = sin_ref[...]
    scale = MLA_QK ** -0.5 * math.log2(math.e)
    cq_tab = cosf * (qg_ref[...] * scale)
    sq_tab = sinf * (qgs_ref[...] * scale)
    ck_tab = cosf * kg_ref[...]
    sk_tab = sinf * kgs_ref[...]
    kr_rot = kr * ck_tab + kr_sw * sk_tab
    kr_ss = jnp.sum(kr * kr, axis=-1, keepdims=True)
    pending = list(background)
    for h in range(MLA_HEADS):
        sl = slice(h * MLA_HEAD_PAD, (h + 1) * MLA_HEAD_PAD)
        qh = qq[:, sl]
        q_rs = lax.rsqrt(jnp.sum(qh * qh, axis=-1, keepdims=True) / MLA_QK + 1e-6)
        if pending:
            q_rs = q_rs + pending.pop(0)()[:, 0:1]
        q_o[:, sl] = ((qh * cq_tab + qq[:, W + h * MLA_HEAD_PAD:W + (h + 1) * MLA_HEAD_PAD] * sq_tab)
                      * q_rs).astype(BF16)
        kh = kn[:, sl]
        k_rs = lax.rsqrt((jnp.sum(kh * kh, axis=-1, keepdims=True) + kr_ss) / MLA_QK + 1e-6)
        k_o[:, sl] = ((kh * ck_tab + kr_rot) * k_rs).astype(BF16)
    for thunk in pending:
        thunk()


def _mla_attn_kernel(q_ref, k_ref, vt_ref, o_ref, acc_ref, m_ref, l_ref, s_ref, p_ref, a_ref):
    qi = pl.program_id(1)
    tq = q_ref.shape[1]
    m_ref[...] = jnp.full(m_ref.shape, MASKED_SCORE, F32)
    l_ref[...] = jnp.zeros(l_ref.shape, F32)
    acc_ref[...] = jnp.zeros(acc_ref.shape, F32)

    def scores(j, masked, slot):
        start = pl.multiple_of(j * tq, tq)
        ties = []
        for h in range(MLA_HEADS):
            hs = slice(h * MLA_HEAD_PAD, (h + 1) * MLA_HEAD_PAD)
            kb = k_ref[0, pl.ds(start, tq), hs]
            st = lax.dot_general(kb, q_ref[0, :, hs], (((1,), (1,)), ((), ())),
                                 preferred_element_type=F32)
            if masked:
                kpos = lax.broadcasted_iota(jnp.int32, st.shape, 0)
                qpos = lax.broadcasted_iota(jnp.int32, st.shape, 1)
                st = jnp.where(kpos <= qpos, st, MASKED_SCORE)
            s_ref[slot, h] = st.astype(BF16)
            ties.append(jnp.minimum(jnp.abs(st[0:1, :]), 0.0))
        return ties

    def softmax(slot, ties=None):
        for h in range(MLA_HEADS):
            sb = s_ref[slot, h]
            m_old = m_ref[h:h + 1, :]
            if ties is not None:
                m_old = m_old + ties[h]
            m_new = jnp.maximum(m_old, jnp.max(sb, axis=0, keepdims=True).astype(F32))
            p_ref[slot, h] = jnp.exp2(sb - m_new.astype(BF16))
            a_ref[slot, h:h + 1, :] = jnp.exp2(m_old - m_new)
            m_ref[h:h + 1, :] = m_new

    def values(j, slot):
        start = pl.multiple_of(j * tq, tq)
        for h in range(MLA_HEADS):
            vs = slice(h * MLA_V, (h + 1) * MLA_V)
            ve = slice(h * MLA_V_EXT, (h + 1) * MLA_V_EXT)
            pvx = _mm(vt_ref[0, ve, pl.ds(start, tq)], p_ref[slot, h])
            alpha = a_ref[slot, h:h + 1, :]
            l_ref[h:h + 1, :] = alpha * l_ref[h:h + 1, :] + pvx[MLA_V:MLA_V + 1, :]
            acc_ref[vs, :] = alpha * acc_ref[vs, :] + pvx[:MLA_V, :]

    def tile(j, masked, slot):
        scores(j, masked, slot)
        softmax(slot)
        values(j, slot)

    def body(i, carry):
        scores(2 * i, False, 0)
        ties = scores(2 * i + 1, False, 1)
        softmax(0, ties)
        values(2 * i, 0)
        softmax(1)
        values(2 * i + 1, 1)
        return carry

    lax.fori_loop(0, qi // 2, body, 0)

    @pl.when(qi % 2 == 1)
    def _():
        tile(qi - 1, False, 0)

    tile(qi, True, 1)
    for h in range(MLA_HEADS):
        vs = slice(h * MLA_V, (h + 1) * MLA_V)
        acc_ref[vs, :] = acc_ref[vs, :] / l_ref[h:h + 1, :]
    o_ref[0] = acc_ref[...].T.astype(o_ref.dtype)


def _mla_attn(q, k, vt):
    B, S, W = q.shape
    tq = min(ATT_TILE, S)
    WV = MLA_HEADS * MLA_V
    return pl.pallas_call(
        _mla_attn_kernel,
        grid=(B, S // tq),
        in_specs=[pl.BlockSpec((1, tq, W), lambda b, i: (b, i, 0)),
                  pl.BlockSpec((1, S, W), lambda b, i: (b, 0, 0)),
                  pl.BlockSpec((1, MLA_HEADS * MLA_V_EXT, S), lambda b, i: (b, 0, 0))],
        out_specs=pl.BlockSpec((1, tq, WV), lambda b, i: (b, i, 0)),
        out_shape=jax.ShapeDtypeStruct((B, S, WV), BF16),
        scratch_shapes=[pltpu.VMEM((WV, tq), F32), pltpu.VMEM((MLA_HEADS, tq), F32),
                        pltpu.VMEM((MLA_HEADS, tq), F32), pltpu.VMEM((2, MLA_HEADS, tq, tq), BF16),
                        pltpu.VMEM((2, MLA_HEADS, tq, tq), BF16),
                        pltpu.VMEM((2, MLA_HEADS, tq), F32)],
        compiler_params=_params("parallel", "arbitrary"),
        name="mla_attn",
    )(q, k, vt)


def _merge_kernel(x_ref, ya_ref, yb_ref, yc_ref, g_ref, wg_ref, bg_ref, wb_ref, wo_ref, o_ref):
    x = x_ref[...]
    h = _rms(x, g_ref[...]).astype(BF16)
    merged = None
    for n, y_ref in enumerate((ya_ref, yb_ref, yc_ref)):
        sl = slice(n * D_MODEL, (n + 1) * D_MODEL)
        gate = _sigmoid(_mm(h, wg_ref[:, sl]) + bg_ref[:, sl])
        term = gate * _mm(y_ref[...], wb_ref[n])
        merged = term if merged is None else merged + term
    o_ref[...] = x + _dot(merged, wo_ref[...])


def _merge(x2, ya, yb, yc, g, w_gate, b_gate, w_branch, w_out, l):
    T = x2.shape[0]
    tm = min(TOK_TILE, T)
    lay = lambda arr: _layer(arr, l)
    return pl.pallas_call(
        _merge_kernel,
        grid=(T // tm,),
        in_specs=[_rows(tm, D_MODEL), _rows(tm, BRANCH_W), _rows(tm, BRANCH_W), _rows(tm, BRANCH_W),
                  lay(g), lay(w_gate), lay(b_gate), lay(w_branch), lay(w_out)],
        out_specs=_rows(tm, D_MODEL),
        out_shape=jax.ShapeDtypeStruct((T, D_MODEL), F32),
        compiler_params=_params("parallel"),
        name="merge",
    )(x2, ya, yb, yc, g, w_gate, b_gate, w_branch, w_out)


def _mem_kv_kernel(m_ref, g_ref, wkv_ref, kg_ref, k_o, v_o):
    h = _rms(m_ref[...], g_ref[...]).astype(BF16)
    kv = _mm(h, wkv_ref[...])
    for hd in range(XA_HEADS):
        sl = slice(hd * XA_HEAD_DIM, (hd + 1) * XA_HEAD_DIM)
        base = 2 * hd * XA_HEAD_DIM
        k_o[:, sl] = _rms(kv[:, base:base + XA_HEAD_DIM], kg_ref[...]).astype(BF16)
        v_o[:, sl] = kv[:, base + XA_HEAD_DIM:base + 2 * XA_HEAD_DIM].astype(BF16)


def _mem_kv(mem2, g, wkv, k_gain, l):
    M = mem2.shape[0]
    tm = min(TOK_TILE, M)
    lay = lambda arr: _layer(arr, l)
    return pl.pallas_call(
        _mem_kv_kernel,
        grid=(M // tm,),
        in_specs=[_rows(tm, D_MODEL), lay(g), lay(wkv), lay(k_gain)],
        out_specs=[_rows(tm, XA_W), _rows(tm, XA_W)],
        out_shape=[jax.ShapeDtypeStruct((M, XA_W), BF16)] * 2,
        compiler_params=_params("parallel"),
        name="mem_kv",
    )(mem2, g, wkv, k_gain)


def _xattn_kernel(x_ref, k_ref, v_ref, g_ref, wq_ref, qg_ref, wo_ref, o_ref):
    x = x_ref[0]
    h = _rms(x, g_ref[...]).astype(BF16)
    q = _mm(h, wq_ref[...])
    scale = XA_HEAD_DIM ** -0.5
    outs = []
    for hd in range(XA_HEADS):
        sl = slice(hd * XA_HEAD_DIM, (hd + 1) * XA_HEAD_DIM)
        qh = _rms(q[:, sl], qg_ref[...]) * scale
        s = _dot_nt(qh, k_ref[0, :, sl])
        pr = jnp.exp(s - jnp.max(s, axis=-1, keepdims=True))
        pr = pr / jnp.sum(pr, axis=-1, keepdims=True)
        outs.append(_dot(pr, v_ref[0, :, sl]))
    o = jnp.concatenate(outs, axis=-1)
    o_ref[0] = x + _dot(o, wo_ref[...])


def _xattn(x3, k3, v3, g, wq, q_gain, wo, l):
    B, S, _ = x3.shape
    ts = min(TOK_TILE, S)
    M = k3.shape[1]
    lay = lambda arr: _layer(arr, l)
    return pl.pallas_call(
        _xattn_kernel,
        grid=(B, S // ts),
        in_specs=[pl.BlockSpec((1, ts, D_MODEL), lambda b, s: (b, s, 0)),
                  pl.BlockSpec((1, M, XA_W), lambda b, s: (b, 0, 0)),
                  pl.BlockSpec((1, M, XA_W), lambda b, s: (b, 0, 0)),
                  lay(g), lay(wq), lay(q_gain), lay(wo)],
        out_specs=pl.BlockSpec((1, ts, D_MODEL), lambda b, s: (b, s, 0)),
        out_shape=jax.ShapeDtypeStruct((B, S, D_MODEL), F32),
        compiler_params=_params("parallel", "parallel"),
        name="xattn",
    )(x3, k3, v3, g, wq, q_gain, wo)


FF_SPLIT = 2


def _ffn_kernel(x_ref, g_ref, w1_ref, w3_ref, w2_ref, o_ref):
    x = x_ref[...]
    h = _rms(x, g_ref[...]).astype(BF16)
    step = D_FF // FF_SPLIT
    acc = x
    for c in range(FF_SPLIT):
        sl = slice(c * step, (c + 1) * step)
        a = _mm(h, w1_ref[:, sl])
        b = _mm(h, w3_ref[:, sl])
        z = a * _sigmoid(a) * b
        acc = acc + _dot(z, w2_ref[sl, :])
    o_ref[...] = acc


def _ffn(x2, g, w1, w3, w2, l):
    T = x2.shape[0]
    tm = min(TOK_TILE, T)
    lay = lambda arr: _layer(arr, l)
    return pl.pallas_call(
        _ffn_kernel,
        grid=(T // tm,),
        in_specs=[_rows(tm, D_MODEL), lay(g), lay(w1), lay(w3), lay(w2)],
        out_specs=_rows(tm, D_MODEL),
        out_shape=jax.ShapeDtypeStruct((T, D_MODEL), F32),
        compiler_params=_params("parallel"),
        name="ffn",
    )(x2, g, w1, w3, w2)


def _block_diag(w):
    L, n, i, j = w.shape
    eye = jnp.eye(n, dtype=w.dtype)
    return jnp.einsum("lnij,nm->lnimj", w, eye).reshape(L, n * i, n * j)


def _pad_heads(w, heads, width, pad_to):
    lead = w.shape[:-1]
    w = w.reshape(lead + (heads, width))
    w = jnp.pad(w, [(0, 0)] * len(lead) + [(0, 0), (0, pad_to - width)])
    return w.reshape(lead + (heads * pad_to,))


def _rope_partner(w):
    half = MLA_ROPE // 2
    return jnp.concatenate([jnp.zeros_like(w[..., :MLA_NOPE]), w[..., MLA_NOPE + half:],
                            w[..., MLA_NOPE:MLA_NOPE + half]], axis=-1)


def _vec(v):
    return v.reshape(v.shape[0], 1, -1).astype(F32)


def kernel(x, mem, positions, norm_mix, norm_xattn, norm_mem, norm_ffn, w_in, b_gate, rwkv_mu, rwkv_w0, rwkv_w_up, rwkv_a0, rwkv_a_up, rwkv_g_up, rwkv_k_k, rwkv_k_a, rwkv_r_k, rwkv_ln_g, rwkv_ln_b, lru_conv_w, lru_conv_b, lru_wa, lru_ba, lru_wx, lru_bx, lru_lambda, mla_q_norm, mla_w_uq, mla_kv_norm, mla_w_ukv, mla_q_gain, mla_k_gain, w_branch, w_out, xa_w_q, xa_w_kv, xa_q_gain, xa_k_gain, xa_w_o, ffn_w1, ffn_w3, ffn_w2):
    B, S, D = x.shape
    T = B * S
    depth = w_in.shape[0]
    x2 = x.reshape(T, D)
    mem2 = mem.reshape(B * N_MEM, D)
    half = MLA_ROPE // 2

    inv_freq = ROPE_THETA ** (-jnp.arange(0, MLA_ROPE, 2, dtype=F32) / MLA_ROPE)
    per_row = LANES // half
    pos_rep = jnp.repeat(positions.reshape(T // per_row, per_row), half, axis=1)
    cos_c, sin_c, nsin_c = _rope_tables(pos_rep, jnp.tile(inv_freq, per_row).reshape(1, LANES))
    cos_c, sin_c, nsin_c = (t.reshape(T, half) for t in (cos_c, sin_c, nsin_c))
    ones = jnp.ones((T, MLA_NOPE), F32)
    zeros = jnp.zeros((T, MLA_NOPE), F32)
    cosf = jnp.concatenate([ones, cos_c, cos_c, ones[:, :LANES - MLA_QK]], axis=1)
    sinf = jnp.concatenate([zeros, nsin_c, sin_c, zeros[:, :LANES - MLA_QK]], axis=1)

    w_in_b = w_in[:, :, :MLA_OFF].astype(BF16)
    w_kr = w_in[:, :, MLA_OFF + Q_RANK + KV_RANK:GATE_OFF].astype(BF16)
    zero_cols = lambda n: jnp.zeros((depth, D, n), BF16)
    w_mla = jnp.concatenate(
        [w_in[:, :, MLA_OFF:MLA_OFF + Q_RANK + KV_RANK].astype(BF16), zero_cols(MLA_NOPE), w_kr,
         zero_cols(LANES - MLA_QK), zero_cols(MLA_NOPE), w_kr[:, :, half:], w_kr[:, :, :half],
         zero_cols(LANES - MLA_QK)], axis=2)
    w_gate = w_in[:, :, GATE_OFF:].astype(BF16)

    zeros_lora = jnp.zeros((depth, W_LORA, RWKV_W), F32)
    wup_pad = jnp.concatenate([rwkv_w_up, zeros_lora], axis=1).astype(BF16)
    aup_pad = jnp.concatenate([zeros_lora, rwkv_a_up], axis=1).astype(BF16)
    gup = rwkv_g_up.astype(BF16)
    r_k = rwkv_r_k.reshape(depth, RWKV_W)

    wa_bd = _block_diag(lru_wa).astype(BF16)
    wx_bd = _block_diag(lru_wx).astype(BF16)

    wuq_h = mla_w_uq.reshape(depth, Q_RANK, MLA_HEADS, MLA_QK)
    wuq2 = jnp.concatenate(
        [_pad_heads(mla_w_uq, MLA_HEADS, MLA_QK, MLA_HEAD_PAD),
         _pad_heads(_rope_partner(wuq_h).reshape(depth, Q_RANK, -1), MLA_HEADS, MLA_QK, MLA_HEAD_PAD)],
        axis=2).astype(BF16)
    wukv = mla_w_ukv.reshape(depth, KV_RANK, MLA_HEADS, MLA_NOPE + MLA_V)
    wuk = _pad_heads(wukv[..., :MLA_NOPE].reshape(depth, KV_RANK, -1), MLA_HEADS, MLA_NOPE,
                     MLA_HEAD_PAD).astype(BF16)
    wuvt = jnp.swapaxes(wukv[..., MLA_NOPE:], 1, 2).transpose(0, 1, 3, 2)
    wuvt = jnp.pad(wuvt, ((0, 0), (0, 0), (0, MLA_V_EXT - MLA_V), (0, KV_RANK)))
    wuvt = wuvt.at[:, :, MLA_V, KV_RANK].set(1.0)
    wuvt = wuvt.reshape(depth, MLA_HEADS * MLA_V_EXT, 2 * KV_RANK).astype(BF16)
    pad_gain = lambda gv: _vec(jnp.pad(gv, ((0, 0), (0, LANES - MLA_QK))))
    qg, qgs = pad_gain(mla_q_gain), pad_gain(_rope_partner(mla_q_gain))
    kg, kgs = pad_gain(mla_k_gain), pad_gain(_rope_partner(mla_k_gain))

    w_branch_b = w_branch.astype(BF16)
    w_out_b = w_out.astype(BF16)
    xa_wkv = xa_w_kv.astype(BF16)
    xa_wq = xa_w_q.astype(BF16)
    xa_wo = xa_w_o.astype(BF16)
    w1, w3, w2 = ffn_w1.astype(BF16), ffn_w3.astype(BF16), ffn_w2.astype(BF16)

    n_mix, n_xa, n_mem, n_ffn = _vec(norm_mix), _vec(norm_xattn), _vec(norm_mem), _vec(norm_ffn)
    rw = [_vec(t) for t in (rwkv_mu, rwkv_w0, rwkv_a0, rwkv_k_k, rwkv_k_a, r_k, rwkv_ln_g, rwkv_ln_b)]
    mu, w0, a0, k_k, k_a, r_kv, ln_g, ln_b = rw
    conv_b, ba, bx, lam = _vec(lru_conv_b), _vec(lru_ba), _vec(lru_bx), _vec(lru_lambda)
    q_norm, kv_norm = _vec(mla_q_norm), _vec(mla_kv_norm)
    b_gate_v, xa_qg, xa_kg = _vec(b_gate), _vec(xa_q_gain), _vec(xa_k_gain)

    lru_w = (lru_conv_w, conv_b, wa_bd, ba, wx_bd, bx, lam)
    mla_w = (q_norm, wuq2, kv_norm, wuk, wuvt, qg, qgs, kg, kgs)
    W = MLA_HEADS * MLA_HEAD_PAD
    for l in range(depth):
        p_rwkv, y_b, q, k, vt = _in_proj(x2, n_mix, w_in_b, w_mla, lru_w, mla_w, cosf, sinf, B, S, l)

        y_a = _rwkv(p_rwkv.reshape(B, S, RWKV_IN), mu, w0, wup_pad, a0, aup_pad, gup, k_k, k_a,
                    r_kv, ln_g, ln_b, l).reshape(T, RWKV_W)
        y_c = _mla_attn(q.reshape(B, S, W), k.reshape(B, S, W), vt).reshape(T, MLA_HEADS * MLA_V)

        x2 = _merge(x2, y_a, y_b, y_c, n_mix, w_gate, b_gate_v, w_branch_b, w_out_b, l)

        mk, mv = _mem_kv(mem2, n_mem, xa_wkv, xa_kg, l)
        x2 = _xattn(x2.reshape(B, S, D), mk.reshape(B, N_MEM, XA_W), mv.reshape(B, N_MEM, XA_W),
                    n_xa, xa_wq, xa_qg, xa_wo, l).reshape(T, D)

        x2 = _ffn(x2, n_ffn, w1, w3, w2, l)
    return x2.reshape(B, S, D)
```

```python
import functools
import math

import jax
import jax.numpy as jnp
from jax import lax
from jax.experimental import pallas as pl
from jax.experimental.pallas import tpu as pltpu

F32 = jnp.float32
BF16 = jnp.bfloat16

D_MODEL = 1024
N_MEM = 256
RWKV_HEADS = 8
RWKV_HEAD_DIM = 64
RWKV_W = RWKV_HEADS * RWKV_HEAD_DIM
W_LORA = 64
A_LORA = 64
G_LORA = 128
RWKV_IN = 3 * RWKV_W + W_LORA + A_LORA + G_LORA
RWKV_LN_EPS = RWKV_HEAD_DIM * 1e-5
LRU_BLOCKS = 8
LRU_W = 512
CONV_WIDTH = 4
LRU_C = 8.0
MLA_HEADS = 8
MLA_NOPE = 64
MLA_ROPE = 32
MLA_QK = MLA_NOPE + MLA_ROPE
MLA_V = 64
Q_RANK = 256
KV_RANK = 128
ROPE_THETA = 10000.0
N_BRANCH = 3
BRANCH_W = 512
XA_HEADS = 4
XA_HEAD_DIM = 128
XA_W = XA_HEADS * XA_HEAD_DIM
D_FF = -(-8 * D_MODEL // (3 * 256)) * 256
LRU_OFF = RWKV_IN
MLA_OFF = LRU_OFF + 2 * LRU_W
GATE_OFF = MLA_OFF + Q_RANK + KV_RANK + MLA_ROPE

LANES = 128
SUBLANES = 8
BF16_SUBLANES = 16
MXU_TILE = 256
VMEM_LIMIT = 56 * 1024 * 1024
MLA_HEAD_PAD = LANES
MLA_V_EXT = MLA_V + BF16_SUBLANES
MLA_PAD = Q_RANK + KV_RANK + 2 * LANES
MASKED_SCORE = -2.0 ** 100
RWKV_CHUNK = 64
RWKV_GROUP = 8
RWKV_WAVE = 4
RWKV_LANE_GROUP = MXU_TILE // RWKV_HEAD_DIM
LRU_SCAN_ROWS = 64
TOK_TILE = 512
ATT_TILE = 256


def _params(*sem):
    return pltpu.CompilerParams(dimension_semantics=sem, vmem_limit_bytes=VMEM_LIMIT)


def _mm(a, b):
    return jnp.dot(a, b, preferred_element_type=F32)


def _dot(a, b):
    return _mm(a.astype(BF16), b.astype(BF16))


def _dot_nt(a, b):
    return lax.dot_general(a.astype(BF16), b.astype(BF16), (((1,), (1,)), ((), ())),
                           preferred_element_type=F32)


def _sigmoid(x):
    return 1.0 / (1.0 + jnp.exp(-x))


def _rms(x, g, eps=1e-6):
    return x * lax.rsqrt(jnp.mean(x * x, axis=-1, keepdims=True) + eps) * g


def _full(shape):
    n = len(shape)
    return pl.BlockSpec(shape, lambda *_: (0,) * n)


def _layer(arr, l):
    tail = arr.shape[1:]
    return pl.BlockSpec((None,) + tail, lambda *_: (l,) + (0,) * len(tail))


def _rows(tm, n):
    return pl.BlockSpec((tm, n), lambda i: (i, 0))


def _rope_kernel(pos_ref, freq_ref, cos_o, sin_o, nsin_o):
    ang = pos_ref[...].astype(F32) * freq_ref[...]
    s = jnp.sin(ang)
    cos_o[...] = jnp.cos(ang)
    sin_o[...] = s
    nsin_o[...] = -s


def _rope_tables(pos_rep, freq_tile):
    R = pos_rep.shape[0]
    tm = min(TOK_TILE, R)
    out = jax.ShapeDtypeStruct((R, LANES), F32)
    return pl.pallas_call(
        _rope_kernel,
        grid=(R // tm,),
        in_specs=[_rows(tm, LANES), _full((1, LANES))],
        out_specs=[_rows(tm, LANES)] * 3,
        out_shape=[out] * 3,
        compiler_params=_params("parallel"),
        name="rope_tables",
    )(pos_rep, freq_tile)


def _in_proj_kernel(x_ref, g_ref, wrl_ref, wm_ref,
                    cw_ref, cb_ref, wa_ref, ba_ref, wx_ref, bx_ref, lam_ref,
                    cos_ref, sin_ref, qn_ref, wuq_ref, kvn_ref, wuk_ref, wuv_ref,
                    qg_ref, qgs_ref, kg_ref, kgs_ref,
                    or_ref, yb_ref, q_o, k_o, vt_o, xpad, hcarry, *, tiles_per_row):
    first = pl.program_id(0) % tiles_per_row == 0
    h = _rms(x_ref[...], g_ref[...]).astype(BF16)
    p_lru = _mm(h, wrl_ref[:, RWKV_IN:])

    def rwkv_cols(c0):
        def run():
            res = _mm(h, wrl_ref[:, c0:c0 + MXU_TILE])
            or_ref[:, c0:c0 + MXU_TILE] = res
            return jnp.minimum(jnp.abs(res[0:1, 0:LANES]), 0.0)
        return run

    mla = {}

    def mla_dots():
        p_mla = _mm(h, wm_ref[...])
        qq, kn = _mla_project(p_mla, qn_ref, wuq_ref, kvn_ref, wuk_ref, wuv_ref, vt_o)
        mla.update(p=p_mla, qq=qq, kn=kn)
        return jnp.minimum(jnp.abs(qq[0:1, 0:LANES] + kn[0:1, 0:LANES]), 0.0)

    rwkv_dots = [rwkv_cols(c0) for c0 in range(0, RWKV_IN, MXU_TILE)]
    n_lru = len(rwkv_dots)
    yb_ref[...] = _lru_tile(p_lru, cw_ref, cb_ref, wa_ref, ba_ref, wx_ref, bx_ref, lam_ref,
                            xpad, hcarry, first,
                            background=[mla_dots] + rwkv_dots[:n_lru]).astype(yb_ref.dtype)
    _mla_rotate(mla["p"], mla["qq"], mla["kn"], cos_ref, sin_ref, qg_ref, qgs_ref, kg_ref, kgs_ref,
                q_o, k_o, background=rwkv_dots[n_lru:])


def _in_proj(x2, g, w_in_b, w_mla, lru_w, mla_w, cosf, sinf, B, S, l):
    T = x2.shape[0]
    tm = min(TOK_TILE, S)
    per_row = S // tm
    W = MLA_HEADS * MLA_HEAD_PAD
    lay = lambda arr: _layer(arr, l)
    return pl.pallas_call(
        functools.partial(_in_proj_kernel, tiles_per_row=per_row),
        grid=(T // tm,),
        in_specs=[_rows(tm, D_MODEL), lay(g), lay(w_in_b), lay(w_mla)] + [lay(w) for w in lru_w]
                 + [_rows(tm, LANES), _rows(tm, LANES)] + [lay(w) for w in mla_w],
        out_specs=[_rows(tm, RWKV_IN), _rows(tm, LRU_W), _rows(tm, W), _rows(tm, W),
                   pl.BlockSpec((1, MLA_HEADS * MLA_V_EXT, tm),
                                lambda i: (i // per_row, 0, i % per_row))],
        out_shape=[jax.ShapeDtypeStruct((T, RWKV_IN), F32),
                   jax.ShapeDtypeStruct((T, LRU_W), BF16),
                   jax.ShapeDtypeStruct((T, W), BF16), jax.ShapeDtypeStruct((T, W), BF16),
                   jax.ShapeDtypeStruct((B, MLA_HEADS * MLA_V_EXT, S), BF16)],
        scratch_shapes=[pltpu.VMEM((tm + SUBLANES, LRU_W), F32), pltpu.VMEM((SUBLANES, LRU_W), F32)],
        compiler_params=_params("arbitrary"),
        name="in_proj",
    )(x2, g, w_in_b, w_mla, *lru_w, cosf, sinf, *mla_w)


def _cumsum_rows(ltri, x):
    hi = x.astype(BF16)
    r1 = x - hi.astype(F32)
    mid = r1.astype(BF16)
    lo = (r1 - mid.astype(F32)).astype(BF16)
    return _mm(ltri, hi) + _mm(ltri, mid) + _mm(ltri, lo)


def _rwkv_kernel(p_ref, mu_ref, w0_ref, wup_ref, a0_ref, aup_ref, gup_ref, kkw_ref, ka_ref,
                 rk_ref, lng_ref, lnb_ref, ltri_ref, bd_ref, y_ref, carry, s_ref):
    @pl.when(pl.program_id(1) == 0)
    def _():
        carry[...] = jnp.zeros_like(carry)
        s_ref[...] = jnp.zeros_like(s_ref)

    C, N = RWKV_CHUNK, RWKV_HEAD_DIM
    GW = RWKV_LANE_GROUP * N
    n_groups = RWKV_W // GW
    bd = bd_ref[...]

    def head_sums(x):
        return jnp.concatenate([_mm(x[:, j * GW:(j + 1) * GW].astype(BF16), bd)
                                for j in range(n_groups)], axis=1)

    p = p_ref[0]
    ts = p.shape[0]
    G = ts // C
    prow = lax.broadcasted_iota(jnp.int32, p.shape, 0)
    prev = jnp.where(prow == 0, carry[SUBLANES - 1:SUBLANES, :], pltpu.roll(p, 1, 0))
    carry[...] = p[ts - SUBLANES:, :]
    pm = p + (prev - p) * mu_ref[...]
    o1, o2, o3 = RWKV_W, 2 * RWKV_W, 3 * RWKV_W

    def exact_zero(x):
        return jnp.minimum(jnp.abs(x[0:1, 0:GW]), 0.0)

    def token_terms(rows, out):
        pr = pm[rows]
        r, k, v = pr[:, :o1], pr[:, o1:o2], pr[:, o2:o3]
        wa = pr[:, o3:o3 + W_LORA + A_LORA]
        gd = pr[:, o3 + W_LORA + A_LORA:]
        out.update(r=r, v=v)

        def decay():
            z = w0_ref[...] + _dot(jnp.tanh(wa), wup_ref[...])
            out["lw"] = -math.exp(-0.5) * _sigmoid(z)
            return exact_zero(out["lw"])

        def rates():
            out["a"] = _sigmoid(a0_ref[...] + _dot(wa, aup_ref[...]))
            out["gate"] = _dot(_sigmoid(gd), gup_ref[...])
            return exact_zero(out["a"] + out["gate"])

        def unit_keys():
            kk = k * kkw_ref[...]
            out["kk"] = kk / jnp.maximum(jnp.sqrt(head_sums(kk * kk)), 1e-12)
            return exact_zero(out["kk"])

        def keys():
            out["k2"] = k * (1.0 + (out["a"] - 1.0) * ka_ref[...])
            out["bonus"] = head_sums(r * out["k2"] * rk_ref[...]) * v
            out["kka"] = out["kk"] * out["a"]
            return exact_zero(out["bonus"] + out["kka"])

        return [decay, rates, unit_keys, keys]

    lane_head = lax.broadcasted_iota(jnp.int32, (C, GW), 1) // N
    head_sel = [lane_head == h for h in range(RWKV_LANE_GROUP)]

    def bdiag(x):
        zero = jnp.zeros_like(x)
        return jnp.concatenate([jnp.where(sel, x, zero) for sel in head_sel], axis=0)

    trow = lax.broadcasted_iota(jnp.int32, (C, GW), 0)
    tcol = lax.broadcasted_iota(jnp.int32, (C, GW), 1) % C
    strict = tcol < trow
    incl = tcol <= trow
    eye = (tcol == trow).astype(F32)
    vrow = lax.broadcasted_iota(jnp.int32, (GW, GW), 0) // N
    vcol = lax.broadcasted_iota(jnp.int32, (GW, GW), 1) // N
    same_head = vrow == vcol
    contract0 = (((0,), (0,)), ((), ()))
    contract1 = (((1,), (1,)), ((), ()))

    def local_terms(chunk_ids, tok, background):
        pending = list(background)
        items = []
        for c in chunk_ids:
            rows = slice((c - chunk_ids[0]) * C, (c - chunk_ids[0] + 1) * C)
            lw = tok["lw"][rows]
            cum = _cumsum_rows(ltri_ref[...], lw)
            ge = jnp.exp(cum)
            gi = jnp.exp(-cum)
            At_all = (-tok["kk"][rows] * jnp.exp(cum - lw)).astype(BF16)
            Bt_all = (tok["kka"][rows] * gi).astype(BF16)
            Kt_all = (tok["k2"][rows] * gi).astype(BF16)
            Rt_all = (tok["r"][rows] * ge).astype(BF16)
            V_all = tok["v"][rows].astype(BF16)
            for j in range(n_groups):
                sl = slice(j * GW, (j + 1) * GW)
                items.append(dict(At=At_all[:, sl], Bt=Bt_all[:, sl], Kt=Kt_all[:, sl],
                                  Rt=Rt_all[:, sl], V=V_all[:, sl], g_row=ge[C - 1:C, sl]))

        for it in items:
            AR = jnp.concatenate([it["At"], it["Rt"]], axis=0)
            BK = jnp.concatenate([bdiag(it["Bt"]), bdiag(it["Kt"])], axis=0)
            sc = lax.dot_general(AR, BK, contract1, preferred_element_type=F32)
            it["L"] = jnp.where(strict, sc[:C, :GW], 0.0)
            it["akm"] = jnp.where(strict, sc[:C, GW:], 0.0).astype(BF16)
            it["rbm"] = jnp.where(incl, sc[C:, :GW], 0.0).astype(BF16)
            it["rkm"] = jnp.where(incl, sc[C:, GW:], 0.0).astype(BF16)

        blk = 2
        for it in items:
            it["T"] = eye + jnp.where(trow // blk == tcol // blk, it["L"], 0.0)
            it["Ld"] = bdiag(it["L"].astype(BF16))
        while blk < C:
            lvl = (trow // (2 * blk) == tcol // (2 * blk)) & (trow // blk != tcol // blk)
            if pending:
                items[0]["T"] = items[0]["T"] + pending.pop(0)()
            for it in items:
                it["Tb"] = it["T"].astype(BF16)
                it["P"] = jnp.where(lvl, _mm(it["Tb"], it["Ld"]), 0.0).astype(BF16)
            for it in items:
                it["T"] = it["T"] + _mm(it["P"], bdiag(it["Tb"]))
            blk *= 2

        for it in items:
            it["Vd"] = bdiag(it["V"])
            it["akv"] = _mm(it["akm"], it["Vd"]).astype(BF16)
        for it in items:
            au = _mm(it["T"].astype(BF16),
                     jnp.concatenate([bdiag(it["At"]), bdiag(it["akv"])], axis=1))
            it["A2"] = au[:, :GW].astype(BF16)
            it["U0"] = au[:, GW:].astype(BF16)
        for it in items:
            ry = _mm(it["rbm"], jnp.concatenate([bdiag(it["A2"]), bdiag(it["U0"])], axis=1))
            it["R2"] = (it["Rt"].astype(F32) + ry[:, :GW]).astype(BF16)
            it["Y0"] = ry[:, GW:] + _mm(it["rkm"], it["Vd"])
            it["Mq"] = (jnp.where(same_head, lax.dot_general(it["A2"], it["Bt"], contract0,
                                                             preferred_element_type=F32), 0.0)
                        * it["g_row"]).astype(BF16)
            it["Nq"] = jnp.where(same_head,
                                 lax.dot_general(jnp.concatenate([it["U0"], it["V"]], axis=0),
                                                 jnp.concatenate([it["Bt"], it["Kt"]], axis=0),
                                                 contract0, preferred_element_type=F32),
                                 0.0) * it["g_row"]
        for step in pending:
            step()
        return [{key: it[key] for key in ("R2", "Y0", "Mq", "Nq", "g_row")} for it in items]

    waves = [range(w0, min(w0 + RWKV_WAVE, G)) for w0 in range(0, G, RWKV_WAVE)]
    toks = [dict() for _ in waves]
    steps = [token_terms(slice(w[0] * C, (w[-1] + 1) * C), tok) for w, tok in zip(waves, toks)]
    for step in steps[0]:
        step()
    items = []
    for i, w in enumerate(waves):
        items += local_terms(w, toks[i], steps[i + 1] if i + 1 < len(waves) else [])
    gate = jnp.concatenate([tok["gate"] for tok in toks], axis=0)
    bonus = jnp.concatenate([tok["bonus"] for tok in toks], axis=0)

    y_chunks = []
    for c in range(G):
        ys = []
        for j in range(n_groups):
            it = items[c * n_groups + j]
            S0 = s_ref[j]
            Sb = S0.astype(BF16)
            ys.append(it["Y0"] + lax.dot_general(it["R2"], Sb, contract1,
                                                 preferred_element_type=F32))
            s_ref[j] = S0 * it["g_row"] + _mm(Sb, it["Mq"]) + it["Nq"]
        y_chunks.append(jnp.concatenate(ys, axis=1))
    Y = jnp.concatenate(y_chunks, axis=0)

    inv_n = 1.0 / N
    mean = head_sums(Y) * inv_n
    yc = Y - mean
    var = head_sums(yc * yc) * inv_n
    yn = yc * lax.rsqrt(var + RWKV_LN_EPS) * lng_ref[...] + lnb_ref[...]
    y_ref[0] = ((yn + bonus) * gate).astype(y_ref.dtype)


def _rwkv(p3, mu, w0, wup_pad, a0, aup_pad, gup, k_k, k_a, r_k, ln_g, ln_b, l):
    B, S, _ = p3.shape
    C = RWKV_CHUNK
    ts = min(RWKV_GROUP * C, S)
    group_w = RWKV_LANE_GROUP * RWKV_HEAD_DIM
    ltri = (jnp.arange(C)[None, :] <= jnp.arange(C)[:, None]).astype(BF16)
    head_of_lane = jnp.arange(group_w) // RWKV_HEAD_DIM
    bd = (head_of_lane[:, None] == head_of_lane[None, :]).astype(BF16)
    lay = lambda arr: _layer(arr, l)
    return pl.pallas_call(
        _rwkv_kernel,
        grid=(B, S // ts),
        in_specs=[pl.BlockSpec((1, ts, RWKV_IN), lambda b, s: (b, s, 0)), lay(mu), lay(w0),
                  lay(wup_pad), lay(a0), lay(aup_pad), lay(gup), lay(k_k), lay(k_a), lay(r_k),
                  lay(ln_g), lay(ln_b), _full((C, C)), _full((group_w, group_w))],
        out_specs=pl.BlockSpec((1, ts, RWKV_W), lambda b, s: (b, s, 0)),
        out_shape=jax.ShapeDtypeStruct((B, S, RWKV_W), BF16),
        scratch_shapes=[pltpu.VMEM((SUBLANES, RWKV_IN), F32),
                        pltpu.VMEM((RWKV_W // group_w, group_w, group_w), F32)],
        compiler_params=_params("parallel", "arbitrary"),
        name="rwkv",
    )(p3, mu, w0, wup_pad, a0, aup_pad, gup, k_k, k_a, r_k, ln_g, ln_b, ltri, bd)


def _shift_rows(x, d, fill):
    n = x.shape[0]
    if d % SUBLANES == 0:
        return jnp.concatenate([jnp.full((d,) + x.shape[1:], fill, x.dtype), x[:n - d]], axis=0)
    rows = lax.broadcasted_iota(jnp.int32, x.shape, 0)
    return jnp.where(rows < d, fill, pltpu.roll(x, d, 0))


def _scan_block(a, u):
    d = 1
    while d < a.shape[0]:
        u = u + a * _shift_rows(u, d, 0.0)
        a = a * _shift_rows(a, d, 1.0)
        d *= 2
    return a, u


def _lru_tile(p, cw_ref, cb_ref, wa_ref, ba_ref, wx_ref, bx_ref, lam_ref, xpad, hcarry, first,
              background=()):
    @pl.when(first)
    def _():
        xpad[0:SUBLANES, :] = jnp.zeros((SUBLANES, LRU_W), F32)
        hcarry[...] = jnp.zeros_like(hcarry)

    ts = p.shape[0]
    xb, gb = p[:, :LRU_W], p[:, LRU_W:]
    xpad[SUBLANES:, :] = xb
    cw = cw_ref[...]
    xc = cb_ref[...] + xb * cw[CONV_WIDTH - 1:CONV_WIDTH, :]
    for j in range(CONV_WIDTH - 1):
        lo = SUBLANES - (CONV_WIDTH - 1) + j
        xc = xc + xpad[lo:lo + ts, :] * cw[j:j + 1, :]
    xpad[0:SUBLANES, :] = xb[ts - SUBLANES:, :]
    rg = _sigmoid(_dot(xc, wa_ref[...]) + ba_ref[...])
    ig = _sigmoid(_dot(xc, wx_ref[...]) + bx_ref[...])
    lam = lam_ref[...]
    softplus_neg_lam = jnp.maximum(-lam, 0.0) + jnp.log(1.0 + jnp.exp(-jnp.abs(lam)))
    log_a = -LRU_C * rg * softplus_neg_lam
    a = jnp.exp(log_a)
    u = jnp.sqrt(1.0 - a * a) * (ig * xc)
    pending = list(background)
    n_row_blocks = ts // LRU_SCAN_ROWS
    stride = max(1, (LRU_W // LANES) * n_row_blocks // max(1, len(pending)))
    strips = []
    for ls in range(LRU_W // LANES):
        cols = slice(ls * LANES, (ls + 1) * LANES)
        h_in = hcarry[SUBLANES - 1:SUBLANES, cols]
        blocks = []
        for rt in range(n_row_blocks):
            if pending and (ls * n_row_blocks + rt) % stride == 0:
                h_in = h_in + pending.pop(0)()
            rs = slice(rt * LRU_SCAN_ROWS, (rt + 1) * LRU_SCAN_ROWS)
            a_blk, u_blk = _scan_block(a[rs, cols], u[rs, cols])
            h_blk = u_blk + a_blk * h_in
            h_in = h_blk[LRU_SCAN_ROWS - 1:, :]
            blocks.append(h_blk)
        strips.append(jnp.concatenate(blocks, axis=0))
    for thunk in pending:
        thunk()
    h = jnp.concatenate(strips, axis=1)
    hcarry[...] = h[ts - SUBLANES:, :]
    gelu = 0.5 * gb * (1.0 + jnp.tanh(math.sqrt(2.0 / math.pi) * (gb + 0.044715 * gb * gb * gb)))
    return h * gelu


def _mla_project(p, qn_ref, wuq_ref, kvn_ref, wuk_ref, wuv_ref, vt_o):
    cq = p[:, :Q_RANK]
    ckv = p[:, Q_RANK:Q_RANK + KV_RANK]
    qq = _dot(_rms(cq, qn_ref[...]), wuq_ref[...])
    ckv_n = _rms(ckv, kvn_ref[...]).astype(BF16)
    kn = _mm(ckv_n, wuk_ref[...])
    one_lane = (lax.broadcasted_iota(jnp.int32, ckv_n.shape, 1) == 0).astype(BF16)
    vt_o[0] = _dot_nt(wuv_ref[...], jnp.concatenate([ckv_n, one_lane], axis=1)).astype(BF16)
    return qq, kn


def _mla_rotate(p, qq, kn, cos_ref, sin_ref, qg_ref, qgs_ref, kg_ref, kgs_ref, q_o, k_o,
                background=()):
    W = MLA_HEADS * MLA_HEAD_PAD
    kr = p[:, Q_RANK + KV_RANK:Q_RANK + KV_RANK + LANES]
    kr_sw = p[:, Q_RANK + KV_RANK + LANES:]
    cosf = cos_ref[...]
    sinf = sin_ref[...]
    scale = MLA_QK ** -0.5 * math.log2(math.e)
    cq_tab = cosf * (qg_ref[...] * scale)
    sq_tab = sinf * (qgs_ref[...] * scale)
    ck_tab = cosf * kg_ref[...]
    sk_tab = sinf * kgs_ref[...]
    kr_rot = kr * ck_tab + kr_sw * sk_tab
    kr_ss = jnp.sum(kr * kr, axis=-1, keepdims=True)
    pending = list(background)
    for h in range(MLA_HEADS):
        sl = slice(h * MLA_HEAD_PAD, (h + 1) * MLA_HEAD_PAD)
        qh = qq[:, sl]
        q_rs = lax.rsqrt(jnp.sum(qh * qh, axis=-1, keepdims=True) / MLA_QK + 1e-6)
        if pending:
            q_rs = q_rs + pending.pop(0)()[:, 0:1]
        q_o[:, sl] = ((qh * cq_tab + qq[:, W + h * MLA_HEAD_PAD:W + (h + 1) * MLA_HEAD_PAD] * sq_tab)
                      * q_rs).astype(BF16)
        kh = kn[:, sl]
        k_rs = lax.rsqrt((jnp.sum(kh * kh, axis=-1, keepdims=True) + kr_ss) / MLA_QK + 1e-6)
        k_o[:, sl] = ((kh * ck_tab + kr_rot) * k_rs).astype(BF16)
    for thunk in pending:
        thunk()


def _mla_attn_kernel(q_ref, k_ref, vt_ref, o_ref, acc_ref, m_ref, l_ref, s_ref, p_ref, a_ref):
    qi = pl.program_id(1)
    tq = q_ref.shape[1]
    m_ref[...] = jnp.full(m_ref.shape, MASKED_SCORE, F32)
    l_ref[...] = jnp.zeros(l_ref.shape, F32)
    acc_ref[...] = jnp.zeros(acc_ref.shape, F32)

    def scores(j, masked, slot):
        start = pl.multiple_of(j * tq, tq)
        ties = []
        for h in range(MLA_HEADS):
            hs = slice(h * MLA_HEAD_PAD, (h + 1) * MLA_HEAD_PAD)
            kb = k_ref[0, pl.ds(start, tq), hs]
            st = lax.dot_general(kb, q_ref[0, :, hs], (((1,), (1,)), ((), ())),
                                 preferred_element_type=F32)
            if masked:
                kpos = lax.broadcasted_iota(jnp.int32, st.shape, 0)
                qpos = lax.broadcasted_iota(jnp.int32, st.shape, 1)
                st = jnp.where(kpos <= qpos, st, MASKED_SCORE)
            s_ref[slot, h] = st.astype(BF16)
            ties.append(jnp.minimum(jnp.abs(st[0:1, :]), 0.0))
        return ties

    def softmax(slot, ties=None):
        for h in range(MLA_HEADS):
            sb = s_ref[slot, h]
            m_old = m_ref[h:h + 1, :]
            if ties is not None:
                m_old = m_old + ties[h]
            m_new = jnp.maximum(m_old, jnp.max(sb, axis=0, keepdims=True).astype(F32))
            p_ref[slot, h] = jnp.exp2(sb - m_new.astype(BF16))
            a_ref[slot, h:h + 1, :] = jnp.exp2(m_old - m_new)
            m_ref[h:h + 1, :] = m_new

    def values(j, slot):
        start = pl.multiple_of(j * tq, tq)
        for h in range(MLA_HEADS):
            vs = slice(h * MLA_V, (h + 1) * MLA_V)
            ve = slice(h * MLA_V_EXT, (h + 1) * MLA_V_EXT)
            pvx = _mm(vt_ref[0, ve, pl.ds(start, tq)], p_ref[slot, h])
            alpha = a_ref[slot, h:h + 1, :]
            l_ref[h:h + 1, :] = alpha * l_ref[h:h + 1, :] + pvx[MLA_V:MLA_V + 1, :]
            acc_ref[vs, :] = alpha * acc_ref[vs, :] + pvx[:MLA_V, :]

    def tile(j, masked, slot):
        scores(j, masked, slot)
        softmax(slot)
        values(j, slot)

    def body(i, carry):
        scores(2 * i, False, 0)
        ties = scores(2 * i + 1, False, 1)
        softmax(0, ties)
        values(2 * i, 0)
        softmax(1)
        values(2 * i + 1, 1)
        return carry

    lax.fori_loop(0, qi // 2, body, 0)

    @pl.when(qi % 2 == 1)
    def _():
        tile(qi - 1, False, 0)

    tile(qi, True, 1)
    for h in range(MLA_HEADS):
        vs = slice(h * MLA_V, (h + 1) * MLA_V)
        acc_ref[vs, :] = acc_ref[vs, :] / l_ref[h:h + 1, :]
    o_ref[0] = acc_ref[...].T.astype(o_ref.dtype)


def _mla_attn(q, k, vt):
    B, S, W = q.shape
    tq = min(ATT_TILE, S)
    WV = MLA_HEADS * MLA_V
    return pl.pallas_call(
        _mla_attn_kernel,
        grid=(B, S // tq),
        in_specs=[pl.BlockSpec((1, tq, W), lambda b, i: (b, i, 0)),
                  pl.BlockSpec((1, S, W), lambda b, i: (b, 0, 0)),
                  pl.BlockSpec((1, MLA_HEADS * MLA_V_EXT, S), lambda b, i: (b, 0, 0))],
        out_specs=pl.BlockSpec((1, tq, WV), lambda b, i: (b, i, 0)),
        out_shape=jax.ShapeDtypeStruct((B, S, WV), BF16),
        scratch_shapes=[pltpu.VMEM((WV, tq), F32), pltpu.VMEM((MLA_HEADS, tq), F32),
                        pltpu.VMEM((MLA_HEADS, tq), F32), pltpu.VMEM((2, MLA_HEADS, tq, tq), BF16),
                        pltpu.VMEM((2, MLA_HEADS, tq, tq), BF16),
                        pltpu.VMEM((2, MLA_HEADS, tq), F32)],
        compiler_params=_params("parallel", "arbitrary"),
        name="mla_attn",
    )(q, k, vt)


def _merge_kernel(x_ref, ya_ref, yb_ref, yc_ref, g_ref, wg_ref, bg_ref, wb_ref, wo_ref, o_ref):
    x = x_ref[...]
    h = _rms(x, g_ref[...]).astype(BF16)
    merged = None
    for n, y_ref in enumerate((ya_ref, yb_ref, yc_ref)):
        sl = slice(n * D_MODEL, (n + 1) * D_MODEL)
        gate = _sigmoid(_mm(h, wg_ref[:, sl]) + bg_ref[:, sl])
        term = gate * _mm(y_ref[...], wb_ref[n])
        merged = term if merged is None else merged + term
    o_ref[...] = x + _dot(merged, wo_ref[...])


def _merge(x2, ya, yb, yc, g, w_gate, b_gate, w_branch, w_out, l):
    T = x2.shape[0]
    tm = min(TOK_TILE, T)
    lay = lambda arr: _layer(arr, l)
    return pl.pallas_call(
        _merge_kernel,
        grid=(T // tm,),
        in_specs=[_rows(tm, D_MODEL), _rows(tm, BRANCH_W), _rows(tm, BRANCH_W), _rows(tm, BRANCH_W),
                  lay(g), lay(w_gate), lay(b_gate), lay(w_branch), lay(w_out)],
        out_specs=_rows(tm, D_MODEL),
        out_shape=jax.ShapeDtypeStruct((T, D_MODEL), F32),
        compiler_params=_params("parallel"),
        name="merge",
    )(x2, ya, yb, yc, g, w_gate, b_gate, w_branch, w_out)


def _mem_kv_kernel(m_ref, g_ref, wkv_ref, kg_ref, k_o, v_o):
    h = _rms(m_ref[...], g_ref[...]).astype(BF16)
    kv = _mm(h, wkv_ref[...])
    for hd in range(XA_HEADS):
        sl = slice(hd * XA_HEAD_DIM, (hd + 1) * XA_HEAD_DIM)
        base = 2 * hd * XA_HEAD_DIM
        k_o[:, sl] = _rms(kv[:, base:base + XA_HEAD_DIM], kg_ref[...]).astype(BF16)
        v_o[:, sl] = kv[:, base + XA_HEAD_DIM:base + 2 * XA_HEAD_DIM].astype(BF16)


def _mem_kv(mem2, g, wkv, k_gain, l):
    M = mem2.shape[0]
    tm = min(TOK_TILE, M)
    lay = lambda arr: _layer(arr, l)
    return pl.pallas_call(
        _mem_kv_kernel,
        grid=(M // tm,),
        in_specs=[_rows(tm, D_MODEL), lay(g), lay(wkv), lay(k_gain)],
        out_specs=[_rows(tm, XA_W), _rows(tm, XA_W)],
        out_shape=[jax.ShapeDtypeStruct((M, XA_W), BF16)] * 2,
        compiler_params=_params("parallel"),
        name="mem_kv",
    )(mem2, g, wkv, k_gain)


def _xattn_kernel(x_ref, k_ref, v_ref, g_ref, wq_ref, qg_ref, wo_ref, o_ref):
    x = x_ref[0]
    h = _rms(x, g_ref[...]).astype(BF16)
    q = _mm(h, wq_ref[...])
    gain = qg_ref[...] * (XA_HEAD_DIM ** -0.5 * math.log2(math.e))
    sls = [slice(hd * XA_HEAD_DIM, (hd + 1) * XA_HEAD_DIM) for hd in range(XA_HEADS)]
    qhs = [_rms(q[:, sl], gain).astype(BF16) for sl in sls]
    scores = [_dot_nt(qh, k_ref[0, :, sl]) for qh, sl in zip(qhs, sls)]
    probs, inv_sums = [], []
    for s in scores:
        e = jnp.exp2(s - jnp.max(s, axis=-1, keepdims=True))
        inv_sums.append(1.0 / jnp.sum(e, axis=-1, keepdims=True))
        probs.append(e.astype(BF16))
    outs = [_mm(pr, v_ref[0, :, sl]) * inv for pr, sl, inv in zip(probs, sls, inv_sums)]
    o = jnp.concatenate(outs, axis=-1)
    o_ref[0] = x + _dot(o, wo_ref[...])


def _xattn(x3, k3, v3, g, wq, q_gain, wo, l):
    B, S, _ = x3.shape
    ts = min(TOK_TILE, S)
    M = k3.shape[1]
    lay = lambda arr: _layer(arr, l)
    return pl.pallas_call(
        _xattn_kernel,
        grid=(B, S // ts),
        in_specs=[pl.BlockSpec((1, ts, D_MODEL), lambda b, s: (b, s, 0)),
                  pl.BlockSpec((1, M, XA_W), lambda b, s: (b, 0, 0)),
                  pl.BlockSpec((1, M, XA_W), lambda b, s: (b, 0, 0)),
                  lay(g), lay(wq), lay(q_gain), lay(wo)],
        out_specs=pl.BlockSpec((1, ts, D_MODEL), lambda b, s: (b, s, 0)),
        out_shape=jax.ShapeDtypeStruct((B, S, D_MODEL), F32),
        compiler_params=_params("parallel", "parallel"),
        name="xattn",
    )(x3, k3, v3, g, wq, q_gain, wo)


FF_SPLIT = 2


def _ffn_kernel(x_ref, g_ref, w1_ref, w3_ref, w2_ref, o_ref):
    x = x_ref[...]
    h = _rms(x, g_ref[...]).astype(BF16)
    step = D_FF // FF_SPLIT
    acc = x
    for c in range(FF_SPLIT):
        sl = slice(c * step, (c + 1) * step)
        a = _mm(h, w1_ref[:, sl])
        b = _mm(h, w3_ref[:, sl])
        z = a * _sigmoid(a) * b
        acc = acc + _dot(z, w2_ref[sl, :])
    o_ref[...] = acc


def _ffn(x2, g, w1, w3, w2, l):
    T = x2.shape[0]
    tm = min(TOK_TILE, T)
    lay = lambda arr: _layer(arr, l)
    return pl.pallas_call(
        _ffn_kernel,
        grid=(T // tm,),
        in_specs=[_rows(tm, D_MODEL), lay(g), lay(w1), lay(w3), lay(w2)],
        out_specs=_rows(tm, D_MODEL),
        out_shape=jax.ShapeDtypeStruct((T, D_MODEL), F32),
        compiler_params=_params("parallel"),
        name="ffn",
    )(x2, g, w1, w3, w2)


def _block_diag(w):
    L, n, i, j = w.shape
    eye = jnp.eye(n, dtype=w.dtype)
    return jnp.einsum("lnij,nm->lnimj", w, eye).reshape(L, n * i, n * j)


def _pad_heads(w, heads, width, pad_to):
    lead = w.shape[:-1]
    w = w.reshape(lead + (heads, width))
    w = jnp.pad(w, [(0, 0)] * len(lead) + [(0, 0), (0, pad_to - width)])
    return w.reshape(lead + (heads * pad_to,))


def _rope_partner(w):
    half = MLA_ROPE // 2
    return jnp.concatenate([jnp.zeros_like(w[..., :MLA_NOPE]), w[..., MLA_NOPE + half:],
                            w[..., MLA_NOPE:MLA_NOPE + half]], axis=-1)


def _vec(v):
    return v.reshape(v.shape[0], 1, -1).astype(F32)


def kernel(x, mem, positions, norm_mix, norm_xattn, norm_mem, norm_ffn, w_in, b_gate, rwkv_mu, rwkv_w0, rwkv_w_up, rwkv_a0, rwkv_a_up, rwkv_g_up, rwkv_k_k, rwkv_k_a, rwkv_r_k, rwkv_ln_g, rwkv_ln_b, lru_conv_w, lru_conv_b, lru_wa, lru_ba, lru_wx, lru_bx, lru_lambda, mla_q_norm, mla_w_uq, mla_kv_norm, mla_w_ukv, mla_q_gain, mla_k_gain, w_branch, w_out, xa_w_q, xa_w_kv, xa_q_gain, xa_k_gain, xa_w_o, ffn_w1, ffn_w3, ffn_w2):
    B, S, D = x.shape
    T = B * S
    depth = w_in.shape[0]
    x2 = x.reshape(T, D)
    mem2 = mem.reshape(B * N_MEM, D)
    half = MLA_ROPE // 2

    inv_freq = ROPE_THETA ** (-jnp.arange(0, MLA_ROPE, 2, dtype=F32) / MLA_ROPE)
    per_row = LANES // half
    pos_rep = jnp.repeat(positions.reshape(T // per_row, per_row), half, axis=1)
    cos_c, sin_c, nsin_c = _rope_tables(pos_rep, jnp.tile(inv_freq, per_row).reshape(1, LANES))
    cos_c, sin_c, nsin_c = (t.reshape(T, half) for t in (cos_c, sin_c, nsin_c))
    ones = jnp.ones((T, MLA_NOPE), F32)
    zeros = jnp.zeros((T, MLA_NOPE), F32)
    cosf = jnp.concatenate([ones, cos_c, cos_c, ones[:, :LANES - MLA_QK]], axis=1)
    sinf = jnp.concatenate([zeros, nsin_c, sin_c, zeros[:, :LANES - MLA_QK]], axis=1)

    w_in_b = w_in[:, :, :MLA_OFF].astype(BF16)
    w_kr = w_in[:, :, MLA_OFF + Q_RANK + KV_RANK:GATE_OFF].astype(BF16)
    zero_cols = lambda n: jnp.zeros((depth, D, n), BF16)
    w_mla = jnp.concatenate(
        [w_in[:, :, MLA_OFF:MLA_OFF + Q_RANK + KV_RANK].astype(BF16), zero_cols(MLA_NOPE), w_kr,
         zero_cols(LANES - MLA_QK), zero_cols(MLA_NOPE), w_kr[:, :, half:], w_kr[:, :, :half],
         zero_cols(LANES - MLA_QK)], axis=2)
    w_gate = w_in[:, :, GATE_OFF:].astype(BF16)

    zeros_lora = jnp.zeros((depth, W_LORA, RWKV_W), F32)
    wup_pad = jnp.concatenate([rwkv_w_up, zeros_lora], axis=1).astype(BF16)
    aup_pad = jnp.concatenate([zeros_lora, rwkv_a_up], axis=1).astype(BF16)
    gup = rwkv_g_up.astype(BF16)
    r_k = rwkv_r_k.reshape(depth, RWKV_W)

    wa_bd = _block_diag(lru_wa).astype(BF16)
    wx_bd = _block_diag(lru_wx).astype(BF16)

    wuq_h = mla_w_uq.reshape(depth, Q_RANK, MLA_HEADS, MLA_QK)
    wuq2 = jnp.concatenate(
        [_pad_heads(mla_w_uq, MLA_HEADS, MLA_QK, MLA_HEAD_PAD),
         _pad_heads(_rope_partner(wuq_h).reshape(depth, Q_RANK, -1), MLA_HEADS, MLA_QK, MLA_HEAD_PAD)],
        axis=2).astype(BF16)
    wukv = mla_w_ukv.reshape(depth, KV_RANK, MLA_HEADS, MLA_NOPE + MLA_V)
    wuk = _pad_heads(wukv[..., :MLA_NOPE].reshape(depth, KV_RANK, -1), MLA_HEADS, MLA_NOPE,
                     MLA_HEAD_PAD).astype(BF16)
    wuvt = jnp.swapaxes(wukv[..., MLA_NOPE:], 1, 2).transpose(0, 1, 3, 2)
    wuvt = jnp.pad(wuvt, ((0, 0), (0, 0), (0, MLA_V_EXT - MLA_V), (0, KV_RANK)))
    wuvt = wuvt.at[:, :, MLA_V, KV_RANK].set(1.0)
    wuvt = wuvt.reshape(depth, MLA_HEADS * MLA_V_EXT, 2 * KV_RANK).astype(BF16)
    pad_gain = lambda gv: _vec(jnp.pad(gv, ((0, 0), (0, LANES - MLA_QK))))
    qg, qgs = pad_gain(mla_q_gain), pad_gain(_rope_partner(mla_q_gain))
    kg, kgs = pad_gain(mla_k_gain), pad_gain(_rope_partner(mla_k_gain))

    w_branch_b = w_branch.astype(BF16)
    w_out_b = w_out.astype(BF16)
    xa_wkv = xa_w_kv.astype(BF16)
    xa_wq = xa_w_q.astype(BF16)
    xa_wo = xa_w_o.astype(BF16)
    w1, w3, w2 = ffn_w1.astype(BF16), ffn_w3.astype(BF16), ffn_w2.astype(BF16)

    n_mix, n_xa, n_mem, n_ffn = _vec(norm_mix), _vec(norm_xattn), _vec(norm_mem), _vec(norm_ffn)
    rw = [_vec(t) for t in (rwkv_mu, rwkv_w0, rwkv_a0, rwkv_k_k, rwkv_k_a, r_k, rwkv_ln_g, rwkv_ln_b)]
    mu, w0, a0, k_k, k_a, r_kv, ln_g, ln_b = rw
    conv_b, ba, bx, lam = _vec(lru_conv_b), _vec(lru_ba), _vec(lru_bx), _vec(lru_lambda)
    q_norm, kv_norm = _vec(mla_q_norm), _vec(mla_kv_norm)
    b_gate_v, xa_qg, xa_kg = _vec(b_gate), _vec(xa_q_gain), _vec(xa_k_gain)

    lru_w = (lru_conv_w, conv_b, wa_bd, ba, wx_bd, bx, lam)
    mla_w = (q_norm, wuq2, kv_norm, wuk, wuvt, qg, qgs, kg, kgs)
    W = MLA_HEADS * MLA_HEAD_PAD
    for l in range(depth):
        p_rwkv, y_b, q, k, vt = _in_proj(x2, n_mix, w_in_b, w_mla, lru_w, mla_w, cosf, sinf, B, S, l)

        y_a = _rwkv(p_rwkv.reshape(B, S, RWKV_IN), mu, w0, wup_pad, a0, aup_pad, gup, k_k, k_a,
                    r_kv, ln_g, ln_b, l).reshape(T, RWKV_W)
        y_c = _mla_attn(q.reshape(B, S, W), k.reshape(B, S, W), vt).reshape(T, MLA_HEADS * MLA_V)

        x2 = _merge(x2, y_a, y_b, y_c, n_mix, w_gate, b_gate_v, w_branch_b, w_out_b, l)

        mk, mv = _mem_kv(mem2, n_mem, xa_wkv, xa_kg, l)
        x2 = _xattn(x2.reshape(B, S, D), mk.reshape(B, N_MEM, XA_W), mv.reshape(B, N_MEM, XA_W),
                    n_xa, xa_wq, xa_qg, xa_wo, l).reshape(T, D)

        x2 = _ffn(x2, n_ffn, w1, w3, w2, l)
    return x2.reshape(B, S, D)
```

```python
import functools
import math

import jax
import jax.numpy as jnp
from jax import lax
from jax.experimental import pallas as pl
from jax.experimental.pallas import tpu as pltpu

F32 = jnp.float32
BF16 = jnp.bfloat16

D_MODEL = 1024
N_MEM = 256
RWKV_HEADS = 8
RWKV_HEAD_DIM = 64
RWKV_W = RWKV_HEADS * RWKV_HEAD_DIM
W_LORA = 64
A_LORA = 64
G_LORA = 128
RWKV_IN = 3 * RWKV_W + W_LORA + A_LORA + G_LORA
RWKV_LN_EPS = RWKV_HEAD_DIM * 1e-5
LRU_BLOCKS = 8
LRU_W = 512
CONV_WIDTH = 4
LRU_C = 8.0
MLA_HEADS = 8
MLA_NOPE = 64
MLA_ROPE = 32
MLA_QK = MLA_NOPE + MLA_ROPE
MLA_V = 64
Q_RANK = 256
KV_RANK = 128
ROPE_THETA = 10000.0
N_BRANCH = 3
BRANCH_W = 512
XA_HEADS = 4
XA_HEAD_DIM = 128
XA_W = XA_HEADS * XA_HEAD_DIM
D_FF = -(-8 * D_MODEL // (3 * 256)) * 256
LRU_OFF = RWKV_IN
MLA_OFF = LRU_OFF + 2 * LRU_W
GATE_OFF = MLA_OFF + Q_RANK + KV_RANK + MLA_ROPE

LANES = 128
SUBLANES = 8
BF16_SUBLANES = 16
MXU_TILE = 256
VMEM_LIMIT = 56 * 1024 * 1024
MLA_HEAD_PAD = LANES
MLA_V_EXT = MLA_V + BF16_SUBLANES
MLA_PAD = Q_RANK + KV_RANK + 2 * LANES
MASKED_SCORE = -2.0 ** 100
RWKV_CHUNK = 64
RWKV_GROUP = 16
RWKV_WAVE = 4
RWKV_LANE_GROUP = MXU_TILE // RWKV_HEAD_DIM
LRU_SCAN_ROWS = 64
TOK_TILE = 512
ATT_TILE = 256


def _params(*sem):
    return pltpu.CompilerParams(dimension_semantics=sem, vmem_limit_bytes=VMEM_LIMIT)


def _mm(a, b):
    return jnp.dot(a, b, preferred_element_type=F32)


def _dot(a, b):
    return _mm(a.astype(BF16), b.astype(BF16))


def _dot_nt(a, b):
    return lax.dot_general(a.astype(BF16), b.astype(BF16), (((1,), (1,)), ((), ())),
                           preferred_element_type=F32)


def _sigmoid(x):
    return 0.5 * jnp.tanh(0.5 * x) + 0.5


def _rms(x, g, eps=1e-6):
    return x * lax.rsqrt(jnp.mean(x * x, axis=-1, keepdims=True) + eps) * g


def _full(shape):
    n = len(shape)
    return pl.BlockSpec(shape, lambda *_: (0,) * n)


def _layer(arr, l):
    tail = arr.shape[1:]
    return pl.BlockSpec((None,) + tail, lambda *_: (l,) + (0,) * len(tail))


def _rows(tm, n):
    return pl.BlockSpec((tm, n), lambda i: (i, 0))


def _rope_kernel(pos_ref, freq_ref, cos_o, sin_o, nsin_o):
    ang = pos_ref[...].astype(F32) * freq_ref[...]
    s = jnp.sin(ang)
    cos_o[...] = jnp.cos(ang)
    sin_o[...] = s
    nsin_o[...] = -s


def _rope_tables(pos_rep, freq_tile):
    R = pos_rep.shape[0]
    tm = min(TOK_TILE, R)
    out = jax.ShapeDtypeStruct((R, LANES), F32)
    return pl.pallas_call(
        _rope_kernel,
        grid=(R // tm,),
        in_specs=[_rows(tm, LANES), _full((1, LANES))],
        out_specs=[_rows(tm, LANES)] * 3,
        out_shape=[out] * 3,
        compiler_params=_params("parallel"),
        name="rope_tables",
    )(pos_rep, freq_tile)


def _in_proj_kernel(x_ref, g_ref, wrl_ref, wm_ref,
                    cw_ref, cb_ref, wa_ref, ba_ref, wx_ref, bx_ref, lam_ref,
                    cos_ref, sin_ref, qn_ref, wuq_ref, kvn_ref, wuk_ref, wuv_ref,
                    qg_ref, qgs_ref, kg_ref, kgs_ref,
                    or_ref, yb_ref, q_o, k_o, vt_o, xpad, hcarry, *, tiles_per_row):
    first = pl.program_id(0) % tiles_per_row == 0
    h = _rms(x_ref[...], g_ref[...]).astype(BF16)
    p_lru = _mm(h, wrl_ref[:, RWKV_IN:])

    def rwkv_cols(c0):
        def run():
            res = _mm(h, wrl_ref[:, c0:c0 + MXU_TILE])
            or_ref[:, c0:c0 + MXU_TILE] = res
            return jnp.minimum(jnp.abs(res[0:1, 0:LANES]), 0.0)
        return run

    mla = {}

    def mla_dots():
        p_mla = _mm(h, wm_ref[...])
        qq, kn = _mla_project(p_mla, qn_ref, wuq_ref, kvn_ref, wuk_ref, wuv_ref, vt_o)
        mla.update(p=p_mla, qq=qq, kn=kn)
        return jnp.minimum(jnp.abs(qq[0:1, 0:LANES] + kn[0:1, 0:LANES]), 0.0)

    rwkv_dots = [rwkv_cols(c0) for c0 in range(0, RWKV_IN, MXU_TILE)]
    n_lru = len(rwkv_dots)
    yb_ref[...] = _lru_tile(p_lru, cw_ref, cb_ref, wa_ref, ba_ref, wx_ref, bx_ref, lam_ref,
                            xpad, hcarry, first,
                            background=[mla_dots] + rwkv_dots[:n_lru]).astype(yb_ref.dtype)
    _mla_rotate(mla["p"], mla["qq"], mla["kn"], cos_ref, sin_ref, qg_ref, qgs_ref, kg_ref, kgs_ref,
                q_o, k_o, background=rwkv_dots[n_lru:])


def _in_proj(x2, g, w_in_b, w_mla, lru_w, mla_w, cosf, sinf, B, S, l):
    T = x2.shape[0]
    tm = min(TOK_TILE, S)
    per_row = S // tm
    W = MLA_HEADS * MLA_HEAD_PAD
    lay = lambda arr: _layer(arr, l)
    return pl.pallas_call(
        functools.partial(_in_proj_kernel, tiles_per_row=per_row),
        grid=(T // tm,),
        in_specs=[_rows(tm, D_MODEL), lay(g), lay(w_in_b), lay(w_mla)] + [lay(w) for w in lru_w]
                 + [_rows(tm, LANES), _rows(tm, LANES)] + [lay(w) for w in mla_w],
        out_specs=[_rows(tm, RWKV_IN), _rows(tm, LRU_W), _rows(tm, W), _rows(tm, W),
                   pl.BlockSpec((1, MLA_HEADS * MLA_V_EXT, tm),
                                lambda i: (i // per_row, 0, i % per_row))],
        out_shape=[jax.ShapeDtypeStruct((T, RWKV_IN), F32),
                   jax.ShapeDtypeStruct((T, LRU_W), BF16),
                   jax.ShapeDtypeStruct((T, W), BF16), jax.ShapeDtypeStruct((T, W), BF16),
                   jax.ShapeDtypeStruct((B, MLA_HEADS * MLA_V_EXT, S), BF16)],
        scratch_shapes=[pltpu.VMEM((tm + SUBLANES, LRU_W), F32), pltpu.VMEM((SUBLANES, LRU_W), F32)],
        compiler_params=_params("arbitrary"),
        name="in_proj",
    )(x2, g, w_in_b, w_mla, *lru_w, cosf, sinf, *mla_w)


def _cumsum_rows(ltri, x):
    hi = x.astype(BF16)
    r1 = x - hi.astype(F32)
    mid = r1.astype(BF16)
    lo = (r1 - mid.astype(F32)).astype(BF16)
    return _mm(ltri, hi) + _mm(ltri, mid) + _mm(ltri, lo)


def _rwkv_kernel(p_ref, mu_ref, w0_ref, wup_ref, a0_ref, aup_ref, gup_ref, kkw_ref, ka_ref,
                 rk_ref, lng_ref, lnb_ref, ltri_ref, bd_ref, y_ref, carry, s_ref):
    @pl.when(pl.program_id(1) == 0)
    def _():
        carry[...] = jnp.zeros_like(carry)
        s_ref[...] = jnp.zeros_like(s_ref)

    C, N = RWKV_CHUNK, RWKV_HEAD_DIM
    GW = RWKV_LANE_GROUP * N
    n_groups = RWKV_W // GW
    bd = bd_ref[...]

    def head_sums(x):
        return jnp.concatenate([_mm(x[:, j * GW:(j + 1) * GW].astype(BF16), bd)
                                for j in range(n_groups)], axis=1)

    p = p_ref[0]
    ts = p.shape[0]
    G = ts // C
    prow = lax.broadcasted_iota(jnp.int32, p.shape, 0)
    prev = jnp.where(prow == 0, carry[SUBLANES - 1:SUBLANES, :], pltpu.roll(p, 1, 0))
    carry[...] = p[ts - SUBLANES:, :]
    pm = p + (prev - p) * mu_ref[...]
    o1, o2, o3 = RWKV_W, 2 * RWKV_W, 3 * RWKV_W

    def exact_zero(x):
        return jnp.minimum(jnp.abs(x[0:1, 0:GW]), 0.0)

    def token_terms(rows, out):
        pr = pm[rows]
        r, k, v = pr[:, :o1], pr[:, o1:o2], pr[:, o2:o3]
        wa = pr[:, o3:o3 + W_LORA + A_LORA]
        gd = pr[:, o3 + W_LORA + A_LORA:]
        out.update(r=r, v=v)

        def decay():
            z = w0_ref[...] + _dot(jnp.tanh(wa), wup_ref[...])
            out["lw"] = -math.exp(-0.5) * _sigmoid(z)
            return exact_zero(out["lw"])

        def rates():
            out["a"] = _sigmoid(a0_ref[...] + _dot(wa, aup_ref[...]))
            out["gate"] = _dot(_sigmoid(gd), gup_ref[...])
            return exact_zero(out["a"] + out["gate"])

        def unit_keys():
            kk = k * kkw_ref[...]
            out["kk"] = kk / jnp.maximum(jnp.sqrt(head_sums(kk * kk)), 1e-12)
            return exact_zero(out["kk"])

        def keys():
            out["k2"] = k * (1.0 + (out["a"] - 1.0) * ka_ref[...])
            out["bonus"] = head_sums(r * out["k2"] * rk_ref[...]) * v
            out["kka"] = out["kk"] * out["a"]
            return exact_zero(out["bonus"] + out["kka"])

        return [decay, rates, unit_keys, keys]

    lane_head = lax.broadcasted_iota(jnp.int32, (C, GW), 1) // N
    head_sel = [lane_head == h for h in range(RWKV_LANE_GROUP)]

    def bdiag(x):
        zero = jnp.zeros_like(x)
        return jnp.concatenate([jnp.where(sel, x, zero) for sel in head_sel], axis=0)

    trow = lax.broadcasted_iota(jnp.int32, (C, GW), 0)
    tcol = lax.broadcasted_iota(jnp.int32, (C, GW), 1) % C
    strict = tcol < trow
    incl = tcol <= trow
    eye = (tcol == trow).astype(F32)
    vrow = lax.broadcasted_iota(jnp.int32, (GW, GW), 0) // N
    vcol = lax.broadcasted_iota(jnp.int32, (GW, GW), 1) // N
    same_head = vrow == vcol
    contract0 = (((0,), (0,)), ((), ()))
    contract1 = (((1,), (1,)), ((), ()))

    def local_terms(chunk_ids, tok, background):
        pending = list(background)
        items = []
        for c in chunk_ids:
            rows = slice((c - chunk_ids[0]) * C, (c - chunk_ids[0] + 1) * C)
            lw = tok["lw"][rows]
            cum = _cumsum_rows(ltri_ref[...], lw)
            ge = jnp.exp(cum)
            gi = jnp.exp(-cum)
            At_all = (-tok["kk"][rows] * jnp.exp(cum - lw)).astype(BF16)
            Bt_all = (tok["kka"][rows] * gi).astype(BF16)
            Kt_all = (tok["k2"][rows] * gi).astype(BF16)
            Rt_all = (tok["r"][rows] * ge).astype(BF16)
            V_all = tok["v"][rows].astype(BF16)
            for j in range(n_groups):
                sl = slice(j * GW, (j + 1) * GW)
                items.append(dict(At=At_all[:, sl], Bt=Bt_all[:, sl], Kt=Kt_all[:, sl],
                                  Rt=Rt_all[:, sl], V=V_all[:, sl], g_row=ge[C - 1:C, sl]))

        for it in items:
            AR = jnp.concatenate([it["At"], it["Rt"]], axis=0)
            BK = jnp.concatenate([bdiag(it["Bt"]), bdiag(it["Kt"])], axis=0)
            sc = lax.dot_general(AR, BK, contract1, preferred_element_type=F32)
            it["L"] = jnp.where(strict, sc[:C, :GW], 0.0)
            it["akm"] = jnp.where(strict, sc[:C, GW:], 0.0).astype(BF16)
            it["rbm"] = jnp.where(incl, sc[C:, :GW], 0.0).astype(BF16)
            it["rkm"] = jnp.where(incl, sc[C:, GW:], 0.0).astype(BF16)

        blk = 2
        for it in items:
            it["T"] = eye + jnp.where(trow // blk == tcol // blk, it["L"], 0.0)
            it["Ld"] = bdiag(it["L"].astype(BF16))
        while blk < C:
            lvl = (trow // (2 * blk) == tcol // (2 * blk)) & (trow // blk != tcol // blk)
            if pending:
                items[0]["T"] = items[0]["T"] + pending.pop(0)()
            for it in items:
                it["Tb"] = it["T"].astype(BF16)
                it["P"] = jnp.where(lvl, _mm(it["Tb"], it["Ld"]), 0.0).astype(BF16)
            for it in items:
                it["T"] = it["T"] + _mm(it["P"], bdiag(it["Tb"]))
            blk *= 2

        for it in items:
            it["Vd"] = bdiag(it["V"])
            it["akv"] = _mm(it["akm"], it["Vd"]).astype(BF16)
        for it in items:
            au = _mm(it["T"].astype(BF16),
                     jnp.concatenate([bdiag(it["At"]), bdiag(it["akv"])], axis=1))
            it["A2"] = au[:, :GW].astype(BF16)
            it["U0"] = au[:, GW:].astype(BF16)
        for it in items:
            ry = _mm(it["rbm"], jnp.concatenate([bdiag(it["A2"]), bdiag(it["U0"])], axis=1))
            it["R2"] = (it["Rt"].astype(F32) + ry[:, :GW]).astype(BF16)
            it["Y0"] = ry[:, GW:] + _mm(it["rkm"], it["Vd"])
            it["Mq"] = (jnp.where(same_head, lax.dot_general(it["A2"], it["Bt"], contract0,
                                                             preferred_element_type=F32), 0.0)
                        * it["g_row"]).astype(BF16)
            it["Nq"] = jnp.where(same_head,
                                 lax.dot_general(jnp.concatenate([it["U0"], it["V"]], axis=0),
                                                 jnp.concatenate([it["Bt"], it["Kt"]], axis=0),
                                                 contract0, preferred_element_type=F32),
                                 0.0) * it["g_row"]
        for step in pending:
            step()
        return [{key: it[key] for key in ("R2", "Y0", "Mq", "Nq", "g_row")} for it in items]

    waves = [range(w0, min(w0 + RWKV_WAVE, G)) for w0 in range(0, G, RWKV_WAVE)]
    toks = [dict() for _ in waves]
    steps = [token_terms(slice(w[0] * C, (w[-1] + 1) * C), tok) for w, tok in zip(waves, toks)]
    for step in steps[0]:
        step()
    items = []
    for i, w in enumerate(waves):
        items += local_terms(w, toks[i], steps[i + 1] if i + 1 < len(waves) else [])
    gate = jnp.concatenate([tok["gate"] for tok in toks], axis=0)
    bonus = jnp.concatenate([tok["bonus"] for tok in toks], axis=0)

    y_chunks = []
    for c in range(G):
        ys = []
        for j in range(n_groups):
            it = items[c * n_groups + j]
            S0 = s_ref[j]
            Sb = S0.astype(BF16)
            ys.append(it["Y0"] + lax.dot_general(it["R2"], Sb, contract1,
                                                 preferred_element_type=F32))
            s_ref[j] = S0 * it["g_row"] + _mm(Sb, it["Mq"]) + it["Nq"]
        y_chunks.append(jnp.concatenate(ys, axis=1))
    Y = jnp.concatenate(y_chunks, axis=0)

    inv_n = 1.0 / N
    mean = head_sums(Y) * inv_n
    yc = Y - mean
    var = head_sums(yc * yc) * inv_n
    yn = yc * lax.rsqrt(var + RWKV_LN_EPS) * lng_ref[...] + lnb_ref[...]
    y_ref[0] = ((yn + bonus) * gate).astype(y_ref.dtype)


def _rwkv(p3, mu, w0, wup_pad, a0, aup_pad, gup, k_k, k_a, r_k, ln_g, ln_b, l):
    B, S, _ = p3.shape
    C = RWKV_CHUNK
    ts = min(RWKV_GROUP * C, S)
    group_w = RWKV_LANE_GROUP * RWKV_HEAD_DIM
    ltri = (jnp.arange(C)[None, :] <= jnp.arange(C)[:, None]).astype(BF16)
    head_of_lane = jnp.arange(group_w) // RWKV_HEAD_DIM
    bd = (head_of_lane[:, None] == head_of_lane[None, :]).astype(BF16)
    lay = lambda arr: _layer(arr, l)
    return pl.pallas_call(
        _rwkv_kernel,
        grid=(B, S // ts),
        in_specs=[pl.BlockSpec((1, ts, RWKV_IN), lambda b, s: (b, s, 0)), lay(mu), lay(w0),
                  lay(wup_pad), lay(a0), lay(aup_pad), lay(gup), lay(k_k), lay(k_a), lay(r_k),
                  lay(ln_g), lay(ln_b), _full((C, C)), _full((group_w, group_w))],
        out_specs=pl.BlockSpec((1, ts, RWKV_W), lambda b, s: (b, s, 0)),
        out_shape=jax.ShapeDtypeStruct((B, S, RWKV_W), BF16),
        scratch_shapes=[pltpu.VMEM((SUBLANES, RWKV_IN), F32),
                        pltpu.VMEM((RWKV_W // group_w, group_w, group_w), F32)],
        compiler_params=_params("parallel", "arbitrary"),
        name="rwkv",
    )(p3, mu, w0, wup_pad, a0, aup_pad, gup, k_k, k_a, r_k, ln_g, ln_b, ltri, bd)


def _shift_rows(x, d, fill):
    n = x.shape[0]
    if d % SUBLANES == 0:
        return jnp.concatenate([jnp.full((d,) + x.shape[1:], fill, x.dtype), x[:n - d]], axis=0)
    rows = lax.broadcasted_iota(jnp.int32, x.shape, 0)
    return jnp.where(rows < d, fill, pltpu.roll(x, d, 0))


def _scan_block(a, u):
    d = 1
    while d < a.shape[0]:
        u = u + a * _shift_rows(u, d, 0.0)
        a = a * _shift_rows(a, d, 1.0)
        d *= 2
    return a, u


def _lru_tile(p, cw_ref, cb_ref, wa_ref, ba_ref, wx_ref, bx_ref, lam_ref, xpad, hcarry, first,
              background=()):
    @pl.when(first)
    def _():
        xpad[0:SUBLANES, :] = jnp.zeros((SUBLANES, LRU_W), F32)
        hcarry[...] = jnp.zeros_like(hcarry)

    ts = p.shape[0]
    xb, gb = p[:, :LRU_W], p[:, LRU_W:]
    xpad[SUBLANES:, :] = xb
    cw = cw_ref[...]
    xc = cb_ref[...] + xb * cw[CONV_WIDTH - 1:CONV_WIDTH, :]
    for j in range(CONV_WIDTH - 1):
        lo = SUBLANES - (CONV_WIDTH - 1) + j
        xc = xc + xpad[lo:lo + ts, :] * cw[j:j + 1, :]
    xpad[0:SUBLANES, :] = xb[ts - SUBLANES:, :]
    rg = _sigmoid(_dot(xc, wa_ref[...]) + ba_ref[...])
    ig = _sigmoid(_dot(xc, wx_ref[...]) + bx_ref[...])
    lam = lam_ref[...]
    softplus_neg_lam = jnp.maximum(-lam, 0.0) + jnp.log(1.0 + jnp.exp(-jnp.abs(lam)))
    log_a = -LRU_C * rg * softplus_neg_lam
    a = jnp.exp(log_a)
    u = jnp.sqrt(1.0 - a * a) * (ig * xc)
    pending = list(background)
    n_row_blocks = ts // LRU_SCAN_ROWS
    stride = max(1, (LRU_W // LANES) * n_row_blocks // max(1, len(pending)))
    strips = []
    for ls in range(LRU_W // LANES):
        cols = slice(ls * LANES, (ls + 1) * LANES)
        h_in = hcarry[SUBLANES - 1:SUBLANES, cols]
        blocks = []
        for rt in range(n_row_blocks):
            if pending and (ls * n_row_blocks + rt) % stride == 0:
                h_in = h_in + pending.pop(0)()
            rs = slice(rt * LRU_SCAN_ROWS, (rt + 1) * LRU_SCAN_ROWS)
            a_blk, u_blk = _scan_block(a[rs, cols], u[rs, cols])
            h_blk = u_blk + a_blk * h_in
            h_in = h_blk[LRU_SCAN_ROWS - 1:, :]
            blocks.append(h_blk)
        strips.append(jnp.concatenate(blocks, axis=0))
    for thunk in pending:
        thunk()
    h = jnp.concatenate(strips, axis=1)
    hcarry[...] = h[ts - SUBLANES:, :]
    gelu = 0.5 * gb * (1.0 + jnp.tanh(math.sqrt(2.0 / math.pi) * (gb + 0.044715 * gb * gb * gb)))
    return h * gelu


def _mla_project(p, qn_ref, wuq_ref, kvn_ref, wuk_ref, wuv_ref, vt_o):
    cq = p[:, :Q_RANK]
    ckv = p[:, Q_RANK:Q_RANK + KV_RANK]
    qq = _dot(_rms(cq, qn_ref[...]), wuq_ref[...])
    ckv_n = _rms(ckv, kvn_ref[...]).astype(BF16)
    kn = _mm(ckv_n, wuk_ref[...])
    one_lane = (lax.broadcasted_iota(jnp.int32, ckv_n.shape, 1) == 0).astype(BF16)
    vt_o[0] = _dot_nt(wuv_ref[...], jnp.concatenate([ckv_n, one_lane], axis=1)).astype(BF16)
    return qq, kn


def _mla_rotate(p, qq, kn, cos_ref, sin_ref, qg_ref, qgs_ref, kg_ref, kgs_ref, q_o, k_o,
                background=()):
    W = MLA_HEADS * MLA_HEAD_PAD
    kr = p[:, Q_RANK + KV_RANK:Q_RANK + KV_RANK + LANES]
    kr_sw = p[:, Q_RANK + KV_RANK + LANES:]
    cosf = cos_ref[...]
    sinf = sin_ref[...]
    scale = MLA_QK ** -0.5 * math.log2(math.e)
    cq_tab = cosf * (qg_ref[...] * scale)
    sq_tab = sinf * (qgs_ref[...] * scale)
    ck_tab = cosf * kg_ref[...]
    sk_tab = sinf * kgs_ref[...]
    kr_rot = kr * ck_tab + kr_sw * sk_tab
    kr_ss = jnp.sum(kr * kr, axis=-1, keepdims=True)
    pending = list(background)
    for h in range(MLA_HEADS):
        sl = slice(h * MLA_HEAD_PAD, (h + 1) * MLA_HEAD_PAD)
        qh = qq[:, sl]
        q_rs = lax.rsqrt(jnp.sum(qh * qh, axis=-1, keepdims=True) / MLA_QK + 1e-6)
        if pending:
            q_rs = q_rs + pending.pop(0)()[:, 0:1]
        q_o[:, sl] = ((qh * cq_tab + qq[:, W + h * MLA_HEAD_PAD:W + (h + 1) * MLA_HEAD_PAD] * sq_tab)
                      * q_rs).astype(BF16)
        kh = kn[:, sl]
        k_rs = lax.rsqrt((jnp.sum(kh * kh, axis=-1, keepdims=True) + kr_ss) / MLA_QK + 1e-6)
        k_o[:, sl] = ((kh * ck_tab + kr_rot) * k_rs).astype(BF16)
    for thunk in pending:
        thunk()


def _mla_attn_kernel(q_ref, k_ref, vt_ref, o_ref, acc_ref, m_ref, l_ref, s_ref, p_ref, a_ref):
    qi = pl.program_id(1)
    tq = q_ref.shape[1]
    m_ref[...] = jnp.full(m_ref.shape, MASKED_SCORE, F32)
    l_ref[...] = jnp.zeros(l_ref.shape, F32)
    acc_ref[...] = jnp.zeros(acc_ref.shape, F32)

    def scores(j, masked, slot):
        start = pl.multiple_of(j * tq, tq)
        ties = []
        for h in range(MLA_HEADS):
            hs = slice(h * MLA_HEAD_PAD, (h + 1) * MLA_HEAD_PAD)
            kb = k_ref[0, pl.ds(start, tq), hs]
            st = lax.dot_general(kb, q_ref[0, :, hs], (((1,), (1,)), ((), ())),
                                 preferred_element_type=F32)
            if masked:
                kpos = lax.broadcasted_iota(jnp.int32, st.shape, 0)
                qpos = lax.broadcasted_iota(jnp.int32, st.shape, 1)
                st = jnp.where(kpos <= qpos, st, MASKED_SCORE)
            s_ref[slot, h] = st.astype(BF16)
            ties.append(jnp.minimum(jnp.abs(st[0:1, :]), 0.0))
        return ties

    def softmax(slot, ties=None):
        for h in range(MLA_HEADS):
            sb = s_ref[slot, h]
            m_old = m_ref[h:h + 1, :]
            if ties is not None:
                m_old = m_old + ties[h]
            m_new = jnp.maximum(m_old, jnp.max(sb, axis=0, keepdims=True).astype(F32))
            p_ref[slot, h] = jnp.exp2(sb - m_new.astype(BF16))
            a_ref[slot, h:h + 1, :] = jnp.exp2(m_old - m_new)
            m_ref[h:h + 1, :] = m_new

    def values(j, slot):
        start = pl.multiple_of(j * tq, tq)
        for h in range(MLA_HEADS):
            vs = slice(h * MLA_V, (h + 1) * MLA_V)
            ve = slice(h * MLA_V_EXT, (h + 1) * MLA_V_EXT)
            pvx = _mm(vt_ref[0, ve, pl.ds(start, tq)], p_ref[slot, h])
            alpha = a_ref[slot, h:h + 1, :]
            l_ref[h:h + 1, :] = alpha * l_ref[h:h + 1, :] + pvx[MLA_V:MLA_V + 1, :]
            acc_ref[vs, :] = alpha * acc_ref[vs, :] + pvx[:MLA_V, :]

    def tile(j, masked, slot):
        scores(j, masked, slot)
        softmax(slot)
        values(j, slot)

    def body(i, carry):
        scores(2 * i, False, 0)
        ties = scores(2 * i + 1, False, 1)
        softmax(0, ties)
        values(2 * i, 0)
        softmax(1)
        values(2 * i + 1, 1)
        return carry

    lax.fori_loop(0, qi // 2, body, 0)

    @pl.when(qi % 2 == 1)
    def _():
        tile(qi - 1, False, 0)

    tile(qi, True, 1)
    for h in range(MLA_HEADS):
        vs = slice(h * MLA_V, (h + 1) * MLA_V)
        acc_ref[vs, :] = acc_ref[vs, :] / l_ref[h:h + 1, :]
    o_ref[0] = acc_ref[...].T.astype(o_ref.dtype)


def _mla_attn(q, k, vt):
    B, S, W = q.shape
    tq = min(ATT_TILE, S)
    WV = MLA_HEADS * MLA_V
    return pl.pallas_call(
        _mla_attn_kernel,
        grid=(B, S // tq),
        in_specs=[pl.BlockSpec((1, tq, W), lambda b, i: (b, i, 0)),
                  pl.BlockSpec((1, S, W), lambda b, i: (b, 0, 0)),
                  pl.BlockSpec((1, MLA_HEADS * MLA_V_EXT, S), lambda b, i: (b, 0, 0))],
        out_specs=pl.BlockSpec((1, tq, WV), lambda b, i: (b, i, 0)),
        out_shape=jax.ShapeDtypeStruct((B, S, WV), BF16),
        scratch_shapes=[pltpu.VMEM((WV, tq), F32), pltpu.VMEM((MLA_HEADS, tq), F32),
                        pltpu.VMEM((MLA_HEADS, tq), F32), pltpu.VMEM((2, MLA_HEADS, tq, tq), BF16),
                        pltpu.VMEM((2, MLA_HEADS, tq, tq), BF16),
                        pltpu.VMEM((2, MLA_HEADS, tq), F32)],
        compiler_params=_params("parallel", "arbitrary"),
        name="mla_attn",
    )(q, k, vt)


def _merge_kernel(x_ref, ya_ref, yb_ref, yc_ref, g_ref, wg_ref, bg_ref, wb_ref, wo_ref, o_ref):
    x = x_ref[...]
    h = _rms(x, g_ref[...]).astype(BF16)
    merged = None
    for n, y_ref in enumerate((ya_ref, yb_ref, yc_ref)):
        sl = slice(n * D_MODEL, (n + 1) * D_MODEL)
        gate = _sigmoid(_mm(h, wg_ref[:, sl]) + bg_ref[:, sl])
        term = gate * _mm(y_ref[...], wb_ref[n])
        merged = term if merged is None else merged + term
    o_ref[...] = x + _dot(merged, wo_ref[...])


def _merge(x2, ya, yb, yc, g, w_gate, b_gate, w_branch, w_out, l):
    T = x2.shape[0]
    tm = min(TOK_TILE, T)
    lay = lambda arr: _layer(arr, l)
    return pl.pallas_call(
        _merge_kernel,
        grid=(T // tm,),
        in_specs=[_rows(tm, D_MODEL), _rows(tm, BRANCH_W), _rows(tm, BRANCH_W), _rows(tm, BRANCH_W),
                  lay(g), lay(w_gate), lay(b_gate), lay(w_branch), lay(w_out)],
        out_specs=_rows(tm, D_MODEL),
        out_shape=jax.ShapeDtypeStruct((T, D_MODEL), F32),
        compiler_params=_params("parallel"),
        name="merge",
    )(x2, ya, yb, yc, g, w_gate, b_gate, w_branch, w_out)


def _mem_kv_kernel(m_ref, g_ref, wkv_ref, kg_ref, k_o, v_o):
    h = _rms(m_ref[...], g_ref[...]).astype(BF16)
    kv = _mm(h, wkv_ref[...])
    for hd in range(XA_HEADS):
        sl = slice(hd * XA_HEAD_DIM, (hd + 1) * XA_HEAD_DIM)
        base = 2 * hd * XA_HEAD_DIM
        k_o[:, sl] = _rms(kv[:, base:base + XA_HEAD_DIM], kg_ref[...]).astype(BF16)
        v_o[:, sl] = kv[:, base + XA_HEAD_DIM:base + 2 * XA_HEAD_DIM].astype(BF16)


def _mem_kv(mem2, g, wkv, k_gain, l):
    M = mem2.shape[0]
    tm = min(TOK_TILE, M)
    lay = lambda arr: _layer(arr, l)
    return pl.pallas_call(
        _mem_kv_kernel,
        grid=(M // tm,),
        in_specs=[_rows(tm, D_MODEL), lay(g), lay(wkv), lay(k_gain)],
        out_specs=[_rows(tm, XA_W), _rows(tm, XA_W)],
        out_shape=[jax.ShapeDtypeStruct((M, XA_W), BF16)] * 2,
        compiler_params=_params("parallel"),
        name="mem_kv",
    )(mem2, g, wkv, k_gain)


def _xattn_kernel(x_ref, k_ref, v_ref, g_ref, wq_ref, qg_ref, wo_ref, o_ref):
    x = x_ref[0]
    h = _rms(x, g_ref[...]).astype(BF16)
    q = _mm(h, wq_ref[...])
    gain = qg_ref[...] * (XA_HEAD_DIM ** -0.5 * math.log2(math.e))
    sls = [slice(hd * XA_HEAD_DIM, (hd + 1) * XA_HEAD_DIM) for hd in range(XA_HEADS)]
    qhs = [_rms(q[:, sl], gain).astype(BF16) for sl in sls]
    scores = [_dot_nt(qh, k_ref[0, :, sl]) for qh, sl in zip(qhs, sls)]
    probs, inv_sums = [], []
    for s in scores:
        e = jnp.exp2(s - jnp.max(s, axis=-1, keepdims=True))
        inv_sums.append(1.0 / jnp.sum(e, axis=-1, keepdims=True))
        probs.append(e.astype(BF16))
    outs = [_mm(pr, v_ref[0, :, sl]) * inv for pr, sl, inv in zip(probs, sls, inv_sums)]
    o = jnp.concatenate(outs, axis=-1)
    o_ref[0] = x + _dot(o, wo_ref[...])


def _xattn(x3, k3, v3, g, wq, q_gain, wo, l):
    B, S, _ = x3.shape
    ts = min(TOK_TILE, S)
    M = k3.shape[1]
    lay = lambda arr: _layer(arr, l)
    return pl.pallas_call(
        _xattn_kernel,
        grid=(B, S // ts),
        in_specs=[pl.BlockSpec((1, ts, D_MODEL), lambda b, s: (b, s, 0)),
                  pl.BlockSpec((1, M, XA_W), lambda b, s: (b, 0, 0)),
                  pl.BlockSpec((1, M, XA_W), lambda b, s: (b, 0, 0)),
                  lay(g), lay(wq), lay(q_gain), lay(wo)],
        out_specs=pl.BlockSpec((1, ts, D_MODEL), lambda b, s: (b, s, 0)),
        out_shape=jax.ShapeDtypeStruct((B, S, D_MODEL), F32),
        compiler_params=_params("parallel", "parallel"),
        name="xattn",
    )(x3, k3, v3, g, wq, q_gain, wo)


FF_SPLIT = 2


def _ffn_kernel(x_ref, g_ref, w1_ref, w3_ref, w2_ref, o_ref):
    x = x_ref[...]
    h = _rms(x, g_ref[...]).astype(BF16)
    step = D_FF // FF_SPLIT
    acc = x
    for c in range(FF_SPLIT):
        sl = slice(c * step, (c + 1) * step)
        a = _mm(h, w1_ref[:, sl])
        b = _mm(h, w3_ref[:, sl])
        z = a * _sigmoid(a) * b
        acc = acc + _dot(z, w2_ref[sl, :])
    o_ref[...] = acc


def _ffn(x2, g, w1, w3, w2, l):
    T = x2.shape[0]
    tm = min(TOK_TILE, T)
    lay = lambda arr: _layer(arr, l)
    return pl.pallas_call(
        _ffn_kernel,
        grid=(T // tm,),
        in_specs=[_rows(tm, D_MODEL), lay(g), lay(w1), lay(w3), lay(w2)],
        out_specs=_rows(tm, D_MODEL),
        out_shape=jax.ShapeDtypeStruct((T, D_MODEL), F32),
        compiler_params=_params("parallel"),
        name="ffn",
    )(x2, g, w1, w3, w2)


def _block_diag(w):
    L, n, i, j = w.shape
    eye = jnp.eye(n, dtype=w.dtype)
    return jnp.einsum("lnij,nm->lnimj", w, eye).reshape(L, n * i, n * j)


def _pad_heads(w, heads, width, pad_to):
    lead = w.shape[:-1]
    w = w.reshape(lead + (heads, width))
    w = jnp.pad(w, [(0, 0)] * len(lead) + [(0, 0), (0, pad_to - width)])
    return w.reshape(lead + (heads * pad_to,))


def _rope_partner(w):
    half = MLA_ROPE // 2
    return jnp.concatenate([jnp.zeros_like(w[..., :MLA_NOPE]), w[..., MLA_NOPE + half:],
                            w[..., MLA_NOPE:MLA_NOPE + half]], axis=-1)


def _vec(v):
    return v.reshape(v.shape[0], 1, -1).astype(F32)


def kernel(x, mem, positions, norm_mix, norm_xattn, norm_mem, norm_ffn, w_in, b_gate, rwkv_mu, rwkv_w0, rwkv_w_up, rwkv_a0, rwkv_a_up, rwkv_g_up, rwkv_k_k, rwkv_k_a, rwkv_r_k, rwkv_ln_g, rwkv_ln_b, lru_conv_w, lru_conv_b, lru_wa, lru_ba, lru_wx, lru_bx, lru_lambda, mla_q_norm, mla_w_uq, mla_kv_norm, mla_w_ukv, mla_q_gain, mla_k_gain, w_branch, w_out, xa_w_q, xa_w_kv, xa_q_gain, xa_k_gain, xa_w_o, ffn_w1, ffn_w3, ffn_w2):
    B, S, D = x.shape
    T = B * S
    depth = w_in.shape[0]
    x2 = x.reshape(T, D)
    mem2 = mem.reshape(B * N_MEM, D)
    half = MLA_ROPE // 2

    inv_freq = ROPE_THETA ** (-jnp.arange(0, MLA_ROPE, 2, dtype=F32) / MLA_ROPE)
    per_row = LANES // half
    pos_rep = jnp.repeat(positions.reshape(T // per_row, per_row), half, axis=1)
    cos_c, sin_c, nsin_c = _rope_tables(pos_rep, jnp.tile(inv_freq, per_row).reshape(1, LANES))
    cos_c, sin_c, nsin_c = (t.reshape(T, half) for t in (cos_c, sin_c, nsin_c))
    ones = jnp.ones((T, MLA_NOPE), F32)
    zeros = jnp.zeros((T, MLA_NOPE), F32)
    cosf = jnp.concatenate([ones, cos_c, cos_c, ones[:, :LANES - MLA_QK]], axis=1)
    sinf = jnp.concatenate([zeros, nsin_c, sin_c, zeros[:, :LANES - MLA_QK]], axis=1)

    w_in_b = w_in[:, :, :MLA_OFF].astype(BF16)
    w_kr = w_in[:, :, MLA_OFF + Q_RANK + KV_RANK:GATE_OFF].astype(BF16)
    zero_cols = lambda n: jnp.zeros((depth, D, n), BF16)
    w_mla = jnp.concatenate(
        [w_in[:, :, MLA_OFF:MLA_OFF + Q_RANK + KV_RANK].astype(BF16), zero_cols(MLA_NOPE), w_kr,
         zero_cols(LANES - MLA_QK), zero_cols(MLA_NOPE), w_kr[:, :, half:], w_kr[:, :, :half],
         zero_cols(LANES - MLA_QK)], axis=2)
    w_gate = w_in[:, :, GATE_OFF:].astype(BF16)

    zeros_lora = jnp.zeros((depth, W_LORA, RWKV_W), F32)
    wup_pad = jnp.concatenate([rwkv_w_up, zeros_lora], axis=1).astype(BF16)
    aup_pad = jnp.concatenate([zeros_lora, rwkv_a_up], axis=1).astype(BF16)
    gup = rwkv_g_up.astype(BF16)
    r_k = rwkv_r_k.reshape(depth, RWKV_W)

    wa_bd = _block_diag(lru_wa).astype(BF16)
    wx_bd = _block_diag(lru_wx).astype(BF16)

    wuq_h = mla_w_uq.reshape(depth, Q_RANK, MLA_HEADS, MLA_QK)
    wuq2 = jnp.concatenate(
        [_pad_heads(mla_w_uq, MLA_HEADS, MLA_QK, MLA_HEAD_PAD),
         _pad_heads(_rope_partner(wuq_h).reshape(depth, Q_RANK, -1), MLA_HEADS, MLA_QK, MLA_HEAD_PAD)],
        axis=2).astype(BF16)
    wukv = mla_w_ukv.reshape(depth, KV_RANK, MLA_HEADS, MLA_NOPE + MLA_V)
    wuk = _pad_heads(wukv[..., :MLA_NOPE].reshape(depth, KV_RANK, -1), MLA_HEADS, MLA_NOPE,
                     MLA_HEAD_PAD).astype(BF16)
    wuvt = jnp.swapaxes(wukv[..., MLA_NOPE:], 1, 2).transpose(0, 1, 3, 2)
    wuvt = jnp.pad(wuvt, ((0, 0), (0, 0), (0, MLA_V_EXT - MLA_V), (0, KV_RANK)))
    wuvt = wuvt.at[:, :, MLA_V, KV_RANK].set(1.0)
    wuvt = wuvt.reshape(depth, MLA_HEADS * MLA_V_EXT, 2 * KV_RANK).astype(BF16)
    pad_gain = lambda gv: _vec(jnp.pad(gv, ((0, 0), (0, LANES - MLA_QK))))
    qg, qgs = pad_gain(mla_q_gain), pad_gain(_rope_partner(mla_q_gain))
    kg, kgs = pad_gain(mla_k_gain), pad_gain(_rope_partner(mla_k_gain))

    w_branch_b = w_branch.astype(BF16)
    w_out_b = w_out.astype(BF16)
    xa_wkv = xa_w_kv.astype(BF16)
    xa_wq = xa_w_q.astype(BF16)
    xa_wo = xa_w_o.astype(BF16)
    w1, w3, w2 = ffn_w1.astype(BF16), ffn_w3.astype(BF16), ffn_w2.astype(BF16)

    n_mix, n_xa, n_mem, n_ffn = _vec(norm_mix), _vec(norm_xattn), _vec(norm_mem), _vec(norm_ffn)
    rw = [_vec(t) for t in (rwkv_mu, rwkv_w0, rwkv_a0, rwkv_k_k, rwkv_k_a, r_k, rwkv_ln_g, rwkv_ln_b)]
    mu, w0, a0, k_k, k_a, r_kv, ln_g, ln_b = rw
    conv_b, ba, bx, lam = _vec(lru_conv_b), _vec(lru_ba), _vec(lru_bx), _vec(lru_lambda)
    q_norm, kv_norm = _vec(mla_q_norm), _vec(mla_kv_norm)
    b_gate_v, xa_qg, xa_kg = _vec(b_gate), _vec(xa_q_gain), _vec(xa_k_gain)

    lru_w = (lru_conv_w, conv_b, wa_bd, ba, wx_bd, bx, lam)
    mla_w = (q_norm, wuq2, kv_norm, wuk, wuvt, qg, qgs, kg, kgs)
    W = MLA_HEADS * MLA_HEAD_PAD
    for l in range(depth):
        p_rwkv, y_b, q, k, vt = _in_proj(x2, n_mix, w_in_b, w_mla, lru_w, mla_w, cosf, sinf, B, S, l)

        y_a = _rwkv(p_rwkv.reshape(B, S, RWKV_IN), mu, w0, wup_pad, a0, aup_pad, gup, k_k, k_a,
                    r_kv, ln_g, ln_b, l).reshape(T, RWKV_W)
        y_c = _mla_attn(q.reshape(B, S, W), k.reshape(B, S, W), vt).reshape(T, MLA_HEADS * MLA_V)

        x2 = _merge(x2, y_a, y_b, y_c, n_mix, w_gate, b_gate_v, w_branch_b, w_out_b, l)

        mk, mv = _mem_kv(mem2, n_mem, xa_wkv, xa_kg, l)
        x2 = _xattn(x2.reshape(B, S, D), mk.reshape(B, N_MEM, XA_W), mv.reshape(B, N_MEM, XA_W),
                    n_xa, xa_wq, xa_qg, xa_wo, l).reshape(T, D)

        x2 = _ffn(x2, n_ffn, w1, w3, w2, l)
    return x2.reshape(B, S, D)
```

```python
import functools
import math

import jax
import jax.numpy as jnp
from jax import lax
from jax.experimental import pallas as pl
from jax.experimental.pallas import tpu as pltpu

F32 = jnp.float32
BF16 = jnp.bfloat16

D_MODEL = 1024
N_MEM = 256
RWKV_HEADS = 8
RWKV_HEAD_DIM = 64
RWKV_W = RWKV_HEADS * RWKV_HEAD_DIM
W_LORA = 64
A_LORA = 64
G_LORA = 128
RWKV_IN = 3 * RWKV_W + W_LORA + A_LORA + G_LORA
RWKV_LN_EPS = RWKV_HEAD_DIM * 1e-5
LRU_BLOCKS = 8
LRU_W = 512
CONV_WIDTH = 4
LRU_C = 8.0
MLA_HEADS = 8
MLA_NOPE = 64
MLA_ROPE = 32
MLA_QK = MLA_NOPE + MLA_ROPE
MLA_V = 64
Q_RANK = 256
KV_RANK = 128
ROPE_THETA = 10000.0
N_BRANCH = 3
BRANCH_W = 512
XA_HEADS = 4
XA_HEAD_DIM = 128
XA_W = XA_HEADS * XA_HEAD_DIM
D_FF = -(-8 * D_MODEL // (3 * 256)) * 256
LRU_OFF = RWKV_IN
MLA_OFF = LRU_OFF + 2 * LRU_W
GATE_OFF = MLA_OFF + Q_RANK + KV_RANK + MLA_ROPE

LANES = 128
SUBLANES = 8
BF16_SUBLANES = 16
MXU_TILE = 256
VMEM_LIMIT = 56 * 1024 * 1024
MLA_HEAD_PAD = LANES
MLA_V_EXT = MLA_V + BF16_SUBLANES
MLA_PAD = Q_RANK + KV_RANK + 2 * LANES
MASKED_SCORE = -2.0 ** 100
RWKV_CHUNK = 64
RWKV_GROUP = 16
RWKV_WAVE = 4
RWKV_LANE_GROUP = MXU_TILE // RWKV_HEAD_DIM
LRU_SCAN_ROWS = 64
TOK_TILE = 512
ATT_TILE = 256


def _params(*sem):
    return pltpu.CompilerParams(dimension_semantics=sem, vmem_limit_bytes=VMEM_LIMIT)


def _mm(a, b):
    return jnp.dot(a, b, preferred_element_type=F32)


def _dot(a, b):
    return _mm(a.astype(BF16), b.astype(BF16))


def _dot_nt(a, b):
    return lax.dot_general(a.astype(BF16), b.astype(BF16), (((1,), (1,)), ((), ())),
                           preferred_element_type=F32)


def _sigmoid(x):
    return 0.5 * jnp.tanh(0.5 * x) + 0.5


def _rms(x, g, eps=1e-6):
    return x * lax.rsqrt(jnp.mean(x * x, axis=-1, keepdims=True) + eps) * g


def _full(shape):
    n = len(shape)
    return pl.BlockSpec(shape, lambda *_: (0,) * n)


def _layer(arr, l):
    tail = arr.shape[1:]
    return pl.BlockSpec((None,) + tail, lambda *_: (l,) + (0,) * len(tail))


def _rows(tm, n):
    return pl.BlockSpec((tm, n), lambda i: (i, 0))


def _rope_kernel(pos_ref, freq_ref, cos_o, sin_o, nsin_o):
    ang = pos_ref[...].astype(F32) * freq_ref[...]
    s = jnp.sin(ang)
    cos_o[...] = jnp.cos(ang)
    sin_o[...] = s
    nsin_o[...] = -s


def _rope_tables(pos_rep, freq_tile):
    R = pos_rep.shape[0]
    tm = min(TOK_TILE, R)
    out = jax.ShapeDtypeStruct((R, LANES), F32)
    return pl.pallas_call(
        _rope_kernel,
        grid=(R // tm,),
        in_specs=[_rows(tm, LANES), _full((1, LANES))],
        out_specs=[_rows(tm, LANES)] * 3,
        out_shape=[out] * 3,
        compiler_params=_params("parallel"),
        name="rope_tables",
    )(pos_rep, freq_tile)


def _in_proj_kernel(x_ref, g_ref, wrl_ref, wm_ref,
                    cw_ref, cb_ref, wa_ref, ba_ref, wx_ref, bx_ref, lam_ref,
                    cos_ref, sin_ref, qn_ref, wuq_ref, kvn_ref, wuk_ref, wuv_ref,
                    qg_ref, qgs_ref, kg_ref, kgs_ref,
                    or_ref, yb_ref, q_o, k_o, vt_o, xpad, hcarry, *, tiles_per_row):
    first = pl.program_id(0) % tiles_per_row == 0
    h = _rms(x_ref[...], g_ref[...]).astype(BF16)
    p_lru = _mm(h, wrl_ref[:, RWKV_IN:])

    def rwkv_cols(c0):
        def run():
            res = _mm(h, wrl_ref[:, c0:c0 + MXU_TILE])
            or_ref[:, c0:c0 + MXU_TILE] = res
            return jnp.minimum(jnp.abs(res[0:1, 0:LANES]), 0.0)
        return run

    mla = {}

    def mla_dots():
        p_mla = _mm(h, wm_ref[...])
        qq, kn = _mla_project(p_mla, qn_ref, wuq_ref, kvn_ref, wuk_ref, wuv_ref, vt_o)
        mla.update(p=p_mla, qq=qq, kn=kn)
        return jnp.minimum(jnp.abs(qq[0:1, 0:LANES] + kn[0:1, 0:LANES]), 0.0)

    rwkv_dots = [rwkv_cols(c0) for c0 in range(0, RWKV_IN, MXU_TILE)]
    n_lru = len(rwkv_dots)
    yb_ref[...] = _lru_tile(p_lru, cw_ref, cb_ref, wa_ref, ba_ref, wx_ref, bx_ref, lam_ref,
                            xpad, hcarry, first,
                            background=[mla_dots] + rwkv_dots[:n_lru]).astype(yb_ref.dtype)
    _mla_rotate(mla["p"], mla["qq"], mla["kn"], cos_ref, sin_ref, qg_ref, qgs_ref, kg_ref, kgs_ref,
                q_o, k_o, background=rwkv_dots[n_lru:])


def _in_proj(x2, g, w_in_b, w_mla, lru_w, mla_w, cosf, sinf, B, S, l):
    T = x2.shape[0]
    tm = min(TOK_TILE, S)
    per_row = S // tm
    W = MLA_HEADS * MLA_HEAD_PAD
    lay = lambda arr: _layer(arr, l)
    return pl.pallas_call(
        functools.partial(_in_proj_kernel, tiles_per_row=per_row),
        grid=(T // tm,),
        in_specs=[_rows(tm, D_MODEL), lay(g), lay(w_in_b), lay(w_mla)] + [lay(w) for w in lru_w]
                 + [_rows(tm, LANES), _rows(tm, LANES)] + [lay(w) for w in mla_w],
        out_specs=[_rows(tm, RWKV_IN), _rows(tm, LRU_W), _rows(tm, W), _rows(tm, W),
                   pl.BlockSpec((1, MLA_HEADS * MLA_V_EXT, tm),
                                lambda i: (i // per_row, 0, i % per_row))],
        out_shape=[jax.ShapeDtypeStruct((T, RWKV_IN), F32),
                   jax.ShapeDtypeStruct((T, LRU_W), BF16),
                   jax.ShapeDtypeStruct((T, W), BF16), jax.ShapeDtypeStruct((T, W), BF16),
                   jax.ShapeDtypeStruct((B, MLA_HEADS * MLA_V_EXT, S), BF16)],
        scratch_shapes=[pltpu.VMEM((tm + SUBLANES, LRU_W), F32), pltpu.VMEM((SUBLANES, LRU_W), F32)],
        compiler_params=_params("arbitrary"),
        name="in_proj",
    )(x2, g, w_in_b, w_mla, *lru_w, cosf, sinf, *mla_w)


def _cumsum_rows(ltri, x):
    hi = x.astype(BF16)
    r1 = x - hi.astype(F32)
    mid = r1.astype(BF16)
    lo = (r1 - mid.astype(F32)).astype(BF16)
    return _mm(ltri, hi) + _mm(ltri, mid) + _mm(ltri, lo)


def _rwkv_kernel(p_ref, mu_ref, w0_ref, wup_ref, a0_ref, aup_ref, gup_ref, kkw_ref, ka_ref,
                 rk_ref, lng_ref, lnb_ref, ltri_ref, bd_ref, y_ref, carry, s_ref):
    @pl.when(pl.program_id(1) == 0)
    def _():
        carry[...] = jnp.zeros_like(carry)
        s_ref[...] = jnp.zeros_like(s_ref)

    C, N = RWKV_CHUNK, RWKV_HEAD_DIM
    GW = RWKV_LANE_GROUP * N
    n_groups = RWKV_W // GW
    bd = bd_ref[...]

    def head_sums(x):
        return jnp.concatenate([_mm(x[:, j * GW:(j + 1) * GW].astype(BF16), bd)
                                for j in range(n_groups)], axis=1)

    p = p_ref[0]
    ts = p.shape[0]
    G = ts // C
    prow = lax.broadcasted_iota(jnp.int32, p.shape, 0)
    prev = jnp.where(prow == 0, carry[SUBLANES - 1:SUBLANES, :], pltpu.roll(p, 1, 0))
    carry[...] = p[ts - SUBLANES:, :]
    pm = p + (prev - p) * mu_ref[...]
    o1, o2, o3 = RWKV_W, 2 * RWKV_W, 3 * RWKV_W

    def exact_zero(x):
        return jnp.minimum(jnp.abs(x[0:1, 0:GW]), 0.0)

    def token_terms(rows, out):
        pr = pm[rows]
        r, k, v = pr[:, :o1], pr[:, o1:o2], pr[:, o2:o3]
        wa = pr[:, o3:o3 + W_LORA + A_LORA]
        gd = pr[:, o3 + W_LORA + A_LORA:]
        out.update(r=r, v=v)

        def decay():
            z = w0_ref[...] + _dot(jnp.tanh(wa), wup_ref[...])
            out["lw"] = -math.exp(-0.5) * _sigmoid(z)
            return exact_zero(out["lw"])

        def rates():
            out["a"] = _sigmoid(a0_ref[...] + _dot(wa, aup_ref[...]))
            out["gate"] = _dot(_sigmoid(gd), gup_ref[...])
            return exact_zero(out["a"] + out["gate"])

        def unit_keys():
            kk = k * kkw_ref[...]
            out["kk"] = kk / jnp.maximum(jnp.sqrt(head_sums(kk * kk)), 1e-12)
            return exact_zero(out["kk"])

        def keys():
            out["k2"] = k * (1.0 + (out["a"] - 1.0) * ka_ref[...])
            out["bonus"] = head_sums(r * out["k2"] * rk_ref[...]) * v
            out["kka"] = out["kk"] * out["a"]
            return exact_zero(out["bonus"] + out["kka"])

        return [decay, rates, unit_keys, keys]

    lane_head = lax.broadcasted_iota(jnp.int32, (C, GW), 1) // N
    head_sel = [lane_head == h for h in range(RWKV_LANE_GROUP)]

    def bdiag(x):
        zero = jnp.zeros_like(x)
        return jnp.concatenate([jnp.where(sel, x, zero) for sel in head_sel], axis=0)

    trow = lax.broadcasted_iota(jnp.int32, (C, GW), 0)
    tcol = lax.broadcasted_iota(jnp.int32, (C, GW), 1) % C
    strict = tcol < trow
    incl = tcol <= trow
    eye = (tcol == trow).astype(F32)
    vrow = lax.broadcasted_iota(jnp.int32, (GW, GW), 0) // N
    vcol = lax.broadcasted_iota(jnp.int32, (GW, GW), 1) // N
    same_head = vrow == vcol
    contract0 = (((0,), (0,)), ((), ()))
    contract1 = (((1,), (1,)), ((), ()))

    def local_terms(chunk_ids, tok, background):
        pending = list(background)
        items = []
        for c in chunk_ids:
            rows = slice((c - chunk_ids[0]) * C, (c - chunk_ids[0] + 1) * C)
            lw = tok["lw"][rows]
            cum = _cumsum_rows(ltri_ref[...], lw)
            ge = jnp.exp(cum)
            gi = jnp.exp(-cum)
            At_all = (-tok["kk"][rows] * jnp.exp(cum - lw)).astype(BF16)
            Bt_all = (tok["kka"][rows] * gi).astype(BF16)
            Kt_all = (tok["k2"][rows] * gi).astype(BF16)
            Rt_all = (tok["r"][rows] * ge).astype(BF16)
            V_all = tok["v"][rows].astype(BF16)
            for j in range(n_groups):
                sl = slice(j * GW, (j + 1) * GW)
                items.append(dict(At=At_all[:, sl], Bt=Bt_all[:, sl], Kt=Kt_all[:, sl],
                                  Rt=Rt_all[:, sl], V=V_all[:, sl], g_row=ge[C - 1:C, sl]))

        for it in items:
            AR = jnp.concatenate([it["At"], it["Rt"]], axis=0)
            BK = jnp.concatenate([bdiag(it["Bt"]), bdiag(it["Kt"])], axis=0)
            sc = lax.dot_general(AR, BK, contract1, preferred_element_type=F32)
            it["L"] = jnp.where(strict, sc[:C, :GW], 0.0)
            it["akm"] = jnp.where(strict, sc[:C, GW:], 0.0).astype(BF16)
            it["rbm"] = jnp.where(incl, sc[C:, :GW], 0.0).astype(BF16)
            it["rkm"] = jnp.where(incl, sc[C:, GW:], 0.0).astype(BF16)

        blk = 2
        for it in items:
            it["T"] = eye + jnp.where(trow // blk == tcol // blk, it["L"], 0.0)
            it["Ld"] = bdiag(it["L"].astype(BF16))
        while blk < C:
            lvl = (trow // (2 * blk) == tcol // (2 * blk)) & (trow // blk != tcol // blk)
            if pending:
                items[0]["T"] = items[0]["T"] + pending.pop(0)()
            for it in items:
                it["Tb"] = it["T"].astype(BF16)
                it["P"] = jnp.where(lvl, _mm(it["Tb"], it["Ld"]), 0.0).astype(BF16)
            for it in items:
                it["T"] = it["T"] + _mm(it["P"], bdiag(it["Tb"]))
            blk *= 2

        for it in items:
            it["Vd"] = bdiag(it["V"])
            it["akv"] = _mm(it["akm"], it["Vd"]).astype(BF16)
        for it in items:
            au = _mm(it["T"].astype(BF16),
                     jnp.concatenate([bdiag(it["At"]), bdiag(it["akv"])], axis=1))
            it["A2"] = au[:, :GW].astype(BF16)
            it["U0"] = au[:, GW:].astype(BF16)
        for it in items:
            ry = _mm(it["rbm"], jnp.concatenate([bdiag(it["A2"]), bdiag(it["U0"])], axis=1))
            it["R2"] = (it["Rt"].astype(F32) + ry[:, :GW]).astype(BF16)
            it["Y0"] = ry[:, GW:] + _mm(it["rkm"], it["Vd"])
            it["Mq"] = (jnp.where(same_head, lax.dot_general(it["A2"], it["Bt"], contract0,
                                                             preferred_element_type=F32), 0.0)
                        * it["g_row"]).astype(BF16)
            it["Nq"] = jnp.where(same_head,
                                 lax.dot_general(jnp.concatenate([it["U0"], it["V"]], axis=0),
                                                 jnp.concatenate([it["Bt"], it["Kt"]], axis=0),
                                                 contract0, preferred_element_type=F32),
                                 0.0) * it["g_row"]
        for step in pending:
            step()
        return [{key: it[key] for key in ("R2", "Y0", "Mq", "Nq", "g_row")} for it in items]

    waves = [range(w0, min(w0 + RWKV_WAVE, G)) for w0 in range(0, G, RWKV_WAVE)]
    toks = [dict() for _ in waves]
    steps = [token_terms(slice(w[0] * C, (w[-1] + 1) * C), tok) for w, tok in zip(waves, toks)]
    for step in steps[0]:
        step()
    items = []
    for i, w in enumerate(waves):
        items += local_terms(w, toks[i], steps[i + 1] if i + 1 < len(waves) else [])
    gate = jnp.concatenate([tok["gate"] for tok in toks], axis=0)
    bonus = jnp.concatenate([tok["bonus"] for tok in toks], axis=0)

    y_chunks = []
    for c in range(G):
        ys = []
        for j in range(n_groups):
            it = items[c * n_groups + j]
            S0 = s_ref[j]
            Sb = S0.astype(BF16)
            ys.append(it["Y0"] + lax.dot_general(it["R2"], Sb, contract1,
                                                 preferred_element_type=F32))
            s_ref[j] = S0 * it["g_row"] + _mm(Sb, it["Mq"]) + it["Nq"]
        y_chunks.append(jnp.concatenate(ys, axis=1))
    Y = jnp.concatenate(y_chunks, axis=0)

    inv_n = 1.0 / N
    mean = head_sums(Y) * inv_n
    yc = Y - mean
    var = head_sums(yc * yc) * inv_n
    yn = yc * lax.rsqrt(var + RWKV_LN_EPS) * lng_ref[...] + lnb_ref[...]
    y_ref[0] = ((yn + bonus) * gate).astype(y_ref.dtype)


def _rwkv(p3, mu, w0, wup_pad, a0, aup_pad, gup, k_k, k_a, r_k, ln_g, ln_b, l):
    B, S, _ = p3.shape
    C = RWKV_CHUNK
    ts = min(RWKV_GROUP * C, S)
    group_w = RWKV_LANE_GROUP * RWKV_HEAD_DIM
    ltri = (jnp.arange(C)[None, :] <= jnp.arange(C)[:, None]).astype(BF16)
    head_of_lane = jnp.arange(group_w) // RWKV_HEAD_DIM
    bd = (head_of_lane[:, None] == head_of_lane[None, :]).astype(BF16)
    lay = lambda arr: _layer(arr, l)
    return pl.pallas_call(
        _rwkv_kernel,
        grid=(B, S // ts),
        in_specs=[pl.BlockSpec((1, ts, RWKV_IN), lambda b, s: (b, s, 0)), lay(mu), lay(w0),
                  lay(wup_pad), lay(a0), lay(aup_pad), lay(gup), lay(k_k), lay(k_a), lay(r_k),
                  lay(ln_g), lay(ln_b), _full((C, C)), _full((group_w, group_w))],
        out_specs=pl.BlockSpec((1, ts, RWKV_W), lambda b, s: (b, s, 0)),
        out_shape=jax.ShapeDtypeStruct((B, S, RWKV_W), BF16),
        scratch_shapes=[pltpu.VMEM((SUBLANES, RWKV_IN), F32),
                        pltpu.VMEM((RWKV_W // group_w, group_w, group_w), F32)],
        compiler_params=_params("parallel", "arbitrary"),
        name="rwkv",
    )(p3, mu, w0, wup_pad, a0, aup_pad, gup, k_k, k_a, r_k, ln_g, ln_b, ltri, bd)


def _scan_block(a, u, h_in):
    n, lanes = a.shape
    groups = n // SUBLANES
    a3 = a.reshape(groups, SUBLANES, lanes)
    u3 = u.reshape(groups, SUBLANES, lanes)
    srow = lax.broadcasted_iota(jnp.int32, a3.shape, 1)
    d = 1
    while d < SUBLANES:
        u_sh = jnp.where(srow < d, 0.0, pltpu.roll(u3, d, 1))
        a_sh = jnp.where(srow < d, 1.0, pltpu.roll(a3, d, 1))
        u3 = u3 + a3 * u_sh
        a3 = a3 * a_sh
        d *= 2
    out = []
    for g in range(groups):
        hg = u3[g] + a3[g] * h_in
        h_in = hg[SUBLANES - 1:, :]
        out.append(hg)
    return jnp.concatenate(out, axis=0), h_in


def _lru_tile(p, cw_ref, cb_ref, wa_ref, ba_ref, wx_ref, bx_ref, lam_ref, xpad, hcarry, first,
              background=()):
    @pl.when(first)
    def _():
        xpad[0:SUBLANES, :] = jnp.zeros((SUBLANES, LRU_W), F32)
        hcarry[...] = jnp.zeros_like(hcarry)

    ts = p.shape[0]
    xb, gb = p[:, :LRU_W], p[:, LRU_W:]
    xpad[SUBLANES:, :] = xb
    cw = cw_ref[...]
    xc = cb_ref[...] + xb * cw[CONV_WIDTH - 1:CONV_WIDTH, :]
    for j in range(CONV_WIDTH - 1):
        lo = SUBLANES - (CONV_WIDTH - 1) + j
        xc = xc + xpad[lo:lo + ts, :] * cw[j:j + 1, :]
    xpad[0:SUBLANES, :] = xb[ts - SUBLANES:, :]
    rg = _sigmoid(_dot(xc, wa_ref[...]) + ba_ref[...])
    ig = _sigmoid(_dot(xc, wx_ref[...]) + bx_ref[...])
    lam = lam_ref[...]
    softplus_neg_lam = jnp.maximum(-lam, 0.0) + jnp.log(1.0 + jnp.exp(-jnp.abs(lam)))
    log_a = -LRU_C * rg * softplus_neg_lam
    a = jnp.exp(log_a)
    u = jnp.sqrt(1.0 - a * a) * (ig * xc)
    pending = list(background)
    n_row_blocks = ts // LRU_SCAN_ROWS
    stride = max(1, (LRU_W // LANES) * n_row_blocks // max(1, len(pending)))
    strips = []
    for ls in range(LRU_W // LANES):
        cols = slice(ls * LANES, (ls + 1) * LANES)
        h_in = hcarry[SUBLANES - 1:SUBLANES, cols]
        blocks = []
        for rt in range(n_row_blocks):
            if pending and (ls * n_row_blocks + rt) % stride == 0:
                h_in = h_in + pending.pop(0)()
            rs = slice(rt * LRU_SCAN_ROWS, (rt + 1) * LRU_SCAN_ROWS)
            h_blk, h_in = _scan_block(a[rs, cols], u[rs, cols], h_in)
            blocks.append(h_blk)
        strips.append(jnp.concatenate(blocks, axis=0))
    for thunk in pending:
        thunk()
    h = jnp.concatenate(strips, axis=1)
    hcarry[...] = h[ts - SUBLANES:, :]
    gelu = 0.5 * gb * (1.0 + jnp.tanh(math.sqrt(2.0 / math.pi) * (gb + 0.044715 * gb * gb * gb)))
    return h * gelu


def _mla_project(p, qn_ref, wuq_ref, kvn_ref, wuk_ref, wuv_ref, vt_o):
    cq = p[:, :Q_RANK]
    ckv = p[:, Q_RANK:Q_RANK + KV_RANK]
    qq = _dot(_rms(cq, qn_ref[...]), wuq_ref[...])
    ckv_n = _rms(ckv, kvn_ref[...]).astype(BF16)
    kn = _mm(ckv_n, wuk_ref[...])
    one_lane = (lax.broadcasted_iota(jnp.int32, ckv_n.shape, 1) == 0).astype(BF16)
    vt_o[0] = _dot_nt(wuv_ref[...], jnp.concatenate([ckv_n, one_lane], axis=1)).astype(BF16)
    return qq, kn


def _mla_rotate(p, qq, kn, cos_ref, sin_ref, qg_ref, qgs_ref, kg_ref, kgs_ref, q_o, k_o,
                background=()):
    W = MLA_HEADS * MLA_HEAD_PAD
    kr = p[:, Q_RANK + KV_RANK:Q_RANK + KV_RANK + LANES]
    kr_sw = p[:, Q_RANK + KV_RANK + LANES:]
    cosf = cos_ref[...]
    sinf = sin_ref[...]
    scale = MLA_QK ** -0.5 * math.log2(math.e)
    cq_tab = cosf * (qg_ref[...] * scale)
    sq_tab = sinf * (qgs_ref[...] * scale)
    ck_tab = cosf * kg_ref[...]
    sk_tab = sinf * kgs_ref[...]
    kr_rot = kr * ck_tab + kr_sw * sk_tab
    kr_ss = jnp.sum(kr * kr, axis=-1, keepdims=True)
    pending = list(background)
    for h in range(MLA_HEADS):
        sl = slice(h * MLA_HEAD_PAD, (h + 1) * MLA_HEAD_PAD)
        qh = qq[:, sl]
        q_rs = lax.rsqrt(jnp.sum(qh * qh, axis=-1, keepdims=True) / MLA_QK + 1e-6)
        if pending:
            q_rs = q_rs + pending.pop(0)()[:, 0:1]
        q_o[:, sl] = ((qh * cq_tab + qq[:, W + h * MLA_HEAD_PAD:W + (h + 1) * MLA_HEAD_PAD] * sq_tab)
                      * q_rs).astype(BF16)
        kh = kn[:, sl]
        k_rs = lax.rsqrt((jnp.sum(kh * kh, axis=-1, keepdims=True) + kr_ss) / MLA_QK + 1e-6)
        k_o[:, sl] = ((kh * ck_tab + kr_rot) * k_rs).astype(BF16)
    for thunk in pending:
        thunk()


def _mla_attn_kernel(q_ref, k_ref, vt_ref, o_ref, acc_ref, m_ref, l_ref, s_ref, p_ref, a_ref):
    qi = pl.program_id(1)
    tq = q_ref.shape[1]
    m_ref[...] = jnp.full(m_ref.shape, MASKED_SCORE, F32)
    l_ref[...] = jnp.zeros(l_ref.shape, F32)
    acc_ref[...] = jnp.zeros(acc_ref.shape, F32)

    def scores(j, masked, slot):
        start = pl.multiple_of(j * tq, tq)
        ties = []
        for h in range(MLA_HEADS):
            hs = slice(h * MLA_HEAD_PAD, (h + 1) * MLA_HEAD_PAD)
            kb = k_ref[0, pl.ds(start, tq), hs]
            st = lax.dot_general(kb, q_ref[0, :, hs], (((1,), (1,)), ((), ())),
                                 preferred_element_type=F32)
            if masked:
                kpos = lax.broadcasted_iota(jnp.int32, st.shape, 0)
                qpos = lax.broadcasted_iota(jnp.int32, st.shape, 1)
                st = jnp.where(kpos <= qpos, st, MASKED_SCORE)
            s_ref[slot, h] = st.astype(BF16)
            ties.append(jnp.minimum(jnp.abs(st[0:1, :]), 0.0))
        return ties

    def softmax(slot, ties=None):
        for h in range(MLA_HEADS):
            sb = s_ref[slot, h]
            m_old = m_ref[h:h + 1, :]
            if ties is not None:
                m_old = m_old + ties[h]
            m_new = jnp.maximum(m_old, jnp.max(sb, axis=0, keepdims=True).astype(F32))
            p_ref[slot, h] = jnp.exp2(sb - m_new.astype(BF16))
            a_ref[slot, h:h + 1, :] = jnp.exp2(m_old - m_new)
            m_ref[h:h + 1, :] = m_new

    def values(j, slot):
        start = pl.multiple_of(j * tq, tq)
        for h in range(MLA_HEADS):
            vs = slice(h * MLA_V, (h + 1) * MLA_V)
            ve = slice(h * MLA_V_EXT, (h + 1) * MLA_V_EXT)
            pvx = _mm(vt_ref[0, ve, pl.ds(start, tq)], p_ref[slot, h])
            alpha = a_ref[slot, h:h + 1, :]
            l_ref[h:h + 1, :] = alpha * l_ref[h:h + 1, :] + pvx[MLA_V:MLA_V + 1, :]
            acc_ref[vs, :] = alpha * acc_ref[vs, :] + pvx[:MLA_V, :]

    def pair(j, second_masked):
        scores(j, False, 0)
        ties = scores(j + 1, second_masked, 1)
        softmax(0, ties)
        values(j, 0)
        softmax(1)
        values(j + 1, 1)

    def body(i, carry):
        pair(2 * i, False)
        return carry

    lax.fori_loop(0, qi // 2, body, 0)

    @pl.when(qi % 2 == 1)
    def _():
        pair(qi - 1, True)

    @pl.when(qi % 2 == 0)
    def _():
        scores(qi, True, 1)
        softmax(1)
        values(qi, 1)

    for h in range(MLA_HEADS):
        vs = slice(h * MLA_V, (h + 1) * MLA_V)
        acc_ref[vs, :] = acc_ref[vs, :] / l_ref[h:h + 1, :]
    o_ref[0] = acc_ref[...].T.astype(o_ref.dtype)


def _mla_attn(q, k, vt):
    B, S, W = q.shape
    tq = min(ATT_TILE, S)
    WV = MLA_HEADS * MLA_V
    return pl.pallas_call(
        _mla_attn_kernel,
        grid=(B, S // tq),
        in_specs=[pl.BlockSpec((1, tq, W), lambda b, i: (b, i, 0)),
                  pl.BlockSpec((1, S, W), lambda b, i: (b, 0, 0)),
                  pl.BlockSpec((1, MLA_HEADS * MLA_V_EXT, S), lambda b, i: (b, 0, 0))],
        out_specs=pl.BlockSpec((1, tq, WV), lambda b, i: (b, i, 0)),
        out_shape=jax.ShapeDtypeStruct((B, S, WV), BF16),
        scratch_shapes=[pltpu.VMEM((WV, tq), F32), pltpu.VMEM((MLA_HEADS, tq), F32),
                        pltpu.VMEM((MLA_HEADS, tq), F32), pltpu.VMEM((2, MLA_HEADS, tq, tq), BF16),
                        pltpu.VMEM((2, MLA_HEADS, tq, tq), BF16),
                        pltpu.VMEM((2, MLA_HEADS, tq), F32)],
        compiler_params=_params("parallel", "arbitrary"),
        name="mla_attn",
    )(q, k, vt)


def _merge_kernel(x_ref, ya_ref, yb_ref, yc_ref, g_ref, wg_ref, bg_ref, wb_ref, wo_ref, o_ref):
    x = x_ref[...]
    h = _rms(x, g_ref[...]).astype(BF16)
    merged = None
    for n, y_ref in enumerate((ya_ref, yb_ref, yc_ref)):
        sl = slice(n * D_MODEL, (n + 1) * D_MODEL)
        gate = _sigmoid(_mm(h, wg_ref[:, sl]) + bg_ref[:, sl])
        term = gate * _mm(y_ref[...], wb_ref[n])
        merged = term if merged is None else merged + term
    o_ref[...] = x + _dot(merged, wo_ref[...])


def _merge(x2, ya, yb, yc, g, w_gate, b_gate, w_branch, w_out, l):
    T = x2.shape[0]
    tm = min(TOK_TILE, T)
    lay = lambda arr: _layer(arr, l)
    return pl.pallas_call(
        _merge_kernel,
        grid=(T // tm,),
        in_specs=[_rows(tm, D_MODEL), _rows(tm, BRANCH_W), _rows(tm, BRANCH_W), _rows(tm, BRANCH_W),
                  lay(g), lay(w_gate), lay(b_gate), lay(w_branch), lay(w_out)],
        out_specs=_rows(tm, D_MODEL),
        out_shape=jax.ShapeDtypeStruct((T, D_MODEL), F32),
        compiler_params=_params("parallel"),
        name="merge",
    )(x2, ya, yb, yc, g, w_gate, b_gate, w_branch, w_out)


def _mem_kv_kernel(m_ref, g_ref, wkv_ref, kg_ref, k_o, v_o):
    h = _rms(m_ref[...], g_ref[...]).astype(BF16)
    kv = _mm(h, wkv_ref[...])
    for hd in range(XA_HEADS):
        sl = slice(hd * XA_HEAD_DIM, (hd + 1) * XA_HEAD_DIM)
        base = 2 * hd * XA_HEAD_DIM
        k_o[:, sl] = _rms(kv[:, base:base + XA_HEAD_DIM], kg_ref[...]).astype(BF16)
        v_o[:, sl] = kv[:, base + XA_HEAD_DIM:base + 2 * XA_HEAD_DIM].astype(BF16)


def _mem_kv(mem2, g, wkv, k_gain, l):
    M = mem2.shape[0]
    tm = min(TOK_TILE, M)
    lay = lambda arr: _layer(arr, l)
    return pl.pallas_call(
        _mem_kv_kernel,
        grid=(M // tm,),
        in_specs=[_rows(tm, D_MODEL), lay(g), lay(wkv), lay(k_gain)],
        out_specs=[_rows(tm, XA_W), _rows(tm, XA_W)],
        out_shape=[jax.ShapeDtypeStruct((M, XA_W), BF16)] * 2,
        compiler_params=_params("parallel"),
        name="mem_kv",
    )(mem2, g, wkv, k_gain)


def _xattn_kernel(x_ref, k_ref, v_ref, g_ref, wq_ref, qg_ref, wo_ref, o_ref):
    x = x_ref[0]
    h = _rms(x, g_ref[...]).astype(BF16)
    q = _mm(h, wq_ref[...])
    gain = qg_ref[...] * (XA_HEAD_DIM ** -0.5 * math.log2(math.e))
    sls = [slice(hd * XA_HEAD_DIM, (hd + 1) * XA_HEAD_DIM) for hd in range(XA_HEADS)]
    qhs = [_rms(q[:, sl], gain).astype(BF16) for sl in sls]
    scores = [_dot_nt(qh, k_ref[0, :, sl]) for qh, sl in zip(qhs, sls)]
    probs, inv_sums = [], []
    for s in scores:
        e = jnp.exp2(s - jnp.max(s, axis=-1, keepdims=True))
        inv_sums.append(1.0 / jnp.sum(e, axis=-1, keepdims=True))
        probs.append(e.astype(BF16))
    outs = [_mm(pr, v_ref[0, :, sl]) * inv for pr, sl, inv in zip(probs, sls, inv_sums)]
    o = jnp.concatenate(outs, axis=-1)
    o_ref[0] = x + _dot(o, wo_ref[...])


def _xattn(x3, k3, v3, g, wq, q_gain, wo, l):
    B, S, _ = x3.shape
    ts = min(TOK_TILE, S)
    M = k3.shape[1]
    lay = lambda arr: _layer(arr, l)
    return pl.pallas_call(
        _xattn_kernel,
        grid=(B, S // ts),
        in_specs=[pl.BlockSpec((1, ts, D_MODEL), lambda b, s: (b, s, 0)),
                  pl.BlockSpec((1, M, XA_W), lambda b, s: (b, 0, 0)),
                  pl.BlockSpec((1, M, XA_W), lambda b, s: (b, 0, 0)),
                  lay(g), lay(wq), lay(q_gain), lay(wo)],
        out_specs=pl.BlockSpec((1, ts, D_MODEL), lambda b, s: (b, s, 0)),
        out_shape=jax.ShapeDtypeStruct((B, S, D_MODEL), F32),
        compiler_params=_params("parallel", "parallel"),
        name="xattn",
    )(x3, k3, v3, g, wq, q_gain, wo)


FF_SPLIT = 2


def _ffn_kernel(x_ref, g_ref, w1_ref, w3_ref, w2_ref, o_ref):
    x = x_ref[...]
    h = _rms(x, g_ref[...]).astype(BF16)
    step = D_FF // FF_SPLIT
    acc = x
    for c in range(FF_SPLIT):
        sl = slice(c * step, (c + 1) * step)
        a = _mm(h, w1_ref[:, sl])
        b = _mm(h, w3_ref[:, sl])
        z = a * _sigmoid(a) * b
        acc = acc + _dot(z, w2_ref[sl, :])
    o_ref[...] = acc


def _ffn(x2, g, w1, w3, w2, l):
    T = x2.shape[0]
    tm = min(TOK_TILE, T)
    lay = lambda arr: _layer(arr, l)
    return pl.pallas_call(
        _ffn_kernel,
        grid=(T // tm,),
        in_specs=[_rows(tm, D_MODEL), lay(g), lay(w1), lay(w3), lay(w2)],
        out_specs=_rows(tm, D_MODEL),
        out_shape=jax.ShapeDtypeStruct((T, D_MODEL), F32),
        compiler_params=_params("parallel"),
        name="ffn",
    )(x2, g, w1, w3, w2)


def _block_diag(w):
    L, n, i, j = w.shape
    eye = jnp.eye(n, dtype=w.dtype)
    return jnp.einsum("lnij,nm->lnimj", w, eye).reshape(L, n * i, n * j)


def _pad_heads(w, heads, width, pad_to):
    lead = w.shape[:-1]
    w = w.reshape(lead + (heads, width))
    w = jnp.pad(w, [(0, 0)] * len(lead) + [(0, 0), (0, pad_to - width)])
    return w.reshape(lead + (heads * pad_to,))


def _rope_partner(w):
    half = MLA_ROPE // 2
    return jnp.concatenate([jnp.zeros_like(w[..., :MLA_NOPE]), w[..., MLA_NOPE + half:],
                            w[..., MLA_NOPE:MLA_NOPE + half]], axis=-1)


def _vec(v):
    return v.reshape(v.shape[0], 1, -1).astype(F32)


def kernel(x, mem, positions, norm_mix, norm_xattn, norm_mem, norm_ffn, w_in, b_gate, rwkv_mu, rwkv_w0, rwkv_w_up, rwkv_a0, rwkv_a_up, rwkv_g_up, rwkv_k_k, rwkv_k_a, rwkv_r_k, rwkv_ln_g, rwkv_ln_b, lru_conv_w, lru_conv_b, lru_wa, lru_ba, lru_wx, lru_bx, lru_lambda, mla_q_norm, mla_w_uq, mla_kv_norm, mla_w_ukv, mla_q_gain, mla_k_gain, w_branch, w_out, xa_w_q, xa_w_kv, xa_q_gain, xa_k_gain, xa_w_o, ffn_w1, ffn_w3, ffn_w2):
    B, S, D = x.shape
    T = B * S
    depth = w_in.shape[0]
    x2 = x.reshape(T, D)
    mem2 = mem.reshape(B * N_MEM, D)
    half = MLA_ROPE // 2

    inv_freq = ROPE_THETA ** (-jnp.arange(0, MLA_ROPE, 2, dtype=F32) / MLA_ROPE)
    per_row = LANES // half
    pos_rep = jnp.repeat(positions.reshape(T // per_row, per_row), half, axis=1)
    cos_c, sin_c, nsin_c = _rope_tables(pos_rep, jnp.tile(inv_freq, per_row).reshape(1, LANES))
    cos_c, sin_c, nsin_c = (t.reshape(T, half) for t in (cos_c, sin_c, nsin_c))
    ones = jnp.ones((T, MLA_NOPE), F32)
    zeros = jnp.zeros((T, MLA_NOPE), F32)
    cosf = jnp.concatenate([ones, cos_c, cos_c, ones[:, :LANES - MLA_QK]], axis=1)
    sinf = jnp.concatenate([zeros, nsin_c, sin_c, zeros[:, :LANES - MLA_QK]], axis=1)

    w_in_b = w_in[:, :, :MLA_OFF].astype(BF16)
    w_kr = w_in[:, :, MLA_OFF + Q_RANK + KV_RANK:GATE_OFF].astype(BF16)
    zero_cols = lambda n: jnp.zeros((depth, D, n), BF16)
    w_mla = jnp.concatenate(
        [w_in[:, :, MLA_OFF:MLA_OFF + Q_RANK + KV_RANK].astype(BF16), zero_cols(MLA_NOPE), w_kr,
         zero_cols(LANES - MLA_QK), zero_cols(MLA_NOPE), w_kr[:, :, half:], w_kr[:, :, :half],
         zero_cols(LANES - MLA_QK)], axis=2)
    w_gate = w_in[:, :, GATE_OFF:].astype(BF16)

    zeros_lora = jnp.zeros((depth, W_LORA, RWKV_W), F32)
    wup_pad = jnp.concatenate([rwkv_w_up, zeros_lora], axis=1).astype(BF16)
    aup_pad = jnp.concatenate([zeros_lora, rwkv_a_up], axis=1).astype(BF16)
    gup = rwkv_g_up.astype(BF16)
    r_k = rwkv_r_k.reshape(depth, RWKV_W)

    wa_bd = _block_diag(lru_wa).astype(BF16)
    wx_bd = _block_diag(lru_wx).astype(BF16)

    wuq_h = mla_w_uq.reshape(depth, Q_RANK, MLA_HEADS, MLA_QK)
    wuq2 = jnp.concatenate(
        [_pad_heads(mla_w_uq, MLA_HEADS, MLA_QK, MLA_HEAD_PAD),
         _pad_heads(_rope_partner(wuq_h).reshape(depth, Q_RANK, -1), MLA_HEADS, MLA_QK, MLA_HEAD_PAD)],
        axis=2).astype(BF16)
    wukv = mla_w_ukv.reshape(depth, KV_RANK, MLA_HEADS, MLA_NOPE + MLA_V)
    wuk = _pad_heads(wukv[..., :MLA_NOPE].reshape(depth, KV_RANK, -1), MLA_HEADS, MLA_NOPE,
                     MLA_HEAD_PAD).astype(BF16)
    wuvt = jnp.swapaxes(wukv[..., MLA_NOPE:], 1, 2).transpose(0, 1, 3, 2)
    wuvt = jnp.pad(wuvt, ((0, 0), (0, 0), (0, MLA_V_EXT - MLA_V), (0, KV_RANK)))
    wuvt = wuvt.at[:, :, MLA_V, KV_RANK].set(1.0)
    wuvt = wuvt.reshape(depth, MLA_HEADS * MLA_V_EXT, 2 * KV_RANK).astype(BF16)
    pad_gain = lambda gv: _vec(jnp.pad(gv, ((0, 0), (0, LANES - MLA_QK))))
    qg, qgs = pad_gain(mla_q_gain), pad_gain(_rope_partner(mla_q_gain))
    kg, kgs = pad_gain(mla_k_gain), pad_gain(_rope_partner(mla_k_gain))

    w_branch_b = w_branch.astype(BF16)
    w_out_b = w_out.astype(BF16)
    xa_wkv = xa_w_kv.astype(BF16)
    xa_wq = xa_w_q.astype(BF16)
    xa_wo = xa_w_o.astype(BF16)
    w1, w3, w2 = ffn_w1.astype(BF16), ffn_w3.astype(BF16), ffn_w2.astype(BF16)

    n_mix, n_xa, n_mem, n_ffn = _vec(norm_mix), _vec(norm_xattn), _vec(norm_mem), _vec(norm_ffn)
    rw = [_vec(t) for t in (rwkv_mu, rwkv_w0, rwkv_a0, rwkv_k_k, rwkv_k_a, r_k, rwkv_ln_g, rwkv_ln_b)]
    mu, w0, a0, k_k, k_a, r_kv, ln_g, ln_b = rw
    conv_b, ba, bx, lam = _vec(lru_conv_b), _vec(lru_ba), _vec(lru_bx), _vec(lru_lambda)
    q_norm, kv_norm = _vec(mla_q_norm), _vec(mla_kv_norm)
    b_gate_v, xa_qg, xa_kg = _vec(b_gate), _vec(xa_q_gain), _vec(xa_k_gain)

    lru_w = (lru_conv_w, conv_b, wa_bd, ba, wx_bd, bx, lam)
    mla_w = (q_norm, wuq2, kv_norm, wuk, wuvt, qg, qgs, kg, kgs)
    W = MLA_HEADS * MLA_HEAD_PAD
    for l in range(depth):
        p_rwkv, y_b, q, k, vt = _in_proj(x2, n_mix, w_in_b, w_mla, lru_w, mla_w, cosf, sinf, B, S, l)

        y_a = _rwkv(p_rwkv.reshape(B, S, RWKV_IN), mu, w0, wup_pad, a0, aup_pad, gup, k_k, k_a,
                    r_kv, ln_g, ln_b, l).reshape(T, RWKV_W)
        y_c = _mla_attn(q.reshape(B, S, W), k.reshape(B, S, W), vt).reshape(T, MLA_HEADS * MLA_V)

        x2 = _merge(x2, y_a, y_b, y_c, n_mix, w_gate, b_gate_v, w_branch_b, w_out_b, l)

        mk, mv = _mem_kv(mem2, n_mem, xa_wkv, xa_kg, l)
        x2 = _xattn(x2.reshape(B, S, D), mk.reshape(B, N_MEM, XA_W), mv.reshape(B, N_MEM, XA_W),
                    n_xa, xa_wq, xa_qg, xa_wo, l).reshape(T, D)

        x2 = _ffn(x2, n_ffn, w1, w3, w2, l)
    return x2.reshape(B, S, D)
```

```python
import functools
import math

import jax
import jax.numpy as jnp
from jax import lax
from jax.experimental import pallas as pl
from jax.experimental.pallas import tpu as pltpu

F32 = jnp.float32
BF16 = jnp.bfloat16

D_MODEL = 1024
N_MEM = 256
RWKV_HEADS = 8
RWKV_HEAD_DIM = 64
RWKV_W = RWKV_HEADS * RWKV_HEAD_DIM
W_LORA = 64
A_LORA = 64
G_LORA = 128
RWKV_IN = 3 * RWKV_W + W_LORA + A_LORA + G_LORA
RWKV_LN_EPS = RWKV_HEAD_DIM * 1e-5
LRU_BLOCKS = 8
LRU_W = 512
CONV_WIDTH = 4
LRU_C = 8.0
MLA_HEADS = 8
MLA_NOPE = 64
MLA_ROPE = 32
MLA_QK = MLA_NOPE + MLA_ROPE
MLA_V = 64
Q_RANK = 256
KV_RANK = 128
ROPE_THETA = 10000.0
N_BRANCH = 3
BRANCH_W = 512
XA_HEADS = 4
XA_HEAD_DIM = 128
XA_W = XA_HEADS * XA_HEAD_DIM
D_FF = -(-8 * D_MODEL // (3 * 256)) * 256
LRU_OFF = RWKV_IN
MLA_OFF = LRU_OFF + 2 * LRU_W
GATE_OFF = MLA_OFF + Q_RANK + KV_RANK + MLA_ROPE

LANES = 128
SUBLANES = 8
BF16_SUBLANES = 16
MXU_TILE = 256
VMEM_LIMIT = 56 * 1024 * 1024
MLA_HEAD_PAD = LANES
MLA_V_EXT = MLA_V + BF16_SUBLANES
MLA_PAD = Q_RANK + KV_RANK + 2 * LANES
MASKED_SCORE = -2.0 ** 100
RWKV_CHUNK = 64
RWKV_GROUP = 16
RWKV_WAVE = 4
RWKV_LANE_GROUP = MXU_TILE // RWKV_HEAD_DIM
LRU_SCAN_ROWS = 64
TOK_TILE = 512
ATT_TILE = 256


def _params(*sem):
    return pltpu.CompilerParams(dimension_semantics=sem, vmem_limit_bytes=VMEM_LIMIT)


def _mm(a, b):
    return jnp.dot(a, b, preferred_element_type=F32)


def _dot(a, b):
    return _mm(a.astype(BF16), b.astype(BF16))


def _dot_nt(a, b):
    return lax.dot_general(a.astype(BF16), b.astype(BF16), (((1,), (1,)), ((), ())),
                           preferred_element_type=F32)


def _sigmoid(x):
    return 0.5 * jnp.tanh(0.5 * x) + 0.5


def _rms(x, g, eps=1e-6):
    return x * lax.rsqrt(jnp.mean(x * x, axis=-1, keepdims=True) + eps) * g


def _full(shape):
    n = len(shape)
    return pl.BlockSpec(shape, lambda *_: (0,) * n)


def _layer(arr, l):
    tail = arr.shape[1:]
    return pl.BlockSpec((None,) + tail, lambda *_: (l,) + (0,) * len(tail))


def _rows(tm, n):
    return pl.BlockSpec((tm, n), lambda i: (i, 0))


def _rope_kernel(pos_ref, freq_ref, cos_o, sin_o, nsin_o):
    ang = pos_ref[...].astype(F32) * freq_ref[...]
    s = jnp.sin(ang)
    cos_o[...] = jnp.cos(ang)
    sin_o[...] = s
    nsin_o[...] = -s


def _rope_tables(pos_rep, freq_tile):
    R = pos_rep.shape[0]
    tm = min(TOK_TILE, R)
    out = jax.ShapeDtypeStruct((R, LANES), F32)
    return pl.pallas_call(
        _rope_kernel,
        grid=(R // tm,),
        in_specs=[_rows(tm, LANES), _full((1, LANES))],
        out_specs=[_rows(tm, LANES)] * 3,
        out_shape=[out] * 3,
        compiler_params=_params("parallel"),
        name="rope_tables",
    )(pos_rep, freq_tile)


def _in_proj_kernel(x_ref, g_ref, wrl_ref, wm_ref,
                    cw_ref, cb_ref, wa_ref, ba_ref, wx_ref, bx_ref, lam_ref,
                    cos_ref, sin_ref, qn_ref, wuq_ref, kvn_ref, wuk_ref, wuv_ref,
                    qg_ref, qgs_ref, kg_ref, kgs_ref,
                    or_ref, yb_ref, q_o, k_o, vt_o, xpad, hcarry, *, tiles_per_row):
    first = pl.program_id(0) % tiles_per_row == 0
    h = _rms(x_ref[...], g_ref[...]).astype(BF16)
    p_lru = _mm(h, wrl_ref[:, RWKV_IN:])

    def rwkv_cols(c0):
        def run():
            res = _mm(h, wrl_ref[:, c0:c0 + MXU_TILE])
            or_ref[:, c0:c0 + MXU_TILE] = res
            return jnp.minimum(jnp.abs(res[0:1, 0:LANES]), 0.0)
        return run

    mla = {}

    def mla_dots():
        p_mla = _mm(h, wm_ref[...])
        qq, kn = _mla_project(p_mla, qn_ref, wuq_ref, kvn_ref, wuk_ref, wuv_ref, vt_o)
        mla.update(p=p_mla, qq=qq, kn=kn)
        return jnp.minimum(jnp.abs(qq[0:1, 0:LANES] + kn[0:1, 0:LANES]), 0.0)

    rwkv_dots = [rwkv_cols(c0) for c0 in range(0, RWKV_IN, MXU_TILE)]
    n_lru = len(rwkv_dots)
    yb_ref[...] = _lru_tile(p_lru, cw_ref, cb_ref, wa_ref, ba_ref, wx_ref, bx_ref, lam_ref,
                            xpad, hcarry, first,
                            background=[mla_dots] + rwkv_dots[:n_lru]).astype(yb_ref.dtype)
    _mla_rotate(mla["p"], mla["qq"], mla["kn"], cos_ref, sin_ref, qg_ref, qgs_ref, kg_ref, kgs_ref,
                q_o, k_o, background=rwkv_dots[n_lru:])


def _in_proj(x2, g, w_in_b, w_mla, lru_w, mla_w, cosf, sinf, B, S, l):
    T = x2.shape[0]
    tm = min(TOK_TILE, S)
    per_row = S // tm
    W = MLA_HEADS * MLA_HEAD_PAD
    lay = lambda arr: _layer(arr, l)
    return pl.pallas_call(
        functools.partial(_in_proj_kernel, tiles_per_row=per_row),
        grid=(T // tm,),
        in_specs=[_rows(tm, D_MODEL), lay(g), lay(w_in_b), lay(w_mla)] + [lay(w) for w in lru_w]
                 + [_rows(tm, LANES), _rows(tm, LANES)] + [lay(w) for w in mla_w],
        out_specs=[_rows(tm, RWKV_IN), _rows(tm, LRU_W), _rows(tm, W), _rows(tm, W),
                   pl.BlockSpec((1, MLA_HEADS * MLA_V_EXT, tm),
                                lambda i: (i // per_row, 0, i % per_row))],
        out_shape=[jax.ShapeDtypeStruct((T, RWKV_IN), F32),
                   jax.ShapeDtypeStruct((T, LRU_W), BF16),
                   jax.ShapeDtypeStruct((T, W), BF16), jax.ShapeDtypeStruct((T, W), BF16),
                   jax.ShapeDtypeStruct((B, MLA_HEADS * MLA_V_EXT, S), BF16)],
        scratch_shapes=[pltpu.VMEM((tm + SUBLANES, LRU_W), F32), pltpu.VMEM((SUBLANES, LRU_W), F32)],
        compiler_params=_params("arbitrary"),
        name="in_proj",
    )(x2, g, w_in_b, w_mla, *lru_w, cosf, sinf, *mla_w)


def _cumsum_rows(ltri, x):
    hi = x.astype(BF16)
    r1 = x - hi.astype(F32)
    mid = r1.astype(BF16)
    lo = (r1 - mid.astype(F32)).astype(BF16)
    return _mm(ltri, hi) + _mm(ltri, mid) + _mm(ltri, lo)


def _rwkv_kernel(p_ref, mu_ref, w0_ref, wup_ref, a0_ref, aup_ref, gup_ref, kkw_ref, ka_ref,
                 rk_ref, lng_ref, lnb_ref, ltri_ref, bd_ref, y_ref, carry, s_ref):
    @pl.when(pl.program_id(1) == 0)
    def _():
        carry[...] = jnp.zeros_like(carry)
        s_ref[...] = jnp.zeros_like(s_ref)

    C, N = RWKV_CHUNK, RWKV_HEAD_DIM
    GW = RWKV_LANE_GROUP * N
    n_groups = RWKV_W // GW
    bd = bd_ref[...]

    def head_sums(x):
        return jnp.concatenate([_mm(x[:, j * GW:(j + 1) * GW].astype(BF16), bd)
                                for j in range(n_groups)], axis=1)

    p = p_ref[0]
    ts = p.shape[0]
    G = ts // C
    prow = lax.broadcasted_iota(jnp.int32, p.shape, 0)
    prev = jnp.where(prow == 0, carry[SUBLANES - 1:SUBLANES, :], pltpu.roll(p, 1, 0))
    carry[...] = p[ts - SUBLANES:, :]
    pm = p + (prev - p) * mu_ref[...]
    o1, o2, o3 = RWKV_W, 2 * RWKV_W, 3 * RWKV_W

    def exact_zero(x):
        return jnp.minimum(jnp.abs(x[0:1, 0:GW]), 0.0)

    def token_terms(rows, out):
        pr = pm[rows]
        r, k, v = pr[:, :o1], pr[:, o1:o2], pr[:, o2:o3]
        wa = pr[:, o3:o3 + W_LORA + A_LORA]
        gd = pr[:, o3 + W_LORA + A_LORA:]
        out.update(r=r, v=v)

        def decay():
            z = w0_ref[...] + _dot(jnp.tanh(wa), wup_ref[...])
            out["lw"] = -math.exp(-0.5) * _sigmoid(z)
            return exact_zero(out["lw"])

        def rates():
            out["a"] = _sigmoid(a0_ref[...] + _dot(wa, aup_ref[...]))
            out["gate"] = _dot(_sigmoid(gd), gup_ref[...])
            return exact_zero(out["a"] + out["gate"])

        def unit_keys():
            kk = k * kkw_ref[...]
            out["kk"] = kk / jnp.maximum(jnp.sqrt(head_sums(kk * kk)), 1e-12)
            return exact_zero(out["kk"])

        def keys():
            out["k2"] = k * (1.0 + (out["a"] - 1.0) * ka_ref[...])
            out["bonus"] = head_sums(r * out["k2"] * rk_ref[...]) * v
            out["kka"] = out["kk"] * out["a"]
            return exact_zero(out["bonus"] + out["kka"])

        return [decay, rates, unit_keys, keys]

    lane_head = lax.broadcasted_iota(jnp.int32, (C, GW), 1) // N
    head_sel = [lane_head == h for h in range(RWKV_LANE_GROUP)]

    def bdiag(x):
        zero = jnp.zeros_like(x)
        return jnp.concatenate([jnp.where(sel, x, zero) for sel in head_sel], axis=0)

    trow = lax.broadcasted_iota(jnp.int32, (C, GW), 0)
    tcol = lax.broadcasted_iota(jnp.int32, (C, GW), 1) % C
    strict = tcol < trow
    incl = tcol <= trow
    eye = (tcol == trow).astype(F32)
    vrow = lax.broadcasted_iota(jnp.int32, (GW, GW), 0) // N
    vcol = lax.broadcasted_iota(jnp.int32, (GW, GW), 1) // N
    same_head = vrow == vcol
    contract0 = (((0,), (0,)), ((), ()))
    contract1 = (((1,), (1,)), ((), ()))

    def local_terms(chunk_ids, tok, background):
        pending = list(background)
        items = []
        for c in chunk_ids:
            rows = slice((c - chunk_ids[0]) * C, (c - chunk_ids[0] + 1) * C)
            lw = tok["lw"][rows]
            cum = _cumsum_rows(ltri_ref[...], lw)
            ge = jnp.exp(cum)
            gi = jnp.exp(-cum)
            At_all = (-tok["kk"][rows] * jnp.exp(cum - lw)).astype(BF16)
            Bt_all = (tok["kka"][rows] * gi).astype(BF16)
            Kt_all = (tok["k2"][rows] * gi).astype(BF16)
            Rt_all = (tok["r"][rows] * ge).astype(BF16)
            V_all = tok["v"][rows].astype(BF16)
            for j in range(n_groups):
                sl = slice(j * GW, (j + 1) * GW)
                items.append(dict(At=At_all[:, sl], Bt=Bt_all[:, sl], Kt=Kt_all[:, sl],
                                  Rt=Rt_all[:, sl], V=V_all[:, sl], g_row=ge[C - 1:C, sl]))

        for it in items:
            AR = jnp.concatenate([it["At"], it["Rt"]], axis=0)
            BK = jnp.concatenate([bdiag(it["Bt"]), bdiag(it["Kt"])], axis=0)
            sc = lax.dot_general(AR, BK, contract1, preferred_element_type=F32)
            it["L"] = jnp.where(strict, sc[:C, :GW], 0.0)
            it["akm"] = jnp.where(strict, sc[:C, GW:], 0.0).astype(BF16)
            it["rbm"] = jnp.where(incl, sc[C:, :GW], 0.0).astype(BF16)
            it["rkm"] = jnp.where(incl, sc[C:, GW:], 0.0).astype(BF16)

        blk = 2
        for it in items:
            it["T"] = eye + jnp.where(trow // blk == tcol // blk, it["L"], 0.0)
            it["Ld"] = bdiag(it["L"].astype(BF16))
        while blk < C:
            lvl = (trow // (2 * blk) == tcol // (2 * blk)) & (trow // blk != tcol // blk)
            if pending:
                items[0]["T"] = items[0]["T"] + pending.pop(0)()
            for it in items:
                it["Tb"] = it["T"].astype(BF16)
                it["P"] = jnp.where(lvl, _mm(it["Tb"], it["Ld"]), 0.0).astype(BF16)
            for it in items:
                it["T"] = it["T"] + _mm(it["P"], bdiag(it["Tb"]))
            blk *= 2

        for it in items:
            it["Vd"] = bdiag(it["V"])
            it["akv"] = _mm(it["akm"], it["Vd"]).astype(BF16)
        for it in items:
            au = _mm(it["T"].astype(BF16),
                     jnp.concatenate([bdiag(it["At"]), bdiag(it["akv"])], axis=1))
            it["A2"] = au[:, :GW].astype(BF16)
            it["U0"] = au[:, GW:].astype(BF16)
        for it in items:
            ry = _mm(it["rbm"], jnp.concatenate([bdiag(it["A2"]), bdiag(it["U0"])], axis=1))
            it["R2"] = (it["Rt"].astype(F32) + ry[:, :GW]).astype(BF16)
            it["Y0"] = ry[:, GW:] + _mm(it["rkm"], it["Vd"])
            it["Mq"] = (jnp.where(same_head, lax.dot_general(it["A2"], it["Bt"], contract0,
                                                             preferred_element_type=F32), 0.0)
                        * it["g_row"]).astype(BF16)
            it["Nq"] = jnp.where(same_head,
                                 lax.dot_general(jnp.concatenate([it["U0"], it["V"]], axis=0),
                                                 jnp.concatenate([it["Bt"], it["Kt"]], axis=0),
                                                 contract0, preferred_element_type=F32),
                                 0.0) * it["g_row"]
        for step in pending:
            step()
        return [{key: it[key] for key in ("R2", "Y0", "Mq", "Nq", "g_row")} for it in items]

    waves = [range(w0, min(w0 + RWKV_WAVE, G)) for w0 in range(0, G, RWKV_WAVE)]
    toks = [dict() for _ in waves]
    steps = [token_terms(slice(w[0] * C, (w[-1] + 1) * C), tok) for w, tok in zip(waves, toks)]
    for step in steps[0]:
        step()
    items = []
    for i, w in enumerate(waves):
        items += local_terms(w, toks[i], steps[i + 1] if i + 1 < len(waves) else [])
    gate = jnp.concatenate([tok["gate"] for tok in toks], axis=0)
    bonus = jnp.concatenate([tok["bonus"] for tok in toks], axis=0)

    y_chunks = []
    for c in range(G):
        ys = []
        for j in range(n_groups):
            it = items[c * n_groups + j]
            S0 = s_ref[j]
            Sb = S0.astype(BF16)
            ys.append(it["Y0"] + lax.dot_general(it["R2"], Sb, contract1,
                                                 preferred_element_type=F32))
            s_ref[j] = S0 * it["g_row"] + _mm(Sb, it["Mq"]) + it["Nq"]
        y_chunks.append(jnp.concatenate(ys, axis=1))
    Y = jnp.concatenate(y_chunks, axis=0)

    inv_n = 1.0 / N
    mean = head_sums(Y) * inv_n
    yc = Y - mean
    var = head_sums(yc * yc) * inv_n
    yn = yc * lax.rsqrt(var + RWKV_LN_EPS) * lng_ref[...] + lnb_ref[...]
    y_ref[0] = ((yn + bonus) * gate).astype(y_ref.dtype)


def _rwkv(p3, mu, w0, wup_pad, a0, aup_pad, gup, k_k, k_a, r_k, ln_g, ln_b, l):
    B, S, _ = p3.shape
    C = RWKV_CHUNK
    ts = min(RWKV_GROUP * C, S)
    group_w = RWKV_LANE_GROUP * RWKV_HEAD_DIM
    ltri = (jnp.arange(C)[None, :] <= jnp.arange(C)[:, None]).astype(BF16)
    head_of_lane = jnp.arange(group_w) // RWKV_HEAD_DIM
    bd = (head_of_lane[:, None] == head_of_lane[None, :]).astype(BF16)
    lay = lambda arr: _layer(arr, l)
    return pl.pallas_call(
        _rwkv_kernel,
        grid=(B, S // ts),
        in_specs=[pl.BlockSpec((1, ts, RWKV_IN), lambda b, s: (b, s, 0)), lay(mu), lay(w0),
                  lay(wup_pad), lay(a0), lay(aup_pad), lay(gup), lay(k_k), lay(k_a), lay(r_k),
                  lay(ln_g), lay(ln_b), _full((C, C)), _full((group_w, group_w))],
        out_specs=pl.BlockSpec((1, ts, RWKV_W), lambda b, s: (b, s, 0)),
        out_shape=jax.ShapeDtypeStruct((B, S, RWKV_W), BF16),
        scratch_shapes=[pltpu.VMEM((SUBLANES, RWKV_IN), F32),
                        pltpu.VMEM((RWKV_W // group_w, group_w, group_w), F32)],
        compiler_params=_params("parallel", "arbitrary"),
        name="rwkv",
    )(p3, mu, w0, wup_pad, a0, aup_pad, gup, k_k, k_a, r_k, ln_g, ln_b, ltri, bd)


def _scan_block(a, u, h_in):
    n, lanes = a.shape
    groups = n // SUBLANES
    a3 = a.reshape(groups, SUBLANES, lanes)
    u3 = u.reshape(groups, SUBLANES, lanes)
    srow = lax.broadcasted_iota(jnp.int32, a3.shape, 1)
    d = 1
    while d < SUBLANES:
        u_sh = jnp.where(srow < d, 0.0, pltpu.roll(u3, d, 1))
        a_sh = jnp.where(srow < d, 1.0, pltpu.roll(a3, d, 1))
        u3 = u3 + a3 * u_sh
        a3 = a3 * a_sh
        d *= 2
    out = []
    for g in range(groups):
        hg = u3[g] + a3[g] * h_in
        h_in = hg[SUBLANES - 1:, :]
        out.append(hg)
    return jnp.concatenate(out, axis=0), h_in


def _lru_tile(p, cw_ref, cb_ref, wa_ref, ba_ref, wx_ref, bx_ref, lam_ref, xpad, hcarry, first,
              background=()):
    @pl.when(first)
    def _():
        xpad[0:SUBLANES, :] = jnp.zeros((SUBLANES, LRU_W), F32)
        hcarry[...] = jnp.zeros_like(hcarry)

    ts = p.shape[0]
    xb, gb = p[:, :LRU_W], p[:, LRU_W:]
    xpad[SUBLANES:, :] = xb
    cw = cw_ref[...]
    xc = cb_ref[...] + xb * cw[CONV_WIDTH - 1:CONV_WIDTH, :]
    for j in range(CONV_WIDTH - 1):
        lo = SUBLANES - (CONV_WIDTH - 1) + j
        xc = xc + xpad[lo:lo + ts, :] * cw[j:j + 1, :]
    xpad[0:SUBLANES, :] = xb[ts - SUBLANES:, :]
    rg = _sigmoid(_dot(xc, wa_ref[...]) + ba_ref[...])
    ig = _sigmoid(_dot(xc, wx_ref[...]) + bx_ref[...])
    lam = lam_ref[...]
    softplus_neg_lam = jnp.maximum(-lam, 0.0) + jnp.log(1.0 + jnp.exp(-jnp.abs(lam)))
    log_a = -LRU_C * rg * softplus_neg_lam
    a = jnp.exp(log_a)
    u = jnp.sqrt(1.0 - a * a) * (ig * xc)
    pending = list(background)
    n_row_blocks = ts // LRU_SCAN_ROWS
    stride = max(1, (LRU_W // LANES) * n_row_blocks // max(1, len(pending)))
    strips = []
    for ls in range(LRU_W // LANES):
        cols = slice(ls * LANES, (ls + 1) * LANES)
        h_in = hcarry[SUBLANES - 1:SUBLANES, cols]
        blocks = []
        for rt in range(n_row_blocks):
            if pending and (ls * n_row_blocks + rt) % stride == 0:
                h_in = h_in + pending.pop(0)()
            rs = slice(rt * LRU_SCAN_ROWS, (rt + 1) * LRU_SCAN_ROWS)
            h_blk, h_in = _scan_block(a[rs, cols], u[rs, cols], h_in)
            blocks.append(h_blk)
        strips.append(jnp.concatenate(blocks, axis=0))
    for thunk in pending:
        thunk()
    h = jnp.concatenate(strips, axis=1)
    hcarry[...] = h[ts - SUBLANES:, :]
    gelu = 0.5 * gb * (1.0 + jnp.tanh(math.sqrt(2.0 / math.pi) * (gb + 0.044715 * gb * gb * gb)))
    return h * gelu


def _mla_project(p, qn_ref, wuq_ref, kvn_ref, wuk_ref, wuv_ref, vt_o):
    cq = p[:, :Q_RANK]
    ckv = p[:, Q_RANK:Q_RANK + KV_RANK]
    qq = _dot(_rms(cq, qn_ref[...]), wuq_ref[...])
    ckv_n = _rms(ckv, kvn_ref[...]).astype(BF16)
    kn = _mm(ckv_n, wuk_ref[...])
    one_lane = (lax.broadcasted_iota(jnp.int32, ckv_n.shape, 1) == 0).astype(BF16)
    vt_o[0] = _dot_nt(wuv_ref[...], jnp.concatenate([ckv_n, one_lane], axis=1)).astype(BF16)
    return qq, kn


def _mla_rotate(p, qq, kn, cos_ref, sin_ref, qg_ref, qgs_ref, kg_ref, kgs_ref, q_o, k_o,
                background=()):
    W = MLA_HEADS * MLA_HEAD_PAD
    kr = p[:, Q_RANK + KV_RANK:Q_RANK + KV_RANK + LANES]
    kr_sw = p[:, Q_RANK + KV_RANK + LANES:]
    cosf = cos_ref[...]
    sinf = sin_ref[...]
    scale = MLA_QK ** -0.5 * math.log2(math.e)
    cq_tab = cosf * (qg_ref[...] * scale)
    sq_tab = sinf * (qgs_ref[...] * scale)
    ck_tab = cosf * kg_ref[...]
    sk_tab = sinf * kgs_ref[...]
    kr_rot = kr * ck_tab + kr_sw * sk_tab
    kr_ss = jnp.sum(kr * kr, axis=-1, keepdims=True)
    pending = list(background)
    for h in range(MLA_HEADS):
        sl = slice(h * MLA_HEAD_PAD, (h + 1) * MLA_HEAD_PAD)
        qh = qq[:, sl]
        q_rs = lax.rsqrt(jnp.sum(qh * qh, axis=-1, keepdims=True) / MLA_QK + 1e-6)
        if pending:
            q_rs = q_rs + pending.pop(0)()[:, 0:1]
        q_o[:, sl] = ((qh * cq_tab + qq[:, W + h * MLA_HEAD_PAD:W + (h + 1) * MLA_HEAD_PAD] * sq_tab)
                      * q_rs).astype(BF16)
        kh = kn[:, sl]
        k_rs = lax.rsqrt((jnp.sum(kh * kh, axis=-1, keepdims=True) + kr_ss) / MLA_QK + 1e-6)
        k_o[:, sl] = ((kh * ck_tab + kr_rot) * k_rs).astype(BF16)
    for thunk in pending:
        thunk()


def _mla_attn_kernel(q_ref, k_ref, vt_ref, o_ref, acc_ref, m_ref, l_ref, s_ref, p_ref, a_ref):
    qi = pl.program_id(1)
    tq = q_ref.shape[1]
    m_ref[...] = jnp.full(m_ref.shape, MASKED_SCORE, F32)
    l_ref[...] = jnp.zeros(l_ref.shape, F32)
    acc_ref[...] = jnp.zeros(acc_ref.shape, F32)

    def scores(j, masked, slot):
        start = pl.multiple_of(j * tq, tq)
        ties = []
        for h in range(MLA_HEADS):
            hs = slice(h * MLA_HEAD_PAD, (h + 1) * MLA_HEAD_PAD)
            kb = k_ref[0, pl.ds(start, tq), hs]
            st = lax.dot_general(kb, q_ref[0, :, hs], (((1,), (1,)), ((), ())),
                                 preferred_element_type=F32)
            if masked:
                kpos = lax.broadcasted_iota(jnp.int32, st.shape, 0)
                qpos = lax.broadcasted_iota(jnp.int32, st.shape, 1)
                st = jnp.where(kpos <= qpos, st, MASKED_SCORE)
            s_ref[slot, h] = st.astype(BF16)
            ties.append(jnp.minimum(jnp.abs(st[0:1, :]), 0.0))
        return ties

    def softmax(slot, ties=None):
        for h in range(MLA_HEADS):
            sb = s_ref[slot, h]
            m_old = m_ref[h:h + 1, :]
            if ties is not None:
                m_old = m_old + ties[h]
            m_new = jnp.maximum(m_old, jnp.max(sb, axis=0, keepdims=True).astype(F32))
            p_ref[slot, h] = jnp.exp2(sb - m_new.astype(BF16))
            a_ref[slot, h:h + 1, :] = jnp.exp2(m_old - m_new)
            m_ref[h:h + 1, :] = m_new

    def values(j, slot):
        start = pl.multiple_of(j * tq, tq)
        for h in range(MLA_HEADS):
            vs = slice(h * MLA_V, (h + 1) * MLA_V)
            ve = slice(h * MLA_V_EXT, (h + 1) * MLA_V_EXT)
            pvx = _mm(vt_ref[0, ve, pl.ds(start, tq)], p_ref[slot, h])
            alpha = a_ref[slot, h:h + 1, :]
            l_ref[h:h + 1, :] = alpha * l_ref[h:h + 1, :] + pvx[MLA_V:MLA_V + 1, :]
            acc_ref[vs, :] = alpha * acc_ref[vs, :] + pvx[:MLA_V, :]

    def pair(j, second_masked):
        scores(j, False, 0)
        ties = scores(j + 1, second_masked, 1)
        softmax(0, ties)
        values(j, 0)
        softmax(1)
        values(j + 1, 1)

    def body(i, carry):
        pair(2 * i, False)
        return carry

    lax.fori_loop(0, qi // 2, body, 0)

    @pl.when(qi % 2 == 1)
    def _():
        pair(qi - 1, True)

    @pl.when(qi % 2 == 0)
    def _():
        scores(qi, True, 1)
        softmax(1)
        values(qi, 1)

    for h in range(MLA_HEADS):
        vs = slice(h * MLA_V, (h + 1) * MLA_V)
        acc_ref[vs, :] = acc_ref[vs, :] / l_ref[h:h + 1, :]
    o_ref[0] = acc_ref[...].T.astype(o_ref.dtype)


def _mla_attn(q, k, vt):
    B, S, W = q.shape
    tq = min(ATT_TILE, S)
    WV = MLA_HEADS * MLA_V
    return pl.pallas_call(
        _mla_attn_kernel,
        grid=(B, S // tq),
        in_specs=[pl.BlockSpec((1, tq, W), lambda b, i: (b, i, 0)),
                  pl.BlockSpec((1, S, W), lambda b, i: (b, 0, 0)),
                  pl.BlockSpec((1, MLA_HEADS * MLA_V_EXT, S), lambda b, i: (b, 0, 0))],
        out_specs=pl.BlockSpec((1, tq, WV), lambda b, i: (b, i, 0)),
        out_shape=jax.ShapeDtypeStruct((B, S, WV), BF16),
        scratch_shapes=[pltpu.VMEM((WV, tq), F32), pltpu.VMEM((MLA_HEADS, tq), F32),
                        pltpu.VMEM((MLA_HEADS, tq), F32), pltpu.VMEM((2, MLA_HEADS, tq, tq), BF16),
                        pltpu.VMEM((2, MLA_HEADS, tq, tq), BF16),
                        pltpu.VMEM((2, MLA_HEADS, tq), F32)],
        compiler_params=_params("parallel", "arbitrary"),
        name="mla_attn",
    )(q, k, vt)


def _merge_kernel(x_ref, ya_ref, yb_ref, yc_ref, g_ref, wg_ref, bg_ref, wb_ref, wo_ref, o_ref):
    x = x_ref[...]
    h = _rms(x, g_ref[...]).astype(BF16)
    merged = None
    for n, y_ref in enumerate((ya_ref, yb_ref, yc_ref)):
        sl = slice(n * D_MODEL, (n + 1) * D_MODEL)
        gate = _sigmoid(_mm(h, wg_ref[:, sl]) + bg_ref[:, sl])
        term = gate * _mm(y_ref[...], wb_ref[n])
        merged = term if merged is None else merged + term
    o_ref[...] = x + _dot(merged, wo_ref[...])


def _merge(x2, ya, yb, yc, g, w_gate, b_gate, w_branch, w_out, l):
    T = x2.shape[0]
    tm = min(TOK_TILE, T)
    lay = lambda arr: _layer(arr, l)
    return pl.pallas_call(
        _merge_kernel,
        grid=(T // tm,),
        in_specs=[_rows(tm, D_MODEL), _rows(tm, BRANCH_W), _rows(tm, BRANCH_W), _rows(tm, BRANCH_W),
                  lay(g), lay(w_gate), lay(b_gate), lay(w_branch), lay(w_out)],
        out_specs=_rows(tm, D_MODEL),
        out_shape=jax.ShapeDtypeStruct((T, D_MODEL), F32),
        compiler_params=_params("parallel"),
        name="merge",
    )(x2, ya, yb, yc, g, w_gate, b_gate, w_branch, w_out)


def _mem_kv_kernel(m_ref, g_ref, wkv_ref, kg_ref, k_o, v_o):
    h = _rms(m_ref[...], g_ref[...]).astype(BF16)
    kv = _mm(h, wkv_ref[...])
    for hd in range(XA_HEADS):
        sl = slice(hd * XA_HEAD_DIM, (hd + 1) * XA_HEAD_DIM)
        base = 2 * hd * XA_HEAD_DIM
        k_o[:, sl] = _rms(kv[:, base:base + XA_HEAD_DIM], kg_ref[...]).astype(BF16)
        v_o[:, sl] = kv[:, base + XA_HEAD_DIM:base + 2 * XA_HEAD_DIM].astype(BF16)


def _mem_kv(mem2, g, wkv, k_gain, l):
    M = mem2.shape[0]
    tm = min(TOK_TILE, M)
    lay = lambda arr: _layer(arr, l)
    return pl.pallas_call(
        _mem_kv_kernel,
        grid=(M // tm,),
        in_specs=[_rows(tm, D_MODEL), lay(g), lay(wkv), lay(k_gain)],
        out_specs=[_rows(tm, XA_W), _rows(tm, XA_W)],
        out_shape=[jax.ShapeDtypeStruct((M, XA_W), BF16)] * 2,
        compiler_params=_params("parallel"),
        name="mem_kv",
    )(mem2, g, wkv, k_gain)


def _xattn_kernel(x_ref, k_ref, v_ref, g_ref, wq_ref, qg_ref, wo_ref, o_ref):
    x = x_ref[0]
    h = _rms(x, g_ref[...]).astype(BF16)
    q = _mm(h, wq_ref[...])
    gain = qg_ref[...] * (XA_HEAD_DIM ** -0.5 * math.log2(math.e))
    sls = [slice(hd * XA_HEAD_DIM, (hd + 1) * XA_HEAD_DIM) for hd in range(XA_HEADS)]
    qhs = [_rms(q[:, sl], gain).astype(BF16) for sl in sls]
    scores = [_dot_nt(qh, k_ref[0, :, sl]) for qh, sl in zip(qhs, sls)]
    probs, inv_sums = [], []
    for s in scores:
        e = jnp.exp2(s - jnp.max(s, axis=-1, keepdims=True))
        inv_sums.append(1.0 / jnp.sum(e, axis=-1, keepdims=True))
        probs.append(e.astype(BF16))
    outs = [_mm(pr, v_ref[0, :, sl]) * inv for pr, sl, inv in zip(probs, sls, inv_sums)]
    o = jnp.concatenate(outs, axis=-1)
    o_ref[0] = x + _dot(o, wo_ref[...])


def _xattn(x3, k3, v3, g, wq, q_gain, wo, l):
    B, S, _ = x3.shape
    ts = min(TOK_TILE, S)
    M = k3.shape[1]
    lay = lambda arr: _layer(arr, l)
    return pl.pallas_call(
        _xattn_kernel,
        grid=(B, S // ts),
        in_specs=[pl.BlockSpec((1, ts, D_MODEL), lambda b, s: (b, s, 0)),
                  pl.BlockSpec((1, M, XA_W), lambda b, s: (b, 0, 0)),
                  pl.BlockSpec((1, M, XA_W), lambda b, s: (b, 0, 0)),
                  lay(g), lay(wq), lay(q_gain), lay(wo)],
        out_specs=pl.BlockSpec((1, ts, D_MODEL), lambda b, s: (b, s, 0)),
        out_shape=jax.ShapeDtypeStruct((B, S, D_MODEL), F32),
        compiler_params=_params("parallel", "parallel"),
        name="xattn",
    )(x3, k3, v3, g, wq, q_gain, wo)


FF_SPLIT = 11


def _ffn_kernel(x_ref, g_ref, w1_ref, w3_ref, w2_ref, o_ref):
    x = x_ref[...]
    h = _rms(x, g_ref[...]).astype(BF16)
    step = D_FF // FF_SPLIT
    acc = x
    for c in range(FF_SPLIT):
        sl = slice(c * step, (c + 1) * step)
        a = _mm(h, w1_ref[:, sl])
        b = _mm(h, w3_ref[:, sl])
        z = a * _sigmoid(a) * b
        acc = acc + _dot(z, w2_ref[sl, :])
    o_ref[...] = acc


def _ffn(x2, g, w1, w3, w2, l):
    T = x2.shape[0]
    tm = min(TOK_TILE, T)
    lay = lambda arr: _layer(arr, l)
    return pl.pallas_call(
        _ffn_kernel,
        grid=(T // tm,),
        in_specs=[_rows(tm, D_MODEL), lay(g), lay(w1), lay(w3), lay(w2)],
        out_specs=_rows(tm, D_MODEL),
        out_shape=jax.ShapeDtypeStruct((T, D_MODEL), F32),
        compiler_params=_params("parallel"),
        name="ffn",
    )(x2, g, w1, w3, w2)


def _block_diag(w):
    L, n, i, j = w.shape
    eye = jnp.eye(n, dtype=w.dtype)
    return jnp.einsum("lnij,nm->lnimj", w, eye).reshape(L, n * i, n * j)


def _pad_heads(w, heads, width, pad_to):
    lead = w.shape[:-1]
    w = w.reshape(lead + (heads, width))
    w = jnp.pad(w, [(0, 0)] * len(lead) + [(0, 0), (0, pad_to - width)])
    return w.reshape(lead + (heads * pad_to,))


def _rope_partner(w):
    half = MLA_ROPE // 2
    return jnp.concatenate([jnp.zeros_like(w[..., :MLA_NOPE]), w[..., MLA_NOPE + half:],
                            w[..., MLA_NOPE:MLA_NOPE + half]], axis=-1)


def _vec(v):
    return v.reshape(v.shape[0], 1, -1).astype(F32)


def kernel(x, mem, positions, norm_mix, norm_xattn, norm_mem, norm_ffn, w_in, b_gate, rwkv_mu, rwkv_w0, rwkv_w_up, rwkv_a0, rwkv_a_up, rwkv_g_up, rwkv_k_k, rwkv_k_a, rwkv_r_k, rwkv_ln_g, rwkv_ln_b, lru_conv_w, lru_conv_b, lru_wa, lru_ba, lru_wx, lru_bx, lru_lambda, mla_q_norm, mla_w_uq, mla_kv_norm, mla_w_ukv, mla_q_gain, mla_k_gain, w_branch, w_out, xa_w_q, xa_w_kv, xa_q_gain, xa_k_gain, xa_w_o, ffn_w1, ffn_w3, ffn_w2):
    B, S, D = x.shape
    T = B * S
    depth = w_in.shape[0]
    x2 = x.reshape(T, D)
    mem2 = mem.reshape(B * N_MEM, D)
    half = MLA_ROPE // 2

    inv_freq = ROPE_THETA ** (-jnp.arange(0, MLA_ROPE, 2, dtype=F32) / MLA_ROPE)
    per_row = LANES // half
    pos_rep = jnp.repeat(positions.reshape(T // per_row, per_row), half, axis=1)
    cos_c, sin_c, nsin_c = _rope_tables(pos_rep, jnp.tile(inv_freq, per_row).reshape(1, LANES))
    cos_c, sin_c, nsin_c = (t.reshape(T, half) for t in (cos_c, sin_c, nsin_c))
    ones = jnp.ones((T, MLA_NOPE), F32)
    zeros = jnp.zeros((T, MLA_NOPE), F32)
    cosf = jnp.concatenate([ones, cos_c, cos_c, ones[:, :LANES - MLA_QK]], axis=1)
    sinf = jnp.concatenate([zeros, nsin_c, sin_c, zeros[:, :LANES - MLA_QK]], axis=1)

    w_in_b = w_in[:, :, :MLA_OFF].astype(BF16)
    w_kr = w_in[:, :, MLA_OFF + Q_RANK + KV_RANK:GATE_OFF].astype(BF16)
    zero_cols = lambda n: jnp.zeros((depth, D, n), BF16)
    w_mla = jnp.concatenate(
        [w_in[:, :, MLA_OFF:MLA_OFF + Q_RANK + KV_RANK].astype(BF16), zero_cols(MLA_NOPE), w_kr,
         zero_cols(LANES - MLA_QK), zero_cols(MLA_NOPE), w_kr[:, :, half:], w_kr[:, :, :half],
         zero_cols(LANES - MLA_QK)], axis=2)
    w_gate = w_in[:, :, GATE_OFF:].astype(BF16)

    zeros_lora = jnp.zeros((depth, W_LORA, RWKV_W), F32)
    wup_pad = jnp.concatenate([rwkv_w_up, zeros_lora], axis=1).astype(BF16)
    aup_pad = jnp.concatenate([zeros_lora, rwkv_a_up], axis=1).astype(BF16)
    gup = rwkv_g_up.astype(BF16)
    r_k = rwkv_r_k.reshape(depth, RWKV_W)

    wa_bd = _block_diag(lru_wa).astype(BF16)
    wx_bd = _block_diag(lru_wx).astype(BF16)

    wuq_h = mla_w_uq.reshape(depth, Q_RANK, MLA_HEADS, MLA_QK)
    wuq2 = jnp.concatenate(
        [_pad_heads(mla_w_uq, MLA_HEADS, MLA_QK, MLA_HEAD_PAD),
         _pad_heads(_rope_partner(wuq_h).reshape(depth, Q_RANK, -1), MLA_HEADS, MLA_QK, MLA_HEAD_PAD)],
        axis=2).astype(BF16)
    wukv = mla_w_ukv.reshape(depth, KV_RANK, MLA_HEADS, MLA_NOPE + MLA_V)
    wuk = _pad_heads(wukv[..., :MLA_NOPE].reshape(depth, KV_RANK, -1), MLA_HEADS, MLA_NOPE,
                     MLA_HEAD_PAD).astype(BF16)
    wuvt = jnp.swapaxes(wukv[..., MLA_NOPE:], 1, 2).transpose(0, 1, 3, 2)
    wuvt = jnp.pad(wuvt, ((0, 0), (0, 0), (0, MLA_V_EXT - MLA_V), (0, KV_RANK)))
    wuvt = wuvt.at[:, :, MLA_V, KV_RANK].set(1.0)
    wuvt = wuvt.reshape(depth, MLA_HEADS * MLA_V_EXT, 2 * KV_RANK).astype(BF16)
    pad_gain = lambda gv: _vec(jnp.pad(gv, ((0, 0), (0, LANES - MLA_QK))))
    qg, qgs = pad_gain(mla_q_gain), pad_gain(_rope_partner(mla_q_gain))
    kg, kgs = pad_gain(mla_k_gain), pad_gain(_rope_partner(mla_k_gain))

    w_branch_b = w_branch.astype(BF16)
    w_out_b = w_out.astype(BF16)
    xa_wkv = xa_w_kv.astype(BF16)
    xa_wq = xa_w_q.astype(BF16)
    xa_wo = xa_w_o.astype(BF16)
    w1, w3, w2 = ffn_w1.astype(BF16), ffn_w3.astype(BF16), ffn_w2.astype(BF16)

    n_mix, n_xa, n_mem, n_ffn = _vec(norm_mix), _vec(norm_xattn), _vec(norm_mem), _vec(norm_ffn)
    rw = [_vec(t) for t in (rwkv_mu, rwkv_w0, rwkv_a0, rwkv_k_k, rwkv_k_a, r_k, rwkv_ln_g, rwkv_ln_b)]
    mu, w0, a0, k_k, k_a, r_kv, ln_g, ln_b = rw
    conv_b, ba, bx, lam = _vec(lru_conv_b), _vec(lru_ba), _vec(lru_bx), _vec(lru_lambda)
    q_norm, kv_norm = _vec(mla_q_norm), _vec(mla_kv_norm)
    b_gate_v, xa_qg, xa_kg = _vec(b_gate), _vec(xa_q_gain), _vec(xa_k_gain)

    lru_w = (lru_conv_w, conv_b, wa_bd, ba, wx_bd, bx, lam)
    mla_w = (q_norm, wuq2, kv_norm, wuk, wuvt, qg, qgs, kg, kgs)
    W = MLA_HEADS * MLA_HEAD_PAD
    for l in range(depth):
        p_rwkv, y_b, q, k, vt = _in_proj(x2, n_mix, w_in_b, w_mla, lru_w, mla_w, cosf, sinf, B, S, l)

        y_a = _rwkv(p_rwkv.reshape(B, S, RWKV_IN), mu, w0, wup_pad, a0, aup_pad, gup, k_k, k_a,
                    r_kv, ln_g, ln_b, l).reshape(T, RWKV_W)
        y_c = _mla_attn(q.reshape(B, S, W), k.reshape(B, S, W), vt).reshape(T, MLA_HEADS * MLA_V)

        x2 = _merge(x2, y_a, y_b, y_c, n_mix, w_gate, b_gate_v, w_branch_b, w_out_b, l)

        mk, mv = _mem_kv(mem2, n_mem, xa_wkv, xa_kg, l)
        x2 = _xattn(x2.reshape(B, S, D), mk.reshape(B, N_MEM, XA_W), mv.reshape(B, N_MEM, XA_W),
                    n_xa, xa_wq, xa_qg, xa_wo, l).reshape(T, D)

        x2 = _ffn(x2, n_ffn, w1, w3, w2, l)
    return x2.reshape(B, S, D)
```
